```python
import jax, jax.numpy as jnp
from jax import lax
import numpy as np

D_MODEL = 1024
BATCH = 2
SEQ = 16384
DEPTH = 4

GRID_W = 64
N_HEADS = 8
KV_HEADS = 2
HEAD_DIM = 64
Q_PER_KV = N_HEADS // KV_HEADS
AXIS_DIM = HEAD_DIM // 2
ROPE_THETA = 10000.0
Q_BLOCK = 128
Q_DIM = N_HEADS * HEAD_DIM
KV_DIM = KV_HEADS * HEAD_DIM
CONV_CH = D_MODEL // 2
CONV_WIDTH = 31
IN_PROJ_DIM = Q_DIM + 2 * KV_DIM + 2 * CONV_CH
MIX_DIM = Q_DIM + CONV_CH
POOL_WINDOWS = (2, 4, 8, 16)
POOL_GROUPS = len(POOL_WINDOWS)
POOL_GC = D_MODEL // POOL_GROUPS
FFN_DIM = 2816
N_EXPERTS = 8
TOP_K = 2
EXPERT_DIM = 3584
PLE_DIM = 256
EPS = 1e-6

kernel_name = "hybrid_attn_conv_pool_moe_encoder"


def rms_norm(x, g):
    xf = x.astype(jnp.float32)
    y = xf * lax.rsqrt(jnp.mean(xf * xf, axis=-1, keepdims=True) + EPS)
    return (y * g.astype(jnp.float32)).astype(x.dtype)


def layer_norm(x, g, b):
    xf = x.astype(jnp.float32)
    mu = jnp.mean(xf, axis=-1, keepdims=True)
    var = jnp.mean(jnp.square(xf - mu), axis=-1, keepdims=True)
    y = (xf - mu) * lax.rsqrt(var + EPS)
    return (y * g.astype(jnp.float32) + b.astype(jnp.float32)).astype(x.dtype)


def axial_rope_tables(seq):
    rows = seq // GRID_W
    r = jnp.broadcast_to(jnp.arange(rows, dtype=jnp.float32)[:, None], (rows, GRID_W)).reshape(seq)
    c = jnp.broadcast_to(jnp.arange(GRID_W, dtype=jnp.float32)[None, :], (rows, GRID_W)).reshape(seq)
    inv = ROPE_THETA ** (-jnp.arange(0, AXIS_DIM, 2, dtype=jnp.float32) / AXIS_DIM)
    ang = jnp.concatenate([r[:, None] * inv, c[:, None] * inv], axis=-1)
    return jnp.cos(ang)[None, :, None, :], jnp.sin(ang)[None, :, None, :]


def apply_rope(x, cos, sin):
    xf = x.astype(jnp.float32)
    half = HEAD_DIM // 2
    x1, x2 = xf[..., :half], xf[..., half:]
    return jnp.concatenate([x1 * cos - x2 * sin, x2 * cos + x1 * sin], axis=-1).astype(x.dtype)


def block_attention(q, k, v):
    b, s = q.shape[0], q.shape[1]
    nb = s // Q_BLOCK
    qb = q.reshape(b, nb, Q_BLOCK, KV_HEADS, Q_PER_KV, HEAD_DIM).transpose(1, 0, 2, 3, 4, 5)
    scale = HEAD_DIM ** -0.5

    def one_block(q_blk):
        sc = jnp.einsum('bqhgd,bkhd->bhgqk', q_blk, k).astype(jnp.float32) * scale
        pr = jax.nn.softmax(sc, axis=-1).astype(v.dtype)
        return jnp.einsum('bhgqk,bkhd->bqhgd', pr, v)

    out = lax.map(one_block, qb)
    return out.transpose(1, 0, 2, 3, 4, 5).reshape(b, s, Q_DIM)


def attn_conv_mixer(hn, w_in, q_g, k_g, conv_w, conv_b, cln_g, cln_b, w_out, cos, sin):
    b, s, _ = hn.shape
    proj = jnp.einsum('bsd,de->bse', hn, w_in)
    q = proj[..., :Q_DIM].reshape(b, s, N_HEADS, HEAD_DIM)
    k = proj[..., Q_DIM:Q_DIM + KV_DIM].reshape(b, s, KV_HEADS, HEAD_DIM)
    v = proj[..., Q_DIM + KV_DIM:Q_DIM + 2 * KV_DIM].reshape(b, s, KV_HEADS, HEAD_DIM)
    u = proj[..., Q_DIM + 2 * KV_DIM:]
    q = apply_rope(rms_norm(q, q_g), cos, sin)
    k = apply_rope(rms_norm(k, k_g), cos, sin)
    a_out = block_attention(q, k, v)
    u = u[..., :CONV_CH] * jax.nn.sigmoid(u[..., CONV_CH:])
    u = lax.conv_general_dilated(
        u, conv_w[:, None, :], window_strides=(1,),
        padding=[(CONV_WIDTH // 2, CONV_WIDTH // 2)],
        dimension_numbers=('NWC', 'WIO', 'NWC'),
        feature_group_count=CONV_CH) + conv_b
    c_out = jax.nn.silu(layer_norm(u, cln_g, cln_b))
    mix = jnp.concatenate([a_out, c_out], axis=-1)
    return jnp.einsum('bse,ed->bsd', mix, w_out)


def multiscale_pool_mixer(hn, pool_w, pool_scale):
    b, s, d = hn.shape
    hf = hn.astype(jnp.float32)
    cs = jnp.concatenate([jnp.zeros((b, 1, d), jnp.float32), jnp.cumsum(hf, axis=1)], axis=1)
    t = jnp.arange(s)
    outs = []
    for g, w in enumerate(POOL_WINDOWS):
        left = w // 2
        right = w - 1 - left
        lo = jnp.clip(t - left, 0, s - 1)
        hi = jnp.clip(t + right, 0, s - 1)
        seg = cs[..., g * POOL_GC:(g + 1) * POOL_GC]
        win_sum = seg[:, hi + 1] - seg[:, lo]
        cnt = (hi - lo + 1).astype(jnp.float32)[None, :, None]
        outs.append(win_sum / cnt - hf[..., g * POOL_GC:(g + 1) * POOL_GC])
    y = jnp.stack(outs, axis=2).astype(hn.dtype)
    y = jnp.einsum('bsgc,gcd->bsgd', y, pool_w).reshape(b, s, d)
    return y * pool_scale


def swiglu(h, wg, wu, wd):
    return jnp.einsum('bsf,fd->bsd', jax.nn.silu(jnp.einsum('bsd,df->bsf', h, wg)) * jnp.einsum('bsd,df->bsf', h, wu), wd)


def moe_swiglu(h, router_w, wg, wu, wd):
    logits = jnp.einsum('bsd,de->bse', h, router_w).astype(jnp.float32)
    top_v, top_i = lax.top_k(logits, TOP_K)
    top_p = jax.nn.softmax(top_v, axis=-1)
    gates = jnp.einsum('bske,bsk->bse', jax.nn.one_hot(top_i, N_EXPERTS, dtype=jnp.float32), top_p).astype(h.dtype)
    out = jnp.zeros_like(h)
    for e in range(N_EXPERTS):
        out = out + gates[..., e:e + 1] * swiglu(h, wg[e], wu[e], wd[e])
    return out


def setup_inputs(seed: int = 0) -> dict:
    key = jax.random.key(seed)
    ks = iter(jax.random.split(key, 40))
    n_even = (DEPTH + 1) // 2
    n_odd = DEPTH // 2
    f32 = jnp.float32

    def nrm(shape, scale):
        return jax.random.normal(next(ks), shape, f32) * scale

    def gain(shape):
        return 1.0 + 0.05 * jax.random.normal(next(ks), shape, f32)

    return {
        "x": nrm((BATCH, SEQ, D_MODEL), 1.0),
        "p": nrm((DEPTH, BATCH, SEQ, PLE_DIM), 1.0),
        "norm_mix": gain((DEPTH, D_MODEL)),
        "norm_ffn": gain((DEPTH, D_MODEL)),
        "w_in": nrm((n_even, D_MODEL, IN_PROJ_DIM), D_MODEL ** -0.5),
        "q_norm": gain((n_even, HEAD_DIM)),
        "k_norm": gain((n_even, HEAD_DIM)),
        "conv_w": nrm((n_even, CONV_WIDTH, CONV_CH), CONV_WIDTH ** -0.5),
        "conv_b": nrm((n_even, CONV_CH), 0.02),
        "conv_ln_g": gain((n_even, CONV_CH)),
        "conv_ln_b": nrm((n_even, CONV_CH), 0.02),
        "w_out": nrm((n_even, MIX_DIM, D_MODEL), MIX_DIM ** -0.5),
        "ffn_wg": nrm((n_even, D_MODEL, FFN_DIM), D_MODEL ** -0.5),
        "ffn_wu": nrm((n_even, D_MODEL, FFN_DIM), D_MODEL ** -0.5),
        "ffn_wd": nrm((n_even, FFN_DIM, D_MODEL), FFN_DIM ** -0.5),
        "pool_w": nrm((n_odd, POOL_GROUPS, POOL_GC, POOL_GC), POOL_GC ** -0.5),
        "pool_scale": gain((n_odd, D_MODEL)),
        "router_w": nrm((n_odd, D_MODEL, N_EXPERTS), D_MODEL ** -0.5),
        "moe_wg": nrm((n_odd, N_EXPERTS, D_MODEL, EXPERT_DIM), D_MODEL ** -0.5),
        "moe_wu": nrm((n_odd, N_EXPERTS, D_MODEL, EXPERT_DIM), D_MODEL ** -0.5),
        "moe_wd": nrm((n_odd, N_EXPERTS, EXPERT_DIM, D_MODEL), EXPERT_DIM ** -0.5),
        "ple_norm": gain((DEPTH, D_MODEL)),
        "ple_gate_w": nrm((DEPTH, D_MODEL, D_MODEL), D_MODEL ** -0.5),
        "ple_proj": nrm((DEPTH, PLE_DIM, D_MODEL), PLE_DIM ** -0.5),
    }


def reference(x, p, norm_mix, norm_ffn, w_in, q_norm, k_norm, conv_w, conv_b,
              conv_ln_g, conv_ln_b, w_out, ffn_wg, ffn_wu, ffn_wd, pool_w,
              pool_scale, router_w, moe_wg, moe_wu, moe_wd, ple_norm,
              ple_gate_w, ple_proj):
    cos, sin = axial_rope_tables(x.shape[1])
    h = x
    for i in range(DEPTH):
        j = i // 2
        hn = rms_norm(h, norm_mix[i])
        if i % 2 == 0:
            h = h + attn_conv_mixer(hn, w_in[j], q_norm[j], k_norm[j], conv_w[j], conv_b[j],
                                    conv_ln_g[j], conv_ln_b[j], w_out[j], cos, sin)
            h = h + swiglu(rms_norm(h, norm_ffn[i]), ffn_wg[j], ffn_wu[j], ffn_wd[j])
        else:
            h = h + multiscale_pool_mixer(hn, pool_w[j], pool_scale[j])
            h = h + moe_swiglu(rms_norm(h, norm_ffn[i]), router_w[j], moe_wg[j], moe_wu[j], moe_wd[j])
        gate = jax.nn.sigmoid(jnp.einsum('bsd,de->bse', rms_norm(h, ple_norm[i]), ple_gate_w[i]))
        h = h + gate * jnp.einsum('bsk,kd->bsd', p[i], ple_proj[i])
    return h
```

```python
import functools

import jax
import jax.numpy as jnp
from jax import lax
from jax.experimental import pallas as pl
from jax.experimental.pallas import tpu as pltpu

F32 = jnp.float32
BF16 = jnp.bfloat16

GRID_W = 64
N_HEADS = 8
KV_HEADS = 2
HEAD_DIM = 64
Q_PER_KV = N_HEADS // KV_HEADS
AXIS_DIM = HEAD_DIM // 2
ROPE_THETA = 10000.0
CONV_WIDTH = 31
POOL_WINDOWS = (2, 4, 8, 16)
N_EXPERTS = 8
EPS = 1e-6

LANES = 128
VMEM_LIMIT = 56 * 1024 * 1024
CONV_HALO = 16
POOL_HALO = 8


def _params(*sem):
    return pltpu.CompilerParams(dimension_semantics=sem, vmem_limit_bytes=VMEM_LIMIT)


def _rms(x, g):
    return x * lax.rsqrt(jnp.mean(x * x, axis=-1, keepdims=True) + EPS) * g


def _sigmoid(x):
    return 1.0 / (1.0 + jnp.exp(-x))


def _dot(a, b):
    return jnp.dot(a, b, preferred_element_type=F32)


def _inproj_kernel(h_ref, g_ref, w_ref, qg_ref, kg_ref, cos_ref, sin_ref,
                   q_ref, k_ref, v_ref, u_ref, *, q_dim, kv_dim, conv_ch):
    xn = _rms(h_ref[...], g_ref[...]).astype(BF16)
    proj = _dot(xn, w_ref[...])
    tm = proj.shape[0]
    cos = cos_ref[...]
    sin = sin_ref[...]
    lane = lax.broadcasted_iota(jnp.int32, (tm, LANES), 1)
    head0 = lane < HEAD_DIM
    first_half = (lane % HEAD_DIM) < (HEAD_DIM // 2)

    def norm_rope(x, g, scale):
        sq = x * x
        s0 = jnp.sum(jnp.where(head0, sq, 0.0), axis=-1, keepdims=True)
        s1 = jnp.sum(jnp.where(head0, 0.0, sq), axis=-1, keepdims=True)
        ms = jnp.where(head0, s0, s1) * (1.0 / HEAD_DIM)
        y = x * lax.rsqrt(ms + EPS) * g
        partner = jnp.where(first_half,
                            pltpu.roll(y, LANES - HEAD_DIM // 2, 1),
                            pltpu.roll(y, HEAD_DIM // 2, 1))
        return (y * cos + partner * sin) * scale

    for c in range(q_dim // LANES):
        x = proj[:, c * LANES:(c + 1) * LANES]
        q_ref[:, c * LANES:(c + 1) * LANES] = norm_rope(x, qg_ref[...], HEAD_DIM ** -0.5).astype(BF16)
    for c in range(kv_dim // LANES):
        x = proj[:, q_dim + c * LANES:q_dim + (c + 1) * LANES]
        kk = norm_rope(x, kg_ref[...], 1.0).astype(BF16)
        vv = proj[:, q_dim + kv_dim + c * LANES:q_dim + kv_dim + (c + 1) * LANES].astype(BF16)
        for j in range(LANES // HEAD_DIM):
            k_ref[c * (LANES // HEAD_DIM) + j] = kk[:, j * HEAD_DIM:(j + 1) * HEAD_DIM]
            v_ref[c * (LANES // HEAD_DIM) + j] = vv[:, j * HEAD_DIM:(j + 1) * HEAD_DIM]
    u0 = q_dim + 2 * kv_dim
    u_ref[...] = (proj[:, u0:u0 + conv_ch] * _sigmoid(proj[:, u0 + conv_ch:u0 + 2 * conv_ch])).astype(BF16)


def _in_proj(h, g, w_in, qg, kg, cos_t, sin_t, seq, tm):
    t, d = h.shape
    q_dim = N_HEADS * HEAD_DIM
    kv_dim = KV_HEADS * HEAD_DIM
    conv_ch = (w_in.shape[1] - q_dim - 2 * kv_dim) // 2
    n_seq_blocks = seq // tm
    kern = functools.partial(_inproj_kernel, q_dim=q_dim, kv_dim=kv_dim, conv_ch=conv_ch)
    return pl.pallas_call(
        kern,
        grid=(t // tm,),
        in_specs=[
            pl.BlockSpec((tm, d), lambda i: (i, 0)),
            pl.BlockSpec((1, d), lambda i: (0, 0)),
            pl.BlockSpec(w_in.shape, lambda i: (0, 0)),
            pl.BlockSpec((1, LANES), lambda i: (0, 0)),
            pl.BlockSpec((1, LANES), lambda i: (0, 0)),
            pl.BlockSpec((tm, LANES), lambda i: (i % n_seq_blocks, 0)),
            pl.BlockSpec((tm, LANES), lambda i: (i % n_seq_blocks, 0)),
        ],
        out_specs=[
            pl.BlockSpec((tm, q_dim), lambda i: (i, 0)),
            pl.BlockSpec((KV_HEADS, tm, HEAD_DIM), lambda i: (0, i, 0)),
            pl.BlockSpec((KV_HEADS, tm, HEAD_DIM), lambda i: (0, i, 0)),
            pl.BlockSpec((tm, conv_ch), lambda i: (i, 0)),
        ],
        out_shape=[
            jax.ShapeDtypeStruct((t, q_dim), BF16),
            jax.ShapeDtypeStruct((KV_HEADS, t, HEAD_DIM), BF16),
            jax.ShapeDtypeStruct((KV_HEADS, t, HEAD_DIM), BF16),
            jax.ShapeDtypeStruct((t, conv_ch), BF16),
        ],
        compiler_params=_params("parallel"),
        name="in_proj",
    )(h, g, w_in, qg, kg, cos_t, sin_t)


def _attn_kernel(q_ref, k_ref, v_ref, o_ref, q_sc, m_sc, l_sc, acc_sc, *, tq, tk, n_kv):
    for g in range(Q_PER_KV):
        q_sc[g * tq:(g + 1) * tq, :] = q_ref[:, g * HEAD_DIM:(g + 1) * HEAD_DIM]
    m_sc[...] = jnp.full(m_sc.shape, -jnp.inf, F32)
    l_sc[...] = jnp.zeros(l_sc.shape, F32)
    acc_sc[...] = jnp.zeros(acc_sc.shape, F32)

    def body(j, carry):
        k = k_ref[0, pl.ds(pl.multiple_of(j * tk, tk), tk), :]
        v = v_ref[0, pl.ds(pl.multiple_of(j * tk, tk), tk), :]
        s = lax.dot_general(q_sc[...], k, (((1,), (1,)), ((), ())), preferred_element_type=F32)
        m_prev = m_sc[...]
        m_new = jnp.maximum(m_prev, jnp.max(s, axis=-1, keepdims=True))
        alpha = jnp.exp(m_prev - m_new)
        p = jnp.exp(s - m_new)
        l_sc[...] = alpha * l_sc[...] + jnp.sum(p, axis=-1, keepdims=True)
        acc_sc[...] = alpha * acc_sc[...] + _dot(p.astype(BF16), v)
        m_sc[...] = m_new
        return carry

    lax.fori_loop(0, n_kv, body, 0)
    out = acc_sc[...] * (1.0 / l_sc[...])
    o_ref[...] = jnp.concatenate(
        [out[g * tq:(g + 1) * tq, :] for g in range(Q_PER_KV)], axis=-1).astype(BF16)


def _attention(q, k, v, batch, seq, tq, tk):
    t = q.shape[0]
    n_q = seq // tq
    gw = Q_PER_KV * HEAD_DIM
    kern = functools.partial(_attn_kernel, tq=tq, tk=tk, n_kv=seq // tk)
    return pl.pallas_call(
        kern,
        grid=(batch, KV_HEADS, n_q),
        in_specs=[
            pl.BlockSpec((tq, gw), lambda b, h, i: (b * n_q + i, h)),
            pl.BlockSpec((1, seq, HEAD_DIM), lambda b, h, i: (h, b, 0)),
            pl.BlockSpec((1, seq, HEAD_DIM), lambda b, h, i: (h, b, 0)),
        ],
        out_specs=pl.BlockSpec((tq, gw), lambda b, h, i: (b * n_q + i, h)),
        out_shape=jax.ShapeDtypeStruct((t, N_HEADS * HEAD_DIM), BF16),
        scratch_shapes=[
            pltpu.VMEM((Q_PER_KV * tq, HEAD_DIM), BF16),
            pltpu.VMEM((Q_PER_KV * tq, 1), F32),
            pltpu.VMEM((Q_PER_KV * tq, 1), F32),
            pltpu.VMEM((Q_PER_KV * tq, HEAD_DIM), F32),
        ],
        compiler_params=_params("parallel", "parallel", "parallel"),
        name="attention",
    )(q, k, v)


def _conv_kernel(prev_ref, main_ref, next_ref, w_ref, b_ref, g_ref, beta_ref, o_ref, ext_sc, y_sc,
                 *, tc, n_blocks):
    i = pl.program_id(1)
    ch = main_ref.shape[1]
    prev = prev_ref[...].astype(F32)
    nxt = next_ref[...].astype(F32)
    ext_sc[0:CONV_HALO, :] = jnp.where(i == 0, 0.0, prev)
    ext_sc[CONV_HALO:CONV_HALO + tc, :] = main_ref[...].astype(F32)
    ext_sc[CONV_HALO + tc:, :] = jnp.where(i == n_blocks - 1, 0.0, nxt)
    base = CONV_HALO - CONV_WIDTH // 2
    for c in range(ch // LANES):
        cols = slice(c * LANES, (c + 1) * LANES)
        acc = jnp.zeros((tc, LANES), F32) + b_ref[:, cols]
        for kk in range(CONV_WIDTH):
            acc = acc + ext_sc[base + kk:base + kk + tc, cols] * w_ref[kk:kk + 1, cols]
        y_sc[:, cols] = acc
    y = y_sc[...]
    mu = jnp.mean(y, axis=-1, keepdims=True)
    yc = y - mu
    var = jnp.mean(yc * yc, axis=-1, keepdims=True)
    z = yc * lax.rsqrt(var + EPS) * g_ref[...] + beta_ref[...]
    o_ref[...] = (z * _sigmoid(z)).astype(BF16)


def _conv_module(u, conv_w, conv_b, ln_g, ln_b, batch, seq, tc):
    t, ch = u.shape
    n_blocks = seq // tc
    hb = tc // CONV_HALO
    n_halo = seq // CONV_HALO
    kern = functools.partial(_conv_kernel, tc=tc, n_blocks=n_blocks)
    return pl.pallas_call(
        kern,
        grid=(batch, n_blocks),
        in_specs=[
            pl.BlockSpec((CONV_HALO, ch), lambda b, i: (b * n_halo + jnp.maximum(i * hb - 1, 0), 0)),
            pl.BlockSpec((tc, ch), lambda b, i: (b * n_blocks + i, 0)),
            pl.BlockSpec((CONV_HALO, ch), lambda b, i: (b * n_halo + jnp.minimum((i + 1) * hb, n_halo - 1), 0)),
            pl.BlockSpec((CONV_WIDTH, ch), lambda b, i: (0, 0)),
            pl.BlockSpec((1, ch), lambda b, i: (0, 0)),
            pl.BlockSpec((1, ch), lambda b, i: (0, 0)),
            pl.BlockSpec((1, ch), lambda b, i: (0, 0)),
        ],
        out_specs=pl.BlockSpec((tc, ch), lambda b, i: (b * n_blocks + i, 0)),
        out_shape=jax.ShapeDtypeStruct((t, ch), BF16),
        scratch_shapes=[
            pltpu.VMEM((tc + 2 * CONV_HALO, ch), F32),
            pltpu.VMEM((tc, ch), F32),
        ],
        compiler_params=_params("parallel", "parallel"),
        name="conv_module",
    )(u, u, u, conv_w, conv_b, ln_g, ln_b)


def _outproj_kernel(h_ref, a_ref, c_ref, wa_ref, wc_ref, o_ref):
    o_ref[...] = h_ref[...] + _dot(a_ref[...], wa_ref[...]) + _dot(c_ref[...], wc_ref[...])


def _out_proj(h, a, c, w_a, w_c, tm):
    t, d = h.shape
    return pl.pallas_call(
        _outproj_kernel,
        grid=(t // tm,),
        in_specs=[
            pl.BlockSpec((tm, d), lambda i: (i, 0)),
            pl.BlockSpec((tm, a.shape[1]), lambda i: (i, 0)),
            pl.BlockSpec((tm, c.shape[1]), lambda i: (i, 0)),
            pl.BlockSpec(w_a.shape, lambda i: (0, 0)),
            pl.BlockSpec(w_c.shape, lambda i: (0, 0)),
        ],
        out_specs=pl.BlockSpec((tm, d), lambda i: (i, 0)),
        out_shape=jax.ShapeDtypeStruct((t, d), F32),
        compiler_params=_params("parallel"),
        name="out_proj",
    )(h, a, c, w_a, w_c)


def _ffn_kernel(h_ref, g_ref, wg_ref, wu_ref, wd_ref, o_ref, xn_sc, acc_sc):
    f = pl.program_id(1)

    @pl.when(f == 0)
    def _():
        xn_sc[...] = _rms(h_ref[...], g_ref[...]).astype(BF16)
        acc_sc[...] = jnp.zeros(acc_sc.shape, F32)

    xn = xn_sc[...]
    a = _dot(xn, wg_ref[...])
    b = _dot(xn, wu_ref[...])
    mid = (a * _sigmoid(a) * b).astype(BF16)
    acc_sc[...] += _dot(mid, wd_ref[...])

    @pl.when(f == pl.num_programs(1) - 1)
    def _():
        o_ref[...] = h_ref[...] + acc_sc[...]


def _ffn(h, g, wg, wu, wd, tm, tf):
    t, d = h.shape
    f_dim = wg.shape[1]
    return pl.pallas_call(
        _ffn_kernel,
        grid=(t // tm, f_dim // tf),
        in_specs=[
            pl.BlockSpec((tm, d), lambda i, f: (i, 0)),
            pl.BlockSpec((1, d), lambda i, f: (0, 0)),
            pl.BlockSpec((d, tf), lambda i, f: (0, f)),
            pl.BlockSpec((d, tf), lambda i, f: (0, f)),
            pl.BlockSpec((tf, d), lambda i, f: (f, 0)),
        ],
        out_specs=pl.BlockSpec((tm, d), lambda i, f: (i, 0)),
        out_shape=jax.ShapeDtypeStruct((t, d), F32),
        scratch_shapes=[pltpu.VMEM((tm, d), BF16), pltpu.VMEM((tm, d), F32)],
        compiler_params=_params("parallel", "arbitrary"),
        name="dense_swiglu",
    )(h, g, wg, wu, wd)


def _pool_kernel(prev_ref, main_ref, next_ref, g_ref, w_ref, sc_ref, o_ref, *, tp, seq):
    i = pl.program_id(1)
    g = g_ref[...]
    h_main = main_ref[...]
    hn_main = _rms(h_main, g)
    ext = jnp.concatenate([_rms(prev_ref[...], g), hn_main, _rms(next_ref[...], g)], axis=0).astype(BF16)
    rows = tp + 2 * POOL_HALO
    t_pos = i * tp + lax.broadcasted_iota(jnp.int32, (tp, rows), 0)
    j_pos = i * tp - POOL_HALO + lax.broadcasted_iota(jnp.int32, (tp, rows), 1)
    in_seq = (j_pos >= 0) & (j_pos < seq)
    t_col = i * tp + lax.broadcasted_iota(jnp.int32, (tp, 1), 0)
    gc = w_ref.shape[1]
    for gi, win in enumerate(POOL_WINDOWS):
        left = win // 2
        right = win - 1 - left
        band = (in_seq & (j_pos >= t_pos - left) & (j_pos <= t_pos + right)).astype(BF16)
        cnt = jnp.minimum(t_col + right, seq - 1) - jnp.maximum(t_col - left, 0) + 1
        cols = slice(gi * gc, (gi + 1) * gc)
        win_sum = _dot(band, ext[:, cols])
        y = (win_sum / cnt.astype(F32) - hn_main[:, cols]).astype(BF16)
        o_ref[:, cols] = h_main[:, cols] + _dot(y, w_ref[gi]) * sc_ref[:, cols]


def _pool_mixer(h, g, pool_w, pool_scale, batch, seq, tp):
    t, d = h.shape
    n_blocks = seq // tp
    hb = tp // POOL_HALO
    n_halo = seq // POOL_HALO
    kern = functools.partial(_pool_kernel, tp=tp, seq=seq)
    return pl.pallas_call(
        kern,
        grid=(batch, n_blocks),
        in_specs=[
            pl.BlockSpec((POOL_HALO, d), lambda b, i: (b * n_halo + jnp.maximum(i * hb - 1, 0), 0)),
            pl.BlockSpec((tp, d), lambda b, i: (b * n_blocks + i, 0)),
            pl.BlockSpec((POOL_HALO, d), lambda b, i: (b * n_halo + jnp.minimum((i + 1) * hb, n_halo - 1), 0)),
            pl.BlockSpec((1, d), lambda b, i: (0, 0)),
            pl.BlockSpec(pool_w.shape, lambda b, i: (0, 0, 0)),
            pl.BlockSpec((1, d), lambda b, i: (0, 0)),
        ],
        out_specs=pl.BlockSpec((tp, d), lambda b, i: (b * n_blocks + i, 0)),
        out_shape=jax.ShapeDtypeStruct((t, d), F32),
        compiler_params=_params("parallel", "parallel"),
        name="pool_mixer",
    )(h, h, h, g, pool_w, pool_scale)


def _router_kernel(h_ref, g_ref, wr_ref, xn_ref, gates_ref):
    xn = _rms(h_ref[...], g_ref[...])
    xn_ref[...] = xn.astype(BF16)
    logits = jnp.dot(xn, wr_ref[...], preferred_element_type=F32, precision=lax.Precision.HIGHEST)
    lane = lax.broadcasted_iota(jnp.int32, logits.shape, 1)
    logits = jnp.where(lane < N_EXPERTS, logits, -jnp.inf)
    v1 = jnp.max(logits, axis=-1, keepdims=True)
    i1 = jnp.min(jnp.where(logits == v1, lane, LANES), axis=-1, keepdims=True)
    rest = jnp.where(lane == i1, -jnp.inf, logits)
    v2 = jnp.max(rest, axis=-1, keepdims=True)
    i2 = jnp.min(jnp.where(rest == v2, lane, LANES), axis=-1, keepdims=True)
    e2 = jnp.exp(v2 - v1)
    p1 = 1.0 / (1.0 + e2)
    gates_ref[...] = jnp.where(lane == i1, p1, 0.0) + jnp.where(lane == i2, e2 * p1, 0.0)


def _router(h, g, wr_pad, tm):
    t, d = h.shape
    return pl.pallas_call(
        _router_kernel,
        grid=(t // tm,),
        in_specs=[
            pl.BlockSpec((tm, d), lambda i: (i, 0)),
            pl.BlockSpec((1, d), lambda i: (0, 0)),
            pl.BlockSpec(wr_pad.shape, lambda i: (0, 0)),
        ],
        out_specs=[
            pl.BlockSpec((tm, d), lambda i: (i, 0)),
            pl.BlockSpec((tm, LANES), lambda i: (i, 0)),
        ],
        out_shape=[
            jax.ShapeDtypeStruct((t, d), BF16),
            jax.ShapeDtypeStruct((t, LANES), F32),
        ],
        compiler_params=_params("parallel"),
        name="router",
    )(h, g, wr_pad)


def _moe_kernel(h_ref, xn_ref, gates_ref, wg_ref, wu_ref, wd_ref, o_ref, acc_sc):
    e = pl.program_id(1)
    f = pl.program_id(2)

    @pl.when((e == 0) & (f == 0))
    def _():
        acc_sc[...] = jnp.zeros(acc_sc.shape, F32)

    gates = gates_ref[...]
    lane = lax.broadcasted_iota(jnp.int32, gates.shape, 1)
    gate = jnp.sum(jnp.where(lane == e, gates, 0.0), axis=-1, keepdims=True)
    xn = xn_ref[...]
    a = _dot(xn, wg_ref[0])
    b = _dot(xn, wu_ref[0])
    mid = (a * _sigmoid(a) * b).astype(BF16)
    acc_sc[...] += gate * _dot(mid, wd_ref[0])

    @pl.when((e == pl.num_programs(1) - 1) & (f == pl.num_programs(2) - 1))
    def _():
        o_ref[...] = h_ref[...] + acc_sc[...]


def _moe(h, xn, gates, wg, wu, wd, tm, tf):
    t, d = h.shape
    n_e, _, f_dim = wg.shape
    return pl.pallas_call(
        _moe_kernel,
        grid=(t // tm, n_e, f_dim // tf),
        in_specs=[
            pl.BlockSpec((tm, d), lambda i, e, f: (i, 0)),
            pl.BlockSpec((tm, d), lambda i, e, f: (i, 0)),
            pl.BlockSpec((tm, LANES), lambda i, e, f: (i, 0)),
            pl.BlockSpec((1, d, tf), lambda i, e, f: (e, 0, f)),
            pl.BlockSpec((1, d, tf), lambda i, e, f: (e, 0, f)),
            pl.BlockSpec((1, tf, d), lambda i, e, f: (e, f, 0)),
        ],
        out_specs=pl.BlockSpec((tm, d), lambda i, e, f: (i, 0)),
        out_shape=jax.ShapeDtypeStruct((t, d), F32),
        scratch_shapes=[pltpu.VMEM((tm, d), F32)],
        compiler_params=_params("parallel", "arbitrary", "arbitrary"),
        name="moe_swiglu",
    )(h, xn, gates, wg, wu, wd)


def _ple_kernel(h_ref, p_ref, g_ref, wgate_ref, wproj_ref, o_ref):
    h = h_ref[...]
    gate = _sigmoid(_dot(_rms(h, g_ref[...]).astype(BF16), wgate_ref[...]))
    o_ref[...] = h + gate * _dot(p_ref[...].astype(BF16), wproj_ref[...])


def _ple(h, p, g, w_gate, w_proj, tm):
    t, d = h.shape
    return pl.pallas_call(
        _ple_kernel,
        grid=(t // tm,),
        in_specs=[
            pl.BlockSpec((tm, d), lambda i: (i, 0)),
            pl.BlockSpec((tm, p.shape[1]), lambda i: (i, 0)),
            pl.BlockSpec((1, d), lambda i: (0, 0)),
            pl.BlockSpec(w_gate.shape, lambda i: (0, 0)),
            pl.BlockSpec(w_proj.shape, lambda i: (0, 0)),
        ],
        out_specs=pl.BlockSpec((tm, d), lambda i: (i, 0)),
        out_shape=jax.ShapeDtypeStruct((t, d), F32),
        compiler_params=_params("parallel"),
        name="per_layer_input",
    )(h, p, g, w_gate, w_proj)


def _rope_tables(seq):
    rows = seq // GRID_W
    r = jnp.broadcast_to(jnp.arange(rows, dtype=F32)[:, None], (rows, GRID_W)).reshape(seq)
    c = jnp.broadcast_to(jnp.arange(GRID_W, dtype=F32)[None, :], (rows, GRID_W)).reshape(seq)
    inv = ROPE_THETA ** (-jnp.arange(0, AXIS_DIM, 2, dtype=F32) / AXIS_DIM)
    ang = jnp.concatenate([r[:, None] * inv, c[:, None] * inv], axis=-1)
    cos, sin = jnp.cos(ang), jnp.sin(ang)
    reps = LANES // HEAD_DIM
    return (jnp.tile(jnp.concatenate([cos, cos], axis=-1), (1, reps)),
            jnp.tile(jnp.concatenate([-sin, sin], axis=-1), (1, reps)))


def _tile(n, want):
    t = min(n, want)
    assert n % t == 0, (n, t)
    return t


def kernel(x, p, norm_mix, norm_ffn, w_in, q_norm, k_norm, conv_w, conv_b, conv_ln_g, conv_ln_b, w_out,
           ffn_wg, ffn_wu, ffn_wd, pool_w, pool_scale, router_w, moe_wg, moe_wu, moe_wd, ple_norm,
           ple_gate_w, ple_proj):
    batch, seq, d = x.shape
    depth = p.shape[0]
    t = batch * seq
    q_dim = N_HEADS * HEAD_DIM
    assert seq % GRID_W == 0 and d % LANES == 0

    tm = _tile(seq, 512)
    tm_ffn = _tile(t, 1024)
    tq = _tile(seq, 256)
    tk = _tile(seq, 512)
    tc = _tile(seq, 256)
    tp = _tile(seq, 256)
    tf_ffn = ffn_wg.shape[2] // 2
    tf_moe = moe_wg.shape[3] // 2

    cos_t, sin_t = _rope_tables(seq)
    row = lambda v: v.reshape(1, -1)
    tile_heads = lambda v: jnp.tile(v, LANES // HEAD_DIM).reshape(1, LANES)

    h = x.reshape(t, d)
    for i in range(depth):
        j = i // 2
        if i % 2 == 0:
            q, k, v, u = _in_proj(h, row(norm_mix[i]), w_in[j].astype(BF16), tile_heads(q_norm[j]),
                                  tile_heads(k_norm[j]), cos_t, sin_t, seq, tm)
            a = _attention(q, k, v, batch, seq, tq, tk)
            c = _conv_module(u, conv_w[j], row(conv_b[j]), row(conv_ln_g[j]), row(conv_ln_b[j]),
                             batch, seq, tc)
            wo = w_out[j].astype(BF16)
            h = _out_proj(h, a, c, wo[:q_dim], wo[q_dim:], tm)
            h = _ffn(h, row(norm_ffn[i]), ffn_wg[j].astype(BF16), ffn_wu[j].astype(BF16),
                     ffn_wd[j].astype(BF16), tm_ffn, tf_ffn)
        else:
            h = _pool_mixer(h, row(norm_mix[i]), pool_w[j].astype(BF16), row(pool_scale[j]),
                            batch, seq, tp)
            wr_pad = jnp.pad(router_w[j], ((0, 0), (0, LANES - N_EXPERTS)))
            xn, gates = _router(h, row(norm_ffn[i]), wr_pad, tm)
            h = _moe(h, xn, gates, moe_wg[j].astype(BF16), moe_wu[j].astype(BF16),
                     moe_wd[j].astype(BF16), tm_ffn, tf_moe)
        h = _ple(h, p[i].reshape(t, -1), row(ple_norm[i]), ple_gate_w[i].astype(BF16),
                 ple_proj[i].astype(BF16), tm)
    return h.reshape(batch, seq, d)
```

```python
import functools

import jax
import jax.numpy as jnp
from jax import lax
from jax.experimental import pallas as pl
from jax.experimental.pallas import tpu as pltpu

F32 = jnp.float32
BF16 = jnp.bfloat16

GRID_W = 64
N_HEADS = 8
KV_HEADS = 2
HEAD_DIM = 64
Q_PER_KV = N_HEADS // KV_HEADS
AXIS_DIM = HEAD_DIM // 2
ROPE_THETA = 10000.0
CONV_WIDTH = 31
POOL_WINDOWS = (2, 4, 8, 16)
N_EXPERTS = 8
EPS = 1e-6

LANES = 128
VMEM_LIMIT = 56 * 1024 * 1024
BF16_SUBLANES = 16
VT_ROWS = HEAD_DIM + BF16_SUBLANES
LOG2E = 1.4426950408889634
Q_SCALE = HEAD_DIM ** -0.5 * LOG2E
MAX_UNSHIFTED_LOGIT = 80.0
CONV_HALO = 16
POOL_HALO = 8


def _params(*sem):
    return pltpu.CompilerParams(dimension_semantics=sem, vmem_limit_bytes=VMEM_LIMIT)


def _rms(x, g):
    return x * lax.rsqrt(jnp.mean(x * x, axis=-1, keepdims=True) + EPS) * g


def _sigmoid(x):
    return 1.0 / (1.0 + jnp.exp(-x))


def _dot(a, b):
    return jnp.dot(a, b, preferred_element_type=F32)


def _inproj_kernel(h_ref, g_ref, w_ref, qg_ref, kg_ref, cos_ref, sin_ref,
                   q_ref, k_ref, vt_ref, u_ref, *, q_dim, kv_dim, conv_ch):
    xn = _rms(h_ref[...], g_ref[...]).astype(BF16)
    proj = _dot(xn, w_ref[...])
    tm = proj.shape[0]
    cos = cos_ref[...]
    sin = sin_ref[...]
    lane = lax.broadcasted_iota(jnp.int32, (tm, LANES), 1)
    head0 = lane < HEAD_DIM
    first_half = (lane % HEAD_DIM) < (HEAD_DIM // 2)

    def norm_rope(x, g, scale):
        sq = x * x
        s0 = jnp.sum(jnp.where(head0, sq, 0.0), axis=-1, keepdims=True)
        s1 = jnp.sum(jnp.where(head0, 0.0, sq), axis=-1, keepdims=True)
        ms = jnp.where(head0, s0, s1) * (1.0 / HEAD_DIM)
        y = x * lax.rsqrt(ms + EPS) * g
        partner = jnp.where(first_half,
                            pltpu.roll(y, LANES - HEAD_DIM // 2, 1),
                            pltpu.roll(y, HEAD_DIM // 2, 1))
        return (y * cos + partner * sin) * scale

    for c in range(q_dim // LANES):
        x = proj[:, c * LANES:(c + 1) * LANES]
        q_ref[:, c * LANES:(c + 1) * LANES] = norm_rope(x, qg_ref[...], Q_SCALE).astype(BF16)
    sub = lax.broadcasted_iota(jnp.int32, (VT_ROWS - HEAD_DIM, tm), 0)
    ones_rows = jnp.where(sub == 0, 1.0, 0.0).astype(BF16)
    for c in range(kv_dim // LANES):
        x = proj[:, q_dim + c * LANES:q_dim + (c + 1) * LANES]
        kk = norm_rope(x, kg_ref[...], 1.0).astype(BF16)
        vv_t = proj[:, q_dim + kv_dim + c * LANES:q_dim + kv_dim + (c + 1) * LANES].T
        for j in range(LANES // HEAD_DIM):
            head = c * (LANES // HEAD_DIM) + j
            k_ref[head] = kk[:, j * HEAD_DIM:(j + 1) * HEAD_DIM]
            vt_ref[head, 0:HEAD_DIM, :] = vv_t[j * HEAD_DIM:(j + 1) * HEAD_DIM, :].astype(BF16)
            vt_ref[head, HEAD_DIM:VT_ROWS, :] = ones_rows
    u0 = q_dim + 2 * kv_dim
    u_ref[...] = (proj[:, u0:u0 + conv_ch] * _sigmoid(proj[:, u0 + conv_ch:u0 + 2 * conv_ch])).astype(BF16)


def _in_proj(h, g, w_in, qg, kg, cos_t, sin_t, seq, tm):
    t, d = h.shape
    q_dim = N_HEADS * HEAD_DIM
    kv_dim = KV_HEADS * HEAD_DIM
    conv_ch = (w_in.shape[1] - q_dim - 2 * kv_dim) // 2
    n_seq_blocks = seq // tm
    kern = functools.partial(_inproj_kernel, q_dim=q_dim, kv_dim=kv_dim, conv_ch=conv_ch)
    return pl.pallas_call(
        kern,
        grid=(t // tm,),
        in_specs=[
            pl.BlockSpec((tm, d), lambda i: (i, 0)),
            pl.BlockSpec((1, d), lambda i: (0, 0)),
            pl.BlockSpec(w_in.shape, lambda i: (0, 0)),
            pl.BlockSpec((1, LANES), lambda i: (0, 0)),
            pl.BlockSpec((1, LANES), lambda i: (0, 0)),
            pl.BlockSpec((tm, LANES), lambda i: (i % n_seq_blocks, 0)),
            pl.BlockSpec((tm, LANES), lambda i: (i % n_seq_blocks, 0)),
        ],
        out_specs=[
            pl.BlockSpec((tm, q_dim), lambda i: (i, 0)),
            pl.BlockSpec((KV_HEADS, tm, HEAD_DIM), lambda i: (0, i, 0)),
            pl.BlockSpec((KV_HEADS, VT_ROWS, tm), lambda i: (0, 0, i)),
            pl.BlockSpec((tm, conv_ch), lambda i: (i, 0)),
        ],
        out_shape=[
            jax.ShapeDtypeStruct((t, q_dim), BF16),
            jax.ShapeDtypeStruct((KV_HEADS, t, HEAD_DIM), BF16),
            jax.ShapeDtypeStruct((KV_HEADS, VT_ROWS, t), BF16),
            jax.ShapeDtypeStruct((t, conv_ch), BF16),
        ],
        compiler_params=_params("parallel"),
        name="in_proj",
    )(h, g, w_in, qg, kg, cos_t, sin_t)


def _stack_query_heads(q_ref, q_sc, tq):
    for g in range(Q_PER_KV):
        q_sc[g * tq:(g + 1) * tq, :] = q_ref[:, g * HEAD_DIM:(g + 1) * HEAD_DIM]


def _unstack_query_heads(out, tq):
    return jnp.concatenate([out[g * tq:(g + 1) * tq, :] for g in range(Q_PER_KV)], axis=-1).astype(BF16)


_NT = (((1,), (1,)), ((), ()))


def _attn_unshifted_kernel(q_ref, k_ref, vt_ref, o_ref, q_sc, acc_sc, *, tq, tk, n_kv):
    _stack_query_heads(q_ref, q_sc, tq)
    acc_sc[...] = jnp.zeros(acc_sc.shape, F32)

    def body(j, carry):
        kv0 = pl.multiple_of(j * tk, tk)
        s_t = lax.dot_general(k_ref[0, pl.ds(kv0, tk), :], q_sc[...], _NT,
                              preferred_element_type=F32)
        p_t = jnp.exp2(s_t).astype(BF16)
        acc_sc[...] += _dot(vt_ref[0, :, pl.ds(kv0, tk)], p_t)
        return carry

    lax.fori_loop(0, n_kv, body, 0)
    acc = acc_sc[...]
    out_t = acc[:HEAD_DIM, :] / acc[HEAD_DIM:HEAD_DIM + 1, :]
    o_ref[...] = _unstack_query_heads(out_t.T, tq)


def _attn_online_kernel(q_ref, k_ref, vt_ref, o_ref, q_sc, m_sc, l_sc, acc_sc, *, tq, tk, n_kv):
    _stack_query_heads(q_ref, q_sc, tq)
    m_sc[...] = jnp.full(m_sc.shape, -jnp.inf, F32)
    l_sc[...] = jnp.zeros(l_sc.shape, F32)
    acc_sc[...] = jnp.zeros(acc_sc.shape, F32)

    def body(j, carry):
        kv0 = pl.multiple_of(j * tk, tk)
        s = lax.dot_general(q_sc[...], k_ref[0, pl.ds(kv0, tk), :], _NT, preferred_element_type=F32)
        m_prev = m_sc[...]
        m_new = jnp.maximum(m_prev, jnp.max(s, axis=-1, keepdims=True))
        alpha = jnp.exp2(m_prev - m_new)
        p = jnp.exp2(s - m_new)
        l_sc[...] = alpha * l_sc[...] + jnp.sum(p, axis=-1, keepdims=True)
        v_t = vt_ref[0, 0:HEAD_DIM, pl.ds(kv0, tk)]
        acc_sc[...] = alpha * acc_sc[...] + lax.dot_general(p.astype(BF16), v_t, _NT,
                                                            preferred_element_type=F32)
        m_sc[...] = m_new
        return carry

    lax.fori_loop(0, n_kv, body, 0)
    o_ref[...] = _unstack_query_heads(acc_sc[...] / l_sc[...], tq)


def _attention_call(kern, scratch, name, q, k, vt, batch, seq, tq, tk):
    t = q.shape[0]
    n_q = seq // tq
    gw = Q_PER_KV * HEAD_DIM
    return pl.pallas_call(
        functools.partial(kern, tq=tq, tk=tk, n_kv=seq // tk),
        grid=(batch, KV_HEADS, n_q),
        in_specs=[
            pl.BlockSpec((tq, gw), lambda b, h, i: (b * n_q + i, h)),
            pl.BlockSpec((1, seq, HEAD_DIM), lambda b, h, i: (h, b, 0)),
            pl.BlockSpec((1, VT_ROWS, seq), lambda b, h, i: (h, 0, b)),
        ],
        out_specs=pl.BlockSpec((tq, gw), lambda b, h, i: (b * n_q + i, h)),
        out_shape=jax.ShapeDtypeStruct((t, N_HEADS * HEAD_DIM), BF16),
        scratch_shapes=[pltpu.VMEM((Q_PER_KV * tq, HEAD_DIM), BF16)] + scratch,
        compiler_params=_params("parallel", "parallel", "parallel"),
        name=name,
    )(q, k, vt)


def _attention(q, k, vt, logit_bound, batch, seq, tq, tk, tk_online):
    m = Q_PER_KV * tq

    def unshifted(q, k, vt):
        return _attention_call(_attn_unshifted_kernel, [pltpu.VMEM((VT_ROWS, m), F32)],
                               "attention", q, k, vt, batch, seq, tq, tk)

    def online(q, k, vt):
        scratch = [pltpu.VMEM((m, 1), F32), pltpu.VMEM((m, 1), F32), pltpu.VMEM((m, HEAD_DIM), F32)]
        return _attention_call(_attn_online_kernel, scratch, "attention_online",
                               q, k, vt, batch, seq, tq, tk_online)

    return lax.cond(logit_bound <= MAX_UNSHIFTED_LOGIT, unshifted, online, q, k, vt)


def _conv_kernel(prev_ref, main_ref, next_ref, w_ref, b_ref, g_ref, beta_ref, o_ref, ext_sc, y_sc,
                 *, tc, n_blocks):
    i = pl.program_id(1)
    ch = main_ref.shape[1]
    prev = prev_ref[...].astype(F32)
    nxt = next_ref[...].astype(F32)
    ext_sc[0:CONV_HALO, :] = jnp.where(i == 0, 0.0, prev)
    ext_sc[CONV_HALO:CONV_HALO + tc, :] = main_ref[...].astype(F32)
    ext_sc[CONV_HALO + tc:, :] = jnp.where(i == n_blocks - 1, 0.0, nxt)
    base = CONV_HALO - CONV_WIDTH // 2
    for c in range(ch // LANES):
        cols = slice(c * LANES, (c + 1) * LANES)
        acc = jnp.zeros((tc, LANES), F32) + b_ref[:, cols]
        for kk in range(CONV_WIDTH):
            acc = acc + ext_sc[base + kk:base + kk + tc, cols] * w_ref[kk:kk + 1, cols]
        y_sc[:, cols] = acc
    y = y_sc[...]
    mu = jnp.mean(y, axis=-1, keepdims=True)
    yc = y - mu
    var = jnp.mean(yc * yc, axis=-1, keepdims=True)
    z = yc * lax.rsqrt(var + EPS) * g_ref[...] + beta_ref[...]
    o_ref[...] = (z * _sigmoid(z)).astype(BF16)


def _conv_module(u, conv_w, conv_b, ln_g, ln_b, batch, seq, tc):
    t, ch = u.shape
    n_blocks = seq // tc
    hb = tc // CONV_HALO
    n_halo = seq // CONV_HALO
    kern = functools.partial(_conv_kernel, tc=tc, n_blocks=n_blocks)
    return pl.pallas_call(
        kern,
        grid=(batch, n_blocks),
        in_specs=[
            pl.BlockSpec((CONV_HALO, ch), lambda b, i: (b * n_halo + jnp.maximum(i * hb - 1, 0), 0)),
            pl.BlockSpec((tc, ch), lambda b, i: (b * n_blocks + i, 0)),
            pl.BlockSpec((CONV_HALO, ch), lambda b, i: (b * n_halo + jnp.minimum((i + 1) * hb, n_halo - 1), 0)),
            pl.BlockSpec((CONV_WIDTH, ch), lambda b, i: (0, 0)),
            pl.BlockSpec((1, ch), lambda b, i: (0, 0)),
            pl.BlockSpec((1, ch), lambda b, i: (0, 0)),
            pl.BlockSpec((1, ch), lambda b, i: (0, 0)),
        ],
        out_specs=pl.BlockSpec((tc, ch), lambda b, i: (b * n_blocks + i, 0)),
        out_shape=jax.ShapeDtypeStruct((t, ch), BF16),
        scratch_shapes=[
            pltpu.VMEM((tc + 2 * CONV_HALO, ch), F32),
            pltpu.VMEM((tc, ch), F32),
        ],
        compiler_params=_params("parallel", "parallel"),
        name="conv_module",
    )(u, u, u, conv_w, conv_b, ln_g, ln_b)


def _outproj_kernel(h_ref, a_ref, c_ref, wa_ref, wc_ref, o_ref):
    o_ref[...] = h_ref[...] + _dot(a_ref[...], wa_ref[...]) + _dot(c_ref[...], wc_ref[...])


def _out_proj(h, a, c, w_a, w_c, tm):
    t, d = h.shape
    return pl.pallas_call(
        _outproj_kernel,
        grid=(t // tm,),
        in_specs=[
            pl.BlockSpec((tm, d), lambda i: (i, 0)),
            pl.BlockSpec((tm, a.shape[1]), lambda i: (i, 0)),
            pl.BlockSpec((tm, c.shape[1]), lambda i: (i, 0)),
            pl.BlockSpec(w_a.shape, lambda i: (0, 0)),
            pl.BlockSpec(w_c.shape, lambda i: (0, 0)),
        ],
        out_specs=pl.BlockSpec((tm, d), lambda i: (i, 0)),
        out_shape=jax.ShapeDtypeStruct((t, d), F32),
        compiler_params=_params("parallel"),
        name="out_proj",
    )(h, a, c, w_a, w_c)


def _ffn_kernel(h_ref, g_ref, wg_ref, wu_ref, wd_ref, o_ref, xn_sc, acc_sc):
    f = pl.program_id(1)

    @pl.when(f == 0)
    def _():
        xn_sc[...] = _rms(h_ref[...], g_ref[...]).astype(BF16)
        acc_sc[...] = jnp.zeros(acc_sc.shape, F32)

    xn = xn_sc[...]
    a = _dot(xn, wg_ref[...])
    b = _dot(xn, wu_ref[...])
    mid = (a * _sigmoid(a) * b).astype(BF16)
    acc_sc[...] += _dot(mid, wd_ref[...])

    @pl.when(f == pl.num_programs(1) - 1)
    def _():
        o_ref[...] = h_ref[...] + acc_sc[...]


def _ffn(h, g, wg, wu, wd, tm, tf):
    t, d = h.shape
    f_dim = wg.shape[1]
    return pl.pallas_call(
        _ffn_kernel,
        grid=(t // tm, f_dim // tf),
        in_specs=[
            pl.BlockSpec((tm, d), lambda i, f: (i, 0)),
            pl.BlockSpec((1, d), lambda i, f: (0, 0)),
            pl.BlockSpec((d, tf), lambda i, f: (0, f)),
            pl.BlockSpec((d, tf), lambda i, f: (0, f)),
            pl.BlockSpec((tf, d), lambda i, f: (f, 0)),
        ],
        out_specs=pl.BlockSpec((tm, d), lambda i, f: (i, 0)),
        out_shape=jax.ShapeDtypeStruct((t, d), F32),
        scratch_shapes=[pltpu.VMEM((tm, d), BF16), pltpu.VMEM((tm, d), F32)],
        compiler_params=_params("parallel", "arbitrary"),
        name="dense_swiglu",
    )(h, g, wg, wu, wd)


def _pool_kernel(prev_ref, main_ref, next_ref, g_ref, w_ref, sc_ref, o_ref, *, tp, seq):
    i = pl.program_id(1)
    g = g_ref[...]
    h_main = main_ref[...]
    hn_main = _rms(h_main, g)
    ext = jnp.concatenate([_rms(prev_ref[...], g), hn_main, _rms(next_ref[...], g)], axis=0).astype(BF16)
    rows = tp + 2 * POOL_HALO
    t_pos = i * tp + lax.broadcasted_iota(jnp.int32, (tp, rows), 0)
    j_pos = i * tp - POOL_HALO + lax.broadcasted_iota(jnp.int32, (tp, rows), 1)
    in_seq = (j_pos >= 0) & (j_pos < seq)
    t_col = i * tp + lax.broadcasted_iota(jnp.int32, (tp, 1), 0)
    gc = w_ref.shape[1]
    for gi, win in enumerate(POOL_WINDOWS):
        left = win // 2
        right = win - 1 - left
        band = (in_seq & (j_pos >= t_pos - left) & (j_pos <= t_pos + right)).astype(BF16)
        cnt = jnp.minimum(t_col + right, seq - 1) - jnp.maximum(t_col - left, 0) + 1
        cols = slice(gi * gc, (gi + 1) * gc)
        win_sum = _dot(band, ext[:, cols])
        y = (win_sum / cnt.astype(F32) - hn_main[:, cols]).astype(BF16)
        o_ref[:, cols] = h_main[:, cols] + _dot(y, w_ref[gi]) * sc_ref[:, cols]


def _pool_mixer(h, g, pool_w, pool_scale, batch, seq, tp):
    t, d = h.shape
    n_blocks = seq // tp
    hb = tp // POOL_HALO
    n_halo = seq // POOL_HALO
    kern = functools.partial(_pool_kernel, tp=tp, seq=seq)
    return pl.pallas_call(
        kern,
        grid=(batch, n_blocks),
        in_specs=[
            pl.BlockSpec((POOL_HALO, d), lambda b, i: (b * n_halo + jnp.maximum(i * hb - 1, 0), 0)),
            pl.BlockSpec((tp, d), lambda b, i: (b * n_blocks + i, 0)),
            pl.BlockSpec((POOL_HALO, d), lambda b, i: (b * n_halo + jnp.minimum((i + 1) * hb, n_halo - 1), 0)),
            pl.BlockSpec((1, d), lambda b, i: (0, 0)),
            pl.BlockSpec(pool_w.shape, lambda b, i: (0, 0, 0)),
            pl.BlockSpec((1, d), lambda b, i: (0, 0)),
        ],
        out_specs=pl.BlockSpec((tp, d), lambda b, i: (b * n_blocks + i, 0)),
        out_shape=jax.ShapeDtypeStruct((t, d), F32),
        compiler_params=_params("parallel", "parallel"),
        name="pool_mixer",
    )(h, h, h, g, pool_w, pool_scale)


def _router_kernel(h_ref, g_ref, wr_ref, xn_ref, gates_ref):
    xn = _rms(h_ref[...], g_ref[...])
    xn_ref[...] = xn.astype(BF16)
    logits = jnp.dot(xn, wr_ref[...], preferred_element_type=F32, precision=lax.Precision.HIGHEST)
    lane = lax.broadcasted_iota(jnp.int32, logits.shape, 1)
    logits = jnp.where(lane < N_EXPERTS, logits, -jnp.inf)
    v1 = jnp.max(logits, axis=-1, keepdims=True)
    i1 = jnp.min(jnp.where(logits == v1, lane, LANES), axis=-1, keepdims=True)
    rest = jnp.where(lane == i1, -jnp.inf, logits)
    v2 = jnp.max(rest, axis=-1, keepdims=True)
    i2 = jnp.min(jnp.where(rest == v2, lane, LANES), axis=-1, keepdims=True)
    e2 = jnp.exp(v2 - v1)
    p1 = 1.0 / (1.0 + e2)
    gates_ref[...] = jnp.where(lane == i1, p1, 0.0) + jnp.where(lane == i2, e2 * p1, 0.0)


def _router(h, g, wr_pad, tm):
    t, d = h.shape
    return pl.pallas_call(
        _router_kernel,
        grid=(t // tm,),
        in_specs=[
            pl.BlockSpec((tm, d), lambda i: (i, 0)),
            pl.BlockSpec((1, d), lambda i: (0, 0)),
            pl.BlockSpec(wr_pad.shape, lambda i: (0, 0)),
        ],
        out_specs=[
            pl.BlockSpec((tm, d), lambda i: (i, 0)),
            pl.BlockSpec((tm, LANES), lambda i: (i, 0)),
        ],
        out_shape=[
            jax.ShapeDtypeStruct((t, d), BF16),
            jax.ShapeDtypeStruct((t, LANES), F32),
        ],
        compiler_params=_params("parallel"),
        name="router",
    )(h, g, wr_pad)


def _moe_kernel(h_ref, xn_ref, gates_ref, wg_ref, wu_ref, wd_ref, o_ref, acc_sc):
    e = pl.program_id(1)
    f = pl.program_id(2)

    @pl.when((e == 0) & (f == 0))
    def _():
        acc_sc[...] = jnp.zeros(acc_sc.shape, F32)

    gates = gates_ref[...]
    lane = lax.broadcasted_iota(jnp.int32, gates.shape, 1)
    gate = jnp.sum(jnp.where(lane == e, gates, 0.0), axis=-1, keepdims=True)
    xn = xn_ref[...]
    a = _dot(xn, wg_ref[0])
    b = _dot(xn, wu_ref[0])
    mid = (a * _sigmoid(a) * b).astype(BF16)
    acc_sc[...] += gate * _dot(mid, wd_ref[0])

    @pl.when((e == pl.num_programs(1) - 1) & (f == pl.num_programs(2) - 1))
    def _():
        o_ref[...] = h_ref[...] + acc_sc[...]


def _moe(h, xn, gates, wg, wu, wd, tm, tf):
    t, d = h.shape
    n_e, _, f_dim = wg.shape
    return pl.pallas_call(
        _moe_kernel,
        grid=(t // tm, n_e, f_dim // tf),
        in_specs=[
            pl.BlockSpec((tm, d), lambda i, e, f: (i, 0)),
            pl.BlockSpec((tm, d), lambda i, e, f: (i, 0)),
            pl.BlockSpec((tm, LANES), lambda i, e, f: (i, 0)),
            pl.BlockSpec((1, d, tf), lambda i, e, f: (e, 0, f)),
            pl.BlockSpec((1, d, tf), lambda i, e, f: (e, 0, f)),
            pl.BlockSpec((1, tf, d), lambda i, e, f: (e, f, 0)),
        ],
        out_specs=pl.BlockSpec((tm, d), lambda i, e, f: (i, 0)),
        out_shape=jax.ShapeDtypeStruct((t, d), F32),
        scratch_shapes=[pltpu.VMEM((tm, d), F32)],
        compiler_params=_params("parallel", "arbitrary", "arbitrary"),
        name="moe_swiglu",
    )(h, xn, gates, wg, wu, wd)


def _ple_kernel(h_ref, p_ref, g_ref, wgate_ref, wproj_ref, o_ref):
    h = h_ref[...]
    gate = _sigmoid(_dot(_rms(h, g_ref[...]).astype(BF16), wgate_ref[...]))
    o_ref[...] = h + gate * _dot(p_ref[...].astype(BF16), wproj_ref[...])


def _ple(h, p, g, w_gate, w_proj, tm):
    t, d = h.shape
    return pl.pallas_call(
        _ple_kernel,
        grid=(t // tm,),
        in_specs=[
            pl.BlockSpec((tm, d), lambda i: (i, 0)),
            pl.BlockSpec((tm, p.shape[1]), lambda i: (i, 0)),
            pl.BlockSpec((1, d), lambda i: (0, 0)),
            pl.BlockSpec(w_gate.shape, lambda i: (0, 0)),
            pl.BlockSpec(w_proj.shape, lambda i: (0, 0)),
        ],
        out_specs=pl.BlockSpec((tm, d), lambda i: (i, 0)),
        out_shape=jax.ShapeDtypeStruct((t, d), F32),
        compiler_params=_params("parallel"),
        name="per_layer_input",
    )(h, p, g, w_gate, w_proj)


def _rope_tables(seq):
    rows = seq // GRID_W
    r = jnp.broadcast_to(jnp.arange(rows, dtype=F32)[:, None], (rows, GRID_W)).reshape(seq)
    c = jnp.broadcast_to(jnp.arange(GRID_W, dtype=F32)[None, :], (rows, GRID_W)).reshape(seq)
    inv = ROPE_THETA ** (-jnp.arange(0, AXIS_DIM, 2, dtype=F32) / AXIS_DIM)
    ang = jnp.concatenate([r[:, None] * inv, c[:, None] * inv], axis=-1)
    cos, sin = jnp.cos(ang), jnp.sin(ang)
    reps = LANES // HEAD_DIM
    return (jnp.tile(jnp.concatenate([cos, cos], axis=-1), (1, reps)),
            jnp.tile(jnp.concatenate([-sin, sin], axis=-1), (1, reps)))


def _tile(n, want):
    t = min(n, want)
    assert n % t == 0, (n, t)
    return t


def kernel(x, p, norm_mix, norm_ffn, w_in, q_norm, k_norm, conv_w, conv_b, conv_ln_g, conv_ln_b, w_out,
           ffn_wg, ffn_wu, ffn_wd, pool_w, pool_scale, router_w, moe_wg, moe_wu, moe_wd, ple_norm,
           ple_gate_w, ple_proj):
    batch, seq, d = x.shape
    depth = p.shape[0]
    t = batch * seq
    q_dim = N_HEADS * HEAD_DIM
    assert seq % GRID_W == 0 and d % LANES == 0

    tm = _tile(seq, 512)
    tm_ffn = _tile(t, 1024)
    tq = _tile(seq, 256)
    tk = _tile(seq, 2048)
    tk_online = _tile(seq, 512)
    tc = _tile(seq, 256)
    tp = _tile(seq, 256)
    tf_ffn = ffn_wg.shape[2] // 2
    tf_moe = moe_wg.shape[3] // 2

    cos_t, sin_t = _rope_tables(seq)
    row = lambda v: v.reshape(1, -1)
    tile_heads = lambda v: jnp.tile(v, LANES // HEAD_DIM).reshape(1, LANES)

    h = x.reshape(t, d)
    for i in range(depth):
        j = i // 2
        if i % 2 == 0:
            q, k, vt, u = _in_proj(h, row(norm_mix[i]), w_in[j].astype(BF16), tile_heads(q_norm[j]),
                                   tile_heads(k_norm[j]), cos_t, sin_t, seq, tm)
            logit_bound = (HEAD_DIM ** 0.5 * LOG2E) * jnp.max(jnp.abs(q_norm[j])) * jnp.max(jnp.abs(k_norm[j]))
            a = _attention(q, k, vt, logit_bound, batch, seq, tq, tk, tk_online)
            c = _conv_module(u, conv_w[j], row(conv_b[j]), row(conv_ln_g[j]), row(conv_ln_b[j]),
                             batch, seq, tc)
            wo = w_out[j].astype(BF16)
            h = _out_proj(h, a, c, wo[:q_dim], wo[q_dim:], tm)
            h = _ffn(h, row(norm_ffn[i]), ffn_wg[j].astype(BF16), ffn_wu[j].astype(BF16),
                     ffn_wd[j].astype(BF16), tm_ffn, tf_ffn)
        else:
            h = _pool_mixer(h, row(norm_mix[i]), pool_w[j].astype(BF16), row(pool_scale[j]),
                            batch, seq, tp)
            wr_pad = jnp.pad(router_w[j], ((0, 0), (0, LANES - N_EXPERTS)))
            xn, gates = _router(h, row(norm_ffn[i]), wr_pad, tm)
            h = _moe(h, xn, gates, moe_wg[j].astype(BF16), moe_wu[j].astype(BF16),
                     moe_wd[j].astype(BF16), tm_ffn, tf_moe)
        h = _ple(h, p[i].reshape(t, -1), row(ple_norm[i]), ple_gate_w[i].astype(BF16),
                 ple_proj[i].astype(BF16), tm)
    return h.reshape(batch, seq, d)
```

```python
import functools

import jax
import jax.numpy as jnp
from jax import lax
from jax.experimental import pallas as pl
from jax.experimental.pallas import tpu as pltpu

F32 = jnp.float32
BF16 = jnp.bfloat16

GRID_W = 64
N_HEADS = 8
KV_HEADS = 2
HEAD_DIM = 64
Q_PER_KV = N_HEADS // KV_HEADS
AXIS_DIM = HEAD_DIM // 2
ROPE_THETA = 10000.0
CONV_WIDTH = 31
POOL_WINDOWS = (2, 4, 8, 16)
N_EXPERTS = 8
TOP_K = 2
EPS = 1e-6

LANES = 128
VMEM_LIMIT = 56 * 1024 * 1024
BF16_SUBLANES = 16
VT_ROWS = HEAD_DIM + BF16_SUBLANES
LOG2E = 1.4426950408889634
Q_SCALE = HEAD_DIM ** -0.5 * LOG2E
MAX_UNSHIFTED_LOGIT = 80.0
CONV_HALO = 16
POOL_HALO = 8


def _params(*sem):
    return pltpu.CompilerParams(dimension_semantics=sem, vmem_limit_bytes=VMEM_LIMIT)


def _rms(x, g):
    return x * lax.rsqrt(jnp.mean(x * x, axis=-1, keepdims=True) + EPS) * g


def _sigmoid(x):
    return 1.0 / (1.0 + jnp.exp(-x))


def _dot(a, b):
    return jnp.dot(a, b, preferred_element_type=F32)


def _inproj_kernel(h_ref, g_ref, w_ref, qg_ref, kg_ref, cos_ref, sin_ref,
                   q_ref, k_ref, vt_ref, u_ref, *, q_dim, kv_dim, conv_ch):
    xn = _rms(h_ref[...], g_ref[...]).astype(BF16)
    proj = _dot(xn, w_ref[...])
    tm = proj.shape[0]
    cos = cos_ref[...]
    sin = sin_ref[...]
    lane = lax.broadcasted_iota(jnp.int32, (tm, LANES), 1)
    head0 = lane < HEAD_DIM
    first_half = (lane % HEAD_DIM) < (HEAD_DIM // 2)

    def norm_rope(x, g, scale):
        sq = x * x
        s0 = jnp.sum(jnp.where(head0, sq, 0.0), axis=-1, keepdims=True)
        s1 = jnp.sum(jnp.where(head0, 0.0, sq), axis=-1, keepdims=True)
        ms = jnp.where(head0, s0, s1) * (1.0 / HEAD_DIM)
        y = x * lax.rsqrt(ms + EPS) * g
        partner = jnp.where(first_half,
                            pltpu.roll(y, LANES - HEAD_DIM // 2, 1),
                            pltpu.roll(y, HEAD_DIM // 2, 1))
        return (y * cos + partner * sin) * scale

    for c in range(q_dim // LANES):
        x = proj[:, c * LANES:(c + 1) * LANES]
        q_ref[:, c * LANES:(c + 1) * LANES] = norm_rope(x, qg_ref[...], Q_SCALE).astype(BF16)
    sub = lax.broadcasted_iota(jnp.int32, (VT_ROWS - HEAD_DIM, tm), 0)
    ones_rows = jnp.where(sub == 0, 1.0, 0.0).astype(BF16)
    for c in range(kv_dim // LANES):
        x = proj[:, q_dim + c * LANES:q_dim + (c + 1) * LANES]
        kk = norm_rope(x, kg_ref[...], 1.0).astype(BF16)
        vv_t = proj[:, q_dim + kv_dim + c * LANES:q_dim + kv_dim + (c + 1) * LANES].T
        for j in range(LANES // HEAD_DIM):
            head = c * (LANES // HEAD_DIM) + j
            k_ref[head] = kk[:, j * HEAD_DIM:(j + 1) * HEAD_DIM]
            vt_ref[head, 0:HEAD_DIM, :] = vv_t[j * HEAD_DIM:(j + 1) * HEAD_DIM, :].astype(BF16)
            vt_ref[head, HEAD_DIM:VT_ROWS, :] = ones_rows
    u0 = q_dim + 2 * kv_dim
    u_ref[...] = (proj[:, u0:u0 + conv_ch] * _sigmoid(proj[:, u0 + conv_ch:u0 + 2 * conv_ch])).astype(BF16)


def _in_proj(h, g, w_in, qg, kg, cos_t, sin_t, seq, tm):
    t, d = h.shape
    q_dim = N_HEADS * HEAD_DIM
    kv_dim = KV_HEADS * HEAD_DIM
    conv_ch = (w_in.shape[1] - q_dim - 2 * kv_dim) // 2
    n_seq_blocks = seq // tm
    kern = functools.partial(_inproj_kernel, q_dim=q_dim, kv_dim=kv_dim, conv_ch=conv_ch)
    return pl.pallas_call(
        kern,
        grid=(t // tm,),
        in_specs=[
            pl.BlockSpec((tm, d), lambda i: (i, 0)),
            pl.BlockSpec((1, d), lambda i: (0, 0)),
            pl.BlockSpec(w_in.shape, lambda i: (0, 0)),
            pl.BlockSpec((1, LANES), lambda i: (0, 0)),
            pl.BlockSpec((1, LANES), lambda i: (0, 0)),
            pl.BlockSpec((tm, LANES), lambda i: (i % n_seq_blocks, 0)),
            pl.BlockSpec((tm, LANES), lambda i: (i % n_seq_blocks, 0)),
        ],
        out_specs=[
            pl.BlockSpec((tm, q_dim), lambda i: (i, 0)),
            pl.BlockSpec((KV_HEADS, tm, HEAD_DIM), lambda i: (0, i, 0)),
            pl.BlockSpec((KV_HEADS, VT_ROWS, tm), lambda i: (0, 0, i)),
            pl.BlockSpec((tm, conv_ch), lambda i: (i, 0)),
        ],
        out_shape=[
            jax.ShapeDtypeStruct((t, q_dim), BF16),
            jax.ShapeDtypeStruct((KV_HEADS, t, HEAD_DIM), BF16),
            jax.ShapeDtypeStruct((KV_HEADS, VT_ROWS, t), BF16),
            jax.ShapeDtypeStruct((t, conv_ch), BF16),
        ],
        compiler_params=_params("parallel"),
        name="in_proj",
    )(h, g, w_in, qg, kg, cos_t, sin_t)


def _stack_query_heads(q_ref, q_sc, tq):
    for g in range(Q_PER_KV):
        q_sc[g * tq:(g + 1) * tq, :] = q_ref[:, g * HEAD_DIM:(g + 1) * HEAD_DIM]


def _unstack_query_heads(out, tq):
    return jnp.concatenate([out[g * tq:(g + 1) * tq, :] for g in range(Q_PER_KV)], axis=-1).astype(BF16)


_NT = (((1,), (1,)), ((), ()))


def _attn_unshifted_kernel(q_ref, k_ref, vt_ref, o_ref, q_sc, acc_sc, *, tq, tk, n_kv):
    _stack_query_heads(q_ref, q_sc, tq)
    acc_sc[...] = jnp.zeros(acc_sc.shape, F32)

    def body(j, carry):
        kv0 = pl.multiple_of(j * tk, tk)
        s_t = lax.dot_general(k_ref[0, pl.ds(kv0, tk), :], q_sc[...], _NT,
                              preferred_element_type=F32)
        p_t = jnp.exp2(s_t).astype(BF16)
        acc_sc[...] += _dot(vt_ref[0, :, pl.ds(kv0, tk)], p_t)
        return carry

    lax.fori_loop(0, n_kv, body, 0)
    acc = acc_sc[...]
    out_t = acc[:HEAD_DIM, :] / acc[HEAD_DIM:HEAD_DIM + 1, :]
    o_ref[...] = _unstack_query_heads(out_t.T, tq)


def _attn_online_kernel(q_ref, k_ref, vt_ref, o_ref, q_sc, m_sc, l_sc, acc_sc, *, tq, tk, n_kv):
    _stack_query_heads(q_ref, q_sc, tq)
    m_sc[...] = jnp.full(m_sc.shape, -jnp.inf, F32)
    l_sc[...] = jnp.zeros(l_sc.shape, F32)
    acc_sc[...] = jnp.zeros(acc_sc.shape, F32)

    def body(j, carry):
        kv0 = pl.multiple_of(j * tk, tk)
        s = lax.dot_general(q_sc[...], k_ref[0, pl.ds(kv0, tk), :], _NT, preferred_element_type=F32)
        m_prev = m_sc[...]
        m_new = jnp.maximum(m_prev, jnp.max(s, axis=-1, keepdims=True))
        alpha = jnp.exp2(m_prev - m_new)
        p = jnp.exp2(s - m_new)
        l_sc[...] = alpha * l_sc[...] + jnp.sum(p, axis=-1, keepdims=True)
        v_t = vt_ref[0, 0:HEAD_DIM, pl.ds(kv0, tk)]
        acc_sc[...] = alpha * acc_sc[...] + lax.dot_general(p.astype(BF16), v_t, _NT,
                                                            preferred_element_type=F32)
        m_sc[...] = m_new
        return carry

    lax.fori_loop(0, n_kv, body, 0)
    o_ref[...] = _unstack_query_heads(acc_sc[...] / l_sc[...], tq)


def _attention_call(kern, scratch, name, q, k, vt, batch, seq, tq, tk):
    t = q.shape[0]
    n_q = seq // tq
    gw = Q_PER_KV * HEAD_DIM
    return pl.pallas_call(
        functools.partial(kern, tq=tq, tk=tk, n_kv=seq // tk),
        grid=(batch, KV_HEADS, n_q),
        in_specs=[
            pl.BlockSpec((tq, gw), lambda b, h, i: (b * n_q + i, h)),
            pl.BlockSpec((1, seq, HEAD_DIM), lambda b, h, i: (h, b, 0)),
            pl.BlockSpec((1, VT_ROWS, seq), lambda b, h, i: (h, 0, b)),
        ],
        out_specs=pl.BlockSpec((tq, gw), lambda b, h, i: (b * n_q + i, h)),
        out_shape=jax.ShapeDtypeStruct((t, N_HEADS * HEAD_DIM), BF16),
        scratch_shapes=[pltpu.VMEM((Q_PER_KV * tq, HEAD_DIM), BF16)] + scratch,
        compiler_params=_params("parallel", "parallel", "parallel"),
        name=name,
    )(q, k, vt)


def _attention(q, k, vt, logit_bound, batch, seq, tq, tk, tk_online):
    m = Q_PER_KV * tq

    def unshifted(q, k, vt):
        return _attention_call(_attn_unshifted_kernel, [pltpu.VMEM((VT_ROWS, m), F32)],
                               "attention", q, k, vt, batch, seq, tq, tk)

    def online(q, k, vt):
        scratch = [pltpu.VMEM((m, 1), F32), pltpu.VMEM((m, 1), F32), pltpu.VMEM((m, HEAD_DIM), F32)]
        return _attention_call(_attn_online_kernel, scratch, "attention_online",
                               q, k, vt, batch, seq, tq, tk_online)

    return lax.cond(logit_bound <= MAX_UNSHIFTED_LOGIT, unshifted, online, q, k, vt)


def _conv_kernel(prev_ref, main_ref, next_ref, w_ref, b_ref, g_ref, beta_ref, o_ref, ext_sc, y_sc,
                 *, tc, n_blocks):
    i = pl.program_id(1)
    ch = main_ref.shape[1]
    prev = prev_ref[...].astype(F32)
    nxt = next_ref[...].astype(F32)
    ext_sc[0:CONV_HALO, :] = jnp.where(i == 0, 0.0, prev)
    ext_sc[CONV_HALO:CONV_HALO + tc, :] = main_ref[...].astype(F32)
    ext_sc[CONV_HALO + tc:, :] = jnp.where(i == n_blocks - 1, 0.0, nxt)
    base = CONV_HALO - CONV_WIDTH // 2
    for c in range(ch // LANES):
        cols = slice(c * LANES, (c + 1) * LANES)
        acc = jnp.zeros((tc, LANES), F32) + b_ref[:, cols]
        for kk in range(CONV_WIDTH):
            acc = acc + ext_sc[base + kk:base + kk + tc, cols] * w_ref[kk:kk + 1, cols]
        y_sc[:, cols] = acc
    y = y_sc[...]
    mu = jnp.mean(y, axis=-1, keepdims=True)
    yc = y - mu
    var = jnp.mean(yc * yc, axis=-1, keepdims=True)
    z = yc * lax.rsqrt(var + EPS) * g_ref[...] + beta_ref[...]
    o_ref[...] = (z * _sigmoid(z)).astype(BF16)


def _conv_module(u, conv_w, conv_b, ln_g, ln_b, batch, seq, tc):
    t, ch = u.shape
    n_blocks = seq // tc
    hb = tc // CONV_HALO
    n_halo = seq // CONV_HALO
    kern = functools.partial(_conv_kernel, tc=tc, n_blocks=n_blocks)
    return pl.pallas_call(
        kern,
        grid=(batch, n_blocks),
        in_specs=[
            pl.BlockSpec((CONV_HALO, ch), lambda b, i: (b * n_halo + jnp.maximum(i * hb - 1, 0), 0)),
            pl.BlockSpec((tc, ch), lambda b, i: (b * n_blocks + i, 0)),
            pl.BlockSpec((CONV_HALO, ch), lambda b, i: (b * n_halo + jnp.minimum((i + 1) * hb, n_halo - 1), 0)),
            pl.BlockSpec((CONV_WIDTH, ch), lambda b, i: (0, 0)),
            pl.BlockSpec((1, ch), lambda b, i: (0, 0)),
            pl.BlockSpec((1, ch), lambda b, i: (0, 0)),
            pl.BlockSpec((1, ch), lambda b, i: (0, 0)),
        ],
        out_specs=pl.BlockSpec((tc, ch), lambda b, i: (b * n_blocks + i, 0)),
        out_shape=jax.ShapeDtypeStruct((t, ch), BF16),
        scratch_shapes=[
            pltpu.VMEM((tc + 2 * CONV_HALO, ch), F32),
            pltpu.VMEM((tc, ch), F32),
        ],
        compiler_params=_params("parallel", "parallel"),
        name="conv_module",
    )(u, u, u, conv_w, conv_b, ln_g, ln_b)


def _outproj_kernel(h_ref, a_ref, c_ref, wa_ref, wc_ref, o_ref):
    o_ref[...] = h_ref[...] + _dot(a_ref[...], wa_ref[...]) + _dot(c_ref[...], wc_ref[...])


def _out_proj(h, a, c, w_a, w_c, tm):
    t, d = h.shape
    return pl.pallas_call(
        _outproj_kernel,
        grid=(t // tm,),
        in_specs=[
            pl.BlockSpec((tm, d), lambda i: (i, 0)),
            pl.BlockSpec((tm, a.shape[1]), lambda i: (i, 0)),
            pl.BlockSpec((tm, c.shape[1]), lambda i: (i, 0)),
            pl.BlockSpec(w_a.shape, lambda i: (0, 0)),
            pl.BlockSpec(w_c.shape, lambda i: (0, 0)),
        ],
        out_specs=pl.BlockSpec((tm, d), lambda i: (i, 0)),
        out_shape=jax.ShapeDtypeStruct((t, d), F32),
        compiler_params=_params("parallel"),
        name="out_proj",
    )(h, a, c, w_a, w_c)


def _ffn_kernel(h_ref, g_ref, wg_ref, wu_ref, wd_ref, o_ref, xn_sc, acc_sc):
    f = pl.program_id(1)

    @pl.when(f == 0)
    def _():
        xn_sc[...] = _rms(h_ref[...], g_ref[...]).astype(BF16)
        acc_sc[...] = jnp.zeros(acc_sc.shape, F32)

    xn = xn_sc[...]
    a = _dot(xn, wg_ref[...])
    b = _dot(xn, wu_ref[...])
    mid = (a * _sigmoid(a) * b).astype(BF16)
    acc_sc[...] += _dot(mid, wd_ref[...])

    @pl.when(f == pl.num_programs(1) - 1)
    def _():
        o_ref[...] = h_ref[...] + acc_sc[...]


def _ffn(h, g, wg, wu, wd, tm, tf):
    t, d = h.shape
    f_dim = wg.shape[1]
    return pl.pallas_call(
        _ffn_kernel,
        grid=(t // tm, f_dim // tf),
        in_specs=[
            pl.BlockSpec((tm, d), lambda i, f: (i, 0)),
            pl.BlockSpec((1, d), lambda i, f: (0, 0)),
            pl.BlockSpec((d, tf), lambda i, f: (0, f)),
            pl.BlockSpec((d, tf), lambda i, f: (0, f)),
            pl.BlockSpec((tf, d), lambda i, f: (f, 0)),
        ],
        out_specs=pl.BlockSpec((tm, d), lambda i, f: (i, 0)),
        out_shape=jax.ShapeDtypeStruct((t, d), F32),
        scratch_shapes=[pltpu.VMEM((tm, d), BF16), pltpu.VMEM((tm, d), F32)],
        compiler_params=_params("parallel", "arbitrary"),
        name="dense_swiglu",
    )(h, g, wg, wu, wd)


def _pool_kernel(prev_ref, main_ref, next_ref, g_ref, w_ref, sc_ref, o_ref, *, tp, seq):
    i = pl.program_id(1)
    g = g_ref[...]
    h_main = main_ref[...]
    hn_main = _rms(h_main, g)
    ext = jnp.concatenate([_rms(prev_ref[...], g), hn_main, _rms(next_ref[...], g)], axis=0).astype(BF16)
    rows = tp + 2 * POOL_HALO
    t_pos = i * tp + lax.broadcasted_iota(jnp.int32, (tp, rows), 0)
    j_pos = i * tp - POOL_HALO + lax.broadcasted_iota(jnp.int32, (tp, rows), 1)
    in_seq = (j_pos >= 0) & (j_pos < seq)
    t_col = i * tp + lax.broadcasted_iota(jnp.int32, (tp, 1), 0)
    gc = w_ref.shape[1]
    for gi, win in enumerate(POOL_WINDOWS):
        left = win // 2
        right = win - 1 - left
        band = (in_seq & (j_pos >= t_pos - left) & (j_pos <= t_pos + right)).astype(BF16)
        cnt = jnp.minimum(t_col + right, seq - 1) - jnp.maximum(t_col - left, 0) + 1
        cols = slice(gi * gc, (gi + 1) * gc)
        win_sum = _dot(band, ext[:, cols])
        y = (win_sum / cnt.astype(F32) - hn_main[:, cols]).astype(BF16)
        o_ref[:, cols] = h_main[:, cols] + _dot(y, w_ref[gi]) * sc_ref[:, cols]


def _pool_mixer(h, g, pool_w, pool_scale, batch, seq, tp):
    t, d = h.shape
    n_blocks = seq // tp
    hb = tp // POOL_HALO
    n_halo = seq // POOL_HALO
    kern = functools.partial(_pool_kernel, tp=tp, seq=seq)
    return pl.pallas_call(
        kern,
        grid=(batch, n_blocks),
        in_specs=[
            pl.BlockSpec((POOL_HALO, d), lambda b, i: (b * n_halo + jnp.maximum(i * hb - 1, 0), 0)),
            pl.BlockSpec((tp, d), lambda b, i: (b * n_blocks + i, 0)),
            pl.BlockSpec((POOL_HALO, d), lambda b, i: (b * n_halo + jnp.minimum((i + 1) * hb, n_halo - 1), 0)),
            pl.BlockSpec((1, d), lambda b, i: (0, 0)),
            pl.BlockSpec(pool_w.shape, lambda b, i: (0, 0, 0)),
            pl.BlockSpec((1, d), lambda b, i: (0, 0)),
        ],
        out_specs=pl.BlockSpec((tp, d), lambda b, i: (b * n_blocks + i, 0)),
        out_shape=jax.ShapeDtypeStruct((t, d), F32),
        compiler_params=_params("parallel", "parallel"),
        name="pool_mixer",
    )(h, h, h, g, pool_w, pool_scale)


SUBLANES = 8
META_E1, META_E2, META_RANK1, META_RANK2, META_P1, META_P2 = range(6)


def _to_token_tiles(ref, x):
    tm, d = x.shape
    for a in range(d // LANES):
        ref[pl.ds(a, tm, stride=SUBLANES), :] = x[:, a * LANES:(a + 1) * LANES]


def _from_token_tiles(ref, tm, d):
    return [ref[pl.ds(a, tm, stride=SUBLANES), :] for a in range(d // LANES)]


def _router_kernel(h_ref, g_ref, wr_ref, xn_ref, meta_ref, counts_ref):
    @pl.when(pl.program_id(0) == 0)
    def _():
        counts_ref[...] = jnp.zeros(counts_ref.shape, F32)

    xn = _rms(h_ref[...], g_ref[...])
    _to_token_tiles(xn_ref, xn)
    logits = jnp.dot(xn, wr_ref[...], preferred_element_type=F32, precision=lax.Precision.HIGHEST)
    tm = logits.shape[0]
    lane = lax.broadcasted_iota(jnp.int32, logits.shape, 1)
    logits = jnp.where(lane < N_EXPERTS, logits, -jnp.inf)
    v1 = jnp.max(logits, axis=-1, keepdims=True)
    i1 = jnp.min(jnp.where(logits == v1, lane, LANES), axis=-1, keepdims=True)
    rest = jnp.where(lane == i1, -jnp.inf, logits)
    v2 = jnp.max(rest, axis=-1, keepdims=True)
    i2 = jnp.min(jnp.where(rest == v2, lane, LANES), axis=-1, keepdims=True)
    e2 = jnp.exp(v2 - v1)
    p1 = 1.0 / (1.0 + e2)
    p2 = e2 * p1
    chosen = (lane == i1) | (lane == i2)
    earlier = (lax.broadcasted_iota(jnp.int32, (tm, tm), 0) > lax.broadcasted_iota(jnp.int32, (tm, tm), 1))
    before = _dot(earlier.astype(BF16), chosen.astype(BF16)) + counts_ref[...]
    rank1 = jnp.sum(jnp.where(lane == i1, before, 0.0), axis=-1, keepdims=True)
    rank2 = jnp.sum(jnp.where(lane == i2, before, 0.0), axis=-1, keepdims=True)
    counts_ref[...] += jnp.sum(chosen.astype(F32), axis=0, keepdims=True)
    meta = jnp.zeros(logits.shape, F32)
    for col, val in ((META_E1, i1.astype(F32)), (META_E2, i2.astype(F32)), (META_RANK1, rank1),
                     (META_RANK2, rank2), (META_P1, p1), (META_P2, p2)):
        meta = jnp.where(lane == col, val, meta)
    meta_ref[...] = meta


def _router(h, g, wr_pad, tm):
    t, d = h.shape
    return pl.pallas_call(
        _router_kernel,
        grid=(t // tm,),
        in_specs=[
            pl.BlockSpec((tm, d), lambda i: (i, 0)),
            pl.BlockSpec((1, d), lambda i: (0, 0)),
            pl.BlockSpec(wr_pad.shape, lambda i: (0, 0)),
        ],
        out_specs=[
            pl.BlockSpec((tm * SUBLANES, LANES), lambda i: (i, 0)),
            pl.BlockSpec((tm, LANES), lambda i: (i, 0)),
            pl.BlockSpec((1, LANES), lambda i: (0, 0)),
        ],
        out_shape=[
            jax.ShapeDtypeStruct((t * SUBLANES, LANES), F32),
            jax.ShapeDtypeStruct((t, LANES), F32),
            jax.ShapeDtypeStruct((1, LANES), F32),
        ],
        compiler_params=_params("arbitrary"),
        name="router",
    )(h, g, wr_pad)


def _routing_tables(meta, counts, tile_rows, n_tiles):
    cnt = counts[0, :N_EXPERTS].astype(jnp.int32)
    padded = (cnt + tile_rows - 1) // tile_rows * tile_rows
    ends = jnp.cumsum(padded)
    starts = ends - padded
    pos1 = starts[meta[:, META_E1].astype(jnp.int32)] + meta[:, META_RANK1].astype(jnp.int32)
    pos2 = starts[meta[:, META_E2].astype(jnp.int32)] + meta[:, META_RANK2].astype(jnp.int32)
    n_used = ends[-1] // tile_rows
    tile_start = jnp.minimum(jnp.arange(n_tiles, dtype=jnp.int32), n_used - 1) * tile_rows
    tile_expert = jnp.sum((tile_start[:, None] >= ends[None, :]).astype(jnp.int32), axis=-1)
    return pos1, pos2, tile_expert, n_used.reshape(1)


def _token_rows(ref, i, n=1):
    return ref.at[pl.ds(pl.multiple_of(i * SUBLANES, SUBLANES), n * SUBLANES)]


def _dispatch_kernel(pos1_ref, pos2_ref, xn_hbm, zeros_hbm, xs_hbm, sem, *, chunk):
    del zeros_hbm
    base = pl.program_id(0) * chunk

    def issue(i, carry):
        tok = base + i
        src = _token_rows(xn_hbm, tok)
        pltpu.make_async_copy(src, _token_rows(xs_hbm, pos1_ref[tok]), sem).start()
        pltpu.make_async_copy(src, _token_rows(xs_hbm, pos2_ref[tok]), sem).start()
        return carry

    lax.fori_loop(0, chunk, issue, 0, unroll=8)
    pltpu.make_async_copy(_token_rows(xn_hbm, 0, 2 * chunk), _token_rows(xs_hbm, 0, 2 * chunk), sem).wait()


def _dispatch(pos1, pos2, xn_tiles, n_rows, chunk):
    t = pos1.shape[0]
    zeros = jnp.zeros((n_rows * SUBLANES, LANES), F32)
    return pl.pallas_call(
        functools.partial(_dispatch_kernel, chunk=chunk),
        grid_spec=pltpu.PrefetchScalarGridSpec(
            num_scalar_prefetch=2,
            grid=(t // chunk,),
            in_specs=[pl.BlockSpec(memory_space=pl.ANY), pl.BlockSpec(memory_space=pl.ANY)],
            out_specs=pl.BlockSpec(memory_space=pl.ANY),
            scratch_shapes=[pltpu.SemaphoreType.DMA],
        ),
        out_shape=jax.ShapeDtypeStruct(zeros.shape, F32),
        input_output_aliases={3: 0},
        compiler_params=_params("arbitrary"),
        name="moe_dispatch",
    )(pos1, pos2, xn_tiles, zeros)


def _expert_kernel(tile_expert_ref, n_used_ref, xs_ref, wg_ref, wu_ref, wd_ref, ys_ref, x_sc, acc_sc,
                   *, tm, d):
    del tile_expert_ref
    r = pl.program_id(0)
    f = pl.program_id(1)
    last = pl.num_programs(1) - 1
    used = r < n_used_ref[0]

    @pl.when(used & (f == 0))
    def _():
        x_sc[...] = jnp.concatenate(_from_token_tiles(xs_ref, tm, d), axis=-1).astype(BF16)

    @pl.when(used)
    def _():
        x = x_sc[...]
        a = _dot(x, wg_ref[0])
        b = _dot(x, wu_ref[0])
        y = _dot((a * _sigmoid(a) * b).astype(BF16), wd_ref[0])

        @pl.when(f == 0)
        def _():
            acc_sc[...] = y

        @pl.when((f > 0) & (f < last))
        def _():
            acc_sc[...] += y

        @pl.when(f == last)
        def _():
            _to_token_tiles(ys_ref, acc_sc[...] + y)

    @pl.when(jnp.logical_not(used) & (f == last))
    def _():
        ys_ref[...] = jnp.zeros(ys_ref.shape, F32)


def _experts(tile_expert, n_used, xs, wg, wu, wd, tm, tf):
    n_e, d, f_dim = wg.shape
    n_tiles = xs.shape[0] // (tm * SUBLANES)
    assert f_dim // tf >= 2

    def row_tile(r, f, te, nu):
        return (jnp.minimum(r, nu[0] - 1), 0)

    return pl.pallas_call(
        functools.partial(_expert_kernel, tm=tm, d=d),
        grid_spec=pltpu.PrefetchScalarGridSpec(
            num_scalar_prefetch=2,
            grid=(n_tiles, f_dim // tf),
            in_specs=[
                pl.BlockSpec((tm * SUBLANES, LANES), row_tile),
                pl.BlockSpec((1, d, tf), lambda r, f, te, nu: (te[r], 0, f)),
                pl.BlockSpec((1, d, tf), lambda r, f, te, nu: (te[r], 0, f)),
                pl.BlockSpec((1, tf, d), lambda r, f, te, nu: (te[r], f, 0)),
            ],
            out_specs=pl.BlockSpec((tm * SUBLANES, LANES), lambda r, f, te, nu: (r, 0)),
            scratch_shapes=[pltpu.VMEM((tm, d), BF16), pltpu.VMEM((tm, d), F32)],
        ),
        out_shape=jax.ShapeDtypeStruct(xs.shape, F32),
        compiler_params=_params("arbitrary", "arbitrary"),
        name="moe_experts",
    )(tile_expert, n_used, xs, wg, wu, wd)


def _combine_kernel(pos1_ref, pos2_ref, h_ref, meta_ref, ys_hbm, o_ref, y1_sc, y2_sc, sem, *, tm, d):
    base = pl.program_id(0) * tm

    def issue(i, carry):
        tok = base + i
        pltpu.make_async_copy(_token_rows(ys_hbm, pos1_ref[tok]), _token_rows(y1_sc, i), sem).start()
        pltpu.make_async_copy(_token_rows(ys_hbm, pos2_ref[tok]), _token_rows(y2_sc, i), sem).start()
        return carry

    lax.fori_loop(0, tm, issue, 0, unroll=8)
    pltpu.make_async_copy(_token_rows(ys_hbm, 0, tm), y1_sc, sem).wait()
    pltpu.make_async_copy(_token_rows(ys_hbm, 0, tm), y2_sc, sem).wait()
    meta = meta_ref[...]
    p1 = meta[:, META_P1:META_P1 + 1]
    p2 = meta[:, META_P2:META_P2 + 1]
    y1 = _from_token_tiles(y1_sc, tm, d)
    y2 = _from_token_tiles(y2_sc, tm, d)
    for a in range(d // LANES):
        cols = slice(a * LANES, (a + 1) * LANES)
        o_ref[:, cols] = h_ref[:, cols] + p1 * y1[a] + p2 * y2[a]


def _combine(pos1, pos2, h, meta, ys, tm):
    t, d = h.shape
    return pl.pallas_call(
        functools.partial(_combine_kernel, tm=tm, d=d),
        grid_spec=pltpu.PrefetchScalarGridSpec(
            num_scalar_prefetch=2,
            grid=(t // tm,),
            in_specs=[
                pl.BlockSpec((tm, d), lambda i, p1, p2: (i, 0)),
                pl.BlockSpec((tm, LANES), lambda i, p1, p2: (i, 0)),
                pl.BlockSpec(memory_space=pl.ANY),
            ],
            out_specs=pl.BlockSpec((tm, d), lambda i, p1, p2: (i, 0)),
            scratch_shapes=[pltpu.VMEM((tm * SUBLANES, LANES), F32), pltpu.VMEM((tm * SUBLANES, LANES), F32),
                            pltpu.SemaphoreType.DMA],
        ),
        out_shape=jax.ShapeDtypeStruct((t, d), F32),
        compiler_params=_params("arbitrary"),
        name="moe_combine",
    )(pos1, pos2, h, meta, ys)


def _ple_kernel(h_ref, p_ref, g_ref, wgate_ref, wproj_ref, o_ref):
    h = h_ref[...]
    gate = _sigmoid(_dot(_rms(h, g_ref[...]).astype(BF16), wgate_ref[...]))
    o_ref[...] = h + gate * _dot(p_ref[...].astype(BF16), wproj_ref[...])


def _ple(h, p, g, w_gate, w_proj, tm):
    t, d = h.shape
    return pl.pallas_call(
        _ple_kernel,
        grid=(t // tm,),
        in_specs=[
            pl.BlockSpec((tm, d), lambda i: (i, 0)),
            pl.BlockSpec((tm, p.shape[1]), lambda i: (i, 0)),
            pl.BlockSpec((1, d), lambda i: (0, 0)),
            pl.BlockSpec(w_gate.shape, lambda i: (0, 0)),
            pl.BlockSpec(w_proj.shape, lambda i: (0, 0)),
        ],
        out_specs=pl.BlockSpec((tm, d), lambda i: (i, 0)),
        out_shape=jax.ShapeDtypeStruct((t, d), F32),
        compiler_params=_params("parallel"),
        name="per_layer_input",
    )(h, p, g, w_gate, w_proj)


def _rope_tables(seq):
    rows = seq // GRID_W
    r = jnp.broadcast_to(jnp.arange(rows, dtype=F32)[:, None], (rows, GRID_W)).reshape(seq)
    c = jnp.broadcast_to(jnp.arange(GRID_W, dtype=F32)[None, :], (rows, GRID_W)).reshape(seq)
    inv = ROPE_THETA ** (-jnp.arange(0, AXIS_DIM, 2, dtype=F32) / AXIS_DIM)
    ang = jnp.concatenate([r[:, None] * inv, c[:, None] * inv], axis=-1)
    cos, sin = jnp.cos(ang), jnp.sin(ang)
    reps = LANES // HEAD_DIM
    return (jnp.tile(jnp.concatenate([cos, cos], axis=-1), (1, reps)),
            jnp.tile(jnp.concatenate([-sin, sin], axis=-1), (1, reps)))


def _tile(n, want):
    t = min(n, want)
    assert n % t == 0, (n, t)
    return t


def kernel(x, p, norm_mix, norm_ffn, w_in, q_norm, k_norm, conv_w, conv_b, conv_ln_g, conv_ln_b, w_out,
           ffn_wg, ffn_wu, ffn_wd, pool_w, pool_scale, router_w, moe_wg, moe_wu, moe_wd, ple_norm,
           ple_gate_w, ple_proj):
    batch, seq, d = x.shape
    depth = p.shape[0]
    t = batch * seq
    q_dim = N_HEADS * HEAD_DIM
    assert seq % GRID_W == 0 and d % LANES == 0

    tm = _tile(seq, 512)
    tm_ffn = _tile(t, 1024)
    tq = _tile(seq, 256)
    tk = _tile(seq, 2048)
    tk_online = _tile(seq, 512)
    tc = _tile(seq, 256)
    tp = _tile(seq, 256)
    tf_ffn = ffn_wg.shape[2] // 2
    tf_moe = moe_wg.shape[3] // 2
    tm_moe = _tile(t, 1024)
    n_moe_tiles = TOP_K * t // tm_moe + N_EXPERTS
    dispatch_chunk = _tile(t, 2048)

    cos_t, sin_t = _rope_tables(seq)
    row = lambda v: v.reshape(1, -1)
    tile_heads = lambda v: jnp.tile(v, LANES // HEAD_DIM).reshape(1, LANES)

    h = x.reshape(t, d)
    for i in range(depth):
        j = i // 2
        if i % 2 == 0:
            q, k, vt, u = _in_proj(h, row(norm_mix[i]), w_in[j].astype(BF16), tile_heads(q_norm[j]),
                                   tile_heads(k_norm[j]), cos_t, sin_t, seq, tm)
            logit_bound = (HEAD_DIM ** 0.5 * LOG2E) * jnp.max(jnp.abs(q_norm[j])) * jnp.max(jnp.abs(k_norm[j]))
            a = _attention(q, k, vt, logit_bound, batch, seq, tq, tk, tk_online)
            c = _conv_module(u, conv_w[j], row(conv_b[j]), row(conv_ln_g[j]), row(conv_ln_b[j]),
                             batch, seq, tc)
            wo = w_out[j].astype(BF16)
            h = _out_proj(h, a, c, wo[:q_dim], wo[q_dim:], tm)
            h = _ffn(h, row(norm_ffn[i]), ffn_wg[j].astype(BF16), ffn_wu[j].astype(BF16),
                     ffn_wd[j].astype(BF16), tm_ffn, tf_ffn)
        else:
            h = _pool_mixer(h, row(norm_mix[i]), pool_w[j].astype(BF16), row(pool_scale[j]),
                            batch, seq, tp)
            wr_pad = jnp.pad(router_w[j], ((0, 0), (0, LANES - N_EXPERTS)))
            xn_tiles, meta, counts = _router(h, row(norm_ffn[i]), wr_pad, tm)
            pos1, pos2, tile_expert, n_used = _routing_tables(meta, counts, tm_moe, n_moe_tiles)
            xs = _dispatch(pos1, pos2, xn_tiles, n_moe_tiles * tm_moe, dispatch_chunk)
            ys = _experts(tile_expert, n_used, xs, moe_wg[j].astype(BF16), moe_wu[j].astype(BF16),
                          moe_wd[j].astype(BF16), tm_moe, tf_moe)
            h = _combine(pos1, pos2, h, meta, ys, tm)
        h = _ple(h, p[i].reshape(t, -1), row(ple_norm[i]), ple_gate_w[i].astype(BF16),
                 ple_proj[i].astype(BF16), tm)
    return h.reshape(batch, seq, d)
```

```python
import functools

import jax
import jax.numpy as jnp
from jax import lax
from jax.experimental import pallas as pl
from jax.experimental.pallas import tpu as pltpu

F32 = jnp.float32
BF16 = jnp.bfloat16

GRID_W = 64
N_HEADS = 8
KV_HEADS = 2
HEAD_DIM = 64
Q_PER_KV = N_HEADS // KV_HEADS
AXIS_DIM = HEAD_DIM // 2
ROPE_THETA = 10000.0
CONV_WIDTH = 31
POOL_WINDOWS = (2, 4, 8, 16)
N_EXPERTS = 8
TOP_K = 2
EPS = 1e-6

LANES = 128
VMEM_LIMIT = 56 * 1024 * 1024
BF16_SUBLANES = 16
VT_ROWS = HEAD_DIM + BF16_SUBLANES
LOG2E = 1.4426950408889634
Q_SCALE = HEAD_DIM ** -0.5 * LOG2E
MAX_UNSHIFTED_LOGIT = 80.0
CONV_HALO = 16
POOL_HALO = 8


def _params(*sem):
    return pltpu.CompilerParams(dimension_semantics=sem, vmem_limit_bytes=VMEM_LIMIT)


def _rms(x, g):
    return x * lax.rsqrt(jnp.mean(x * x, axis=-1, keepdims=True) + EPS) * g


def _sigmoid(x):
    return 1.0 / (1.0 + jnp.exp(-x))


def _dot(a, b):
    return jnp.dot(a, b, preferred_element_type=F32)


def _inproj_kernel(h_ref, g_ref, w_ref, qg_ref, kg_ref, cos_ref, sin_ref,
                   q_ref, k_ref, vt_ref, u_ref, *, q_dim, kv_dim, conv_ch):
    xn = _rms(h_ref[...], g_ref[...]).astype(BF16)
    proj = _dot(xn, w_ref[...])
    tm = proj.shape[0]
    cos = cos_ref[...]
    sin = sin_ref[...]
    lane = lax.broadcasted_iota(jnp.int32, (tm, LANES), 1)
    head0 = lane < HEAD_DIM
    first_half = (lane % HEAD_DIM) < (HEAD_DIM // 2)

    def norm_rope(x, g, scale):
        sq = x * x
        s0 = jnp.sum(jnp.where(head0, sq, 0.0), axis=-1, keepdims=True)
        s1 = jnp.sum(jnp.where(head0, 0.0, sq), axis=-1, keepdims=True)
        ms = jnp.where(head0, s0, s1) * (1.0 / HEAD_DIM)
        y = x * lax.rsqrt(ms + EPS) * g
        partner = jnp.where(first_half,
                            pltpu.roll(y, LANES - HEAD_DIM // 2, 1),
                            pltpu.roll(y, HEAD_DIM // 2, 1))
        return (y * cos + partner * sin) * scale

    for c in range(q_dim // LANES):
        x = proj[:, c * LANES:(c + 1) * LANES]
        q_ref[:, c * LANES:(c + 1) * LANES] = norm_rope(x, qg_ref[...], Q_SCALE).astype(BF16)
    sub = lax.broadcasted_iota(jnp.int32, (VT_ROWS - HEAD_DIM, tm), 0)
    ones_rows = jnp.where(sub == 0, 1.0, 0.0).astype(BF16)
    for c in range(kv_dim // LANES):
        x = proj[:, q_dim + c * LANES:q_dim + (c + 1) * LANES]
        kk = norm_rope(x, kg_ref[...], 1.0).astype(BF16)
        vv_t = proj[:, q_dim + kv_dim + c * LANES:q_dim + kv_dim + (c + 1) * LANES].T
        for j in range(LANES // HEAD_DIM):
            head = c * (LANES // HEAD_DIM) + j
            k_ref[head] = kk[:, j * HEAD_DIM:(j + 1) * HEAD_DIM]
            vt_ref[head, 0:HEAD_DIM, :] = vv_t[j * HEAD_DIM:(j + 1) * HEAD_DIM, :].astype(BF16)
            vt_ref[head, HEAD_DIM:VT_ROWS, :] = ones_rows
    u0 = q_dim + 2 * kv_dim
    u_ref[...] = (proj[:, u0:u0 + conv_ch] * _sigmoid(proj[:, u0 + conv_ch:u0 + 2 * conv_ch])).astype(BF16)


def _in_proj(h, g, w_in, qg, kg, cos_t, sin_t, seq, tm):
    t, d = h.shape
    q_dim = N_HEADS * HEAD_DIM
    kv_dim = KV_HEADS * HEAD_DIM
    conv_ch = (w_in.shape[1] - q_dim - 2 * kv_dim) // 2
    n_seq_blocks = seq // tm
    kern = functools.partial(_inproj_kernel, q_dim=q_dim, kv_dim=kv_dim, conv_ch=conv_ch)
    return pl.pallas_call(
        kern,
        grid=(t // tm,),
        in_specs=[
            pl.BlockSpec((tm, d), lambda i: (i, 0)),
            pl.BlockSpec((1, d), lambda i: (0, 0)),
            pl.BlockSpec(w_in.shape, lambda i: (0, 0)),
            pl.BlockSpec((1, LANES), lambda i: (0, 0)),
            pl.BlockSpec((1, LANES), lambda i: (0, 0)),
            pl.BlockSpec((tm, LANES), lambda i: (i % n_seq_blocks, 0)),
            pl.BlockSpec((tm, LANES), lambda i: (i % n_seq_blocks, 0)),
        ],
        out_specs=[
            pl.BlockSpec((tm, q_dim), lambda i: (i, 0)),
            pl.BlockSpec((KV_HEADS, tm, HEAD_DIM), lambda i: (0, i, 0)),
            pl.BlockSpec((KV_HEADS, VT_ROWS, tm), lambda i: (0, 0, i)),
            pl.BlockSpec((tm, conv_ch), lambda i: (i, 0)),
        ],
        out_shape=[
            jax.ShapeDtypeStruct((t, q_dim), BF16),
            jax.ShapeDtypeStruct((KV_HEADS, t, HEAD_DIM), BF16),
            jax.ShapeDtypeStruct((KV_HEADS, VT_ROWS, t), BF16),
            jax.ShapeDtypeStruct((t, conv_ch), BF16),
        ],
        compiler_params=_params("parallel"),
        name="in_proj",
    )(h, g, w_in, qg, kg, cos_t, sin_t)


def _stack_query_heads(q_ref, q_sc, tq):
    for g in range(Q_PER_KV):
        q_sc[g * tq:(g + 1) * tq, :] = q_ref[:, g * HEAD_DIM:(g + 1) * HEAD_DIM]


def _unstack_query_heads(out, tq):
    return jnp.concatenate([out[g * tq:(g + 1) * tq, :] for g in range(Q_PER_KV)], axis=-1).astype(BF16)


_NT = (((1,), (1,)), ((), ()))


def _attn_unshifted_kernel(q_ref, k_ref, vt_ref, o_ref, q_sc, acc_sc, *, tq, tk, n_kv):
    _stack_query_heads(q_ref, q_sc, tq)
    acc_sc[...] = jnp.zeros(acc_sc.shape, F32)

    def body(j, carry):
        kv0 = pl.multiple_of(j * tk, tk)
        s_t = lax.dot_general(k_ref[0, pl.ds(kv0, tk), :], q_sc[...], _NT,
                              preferred_element_type=F32)
        p_t = jnp.exp2(s_t).astype(BF16)
        acc_sc[...] += _dot(vt_ref[0, :, pl.ds(kv0, tk)], p_t)
        return carry

    lax.fori_loop(0, n_kv, body, 0)
    acc = acc_sc[...]
    out_t = acc[:HEAD_DIM, :] / acc[HEAD_DIM:HEAD_DIM + 1, :]
    o_ref[...] = _unstack_query_heads(out_t.T, tq)


def _attn_online_kernel(q_ref, k_ref, vt_ref, o_ref, q_sc, m_sc, l_sc, acc_sc, *, tq, tk, n_kv):
    _stack_query_heads(q_ref, q_sc, tq)
    m_sc[...] = jnp.full(m_sc.shape, -jnp.inf, F32)
    l_sc[...] = jnp.zeros(l_sc.shape, F32)
    acc_sc[...] = jnp.zeros(acc_sc.shape, F32)

    def body(j, carry):
        kv0 = pl.multiple_of(j * tk, tk)
        s = lax.dot_general(q_sc[...], k_ref[0, pl.ds(kv0, tk), :], _NT, preferred_element_type=F32)
        m_prev = m_sc[...]
        m_new = jnp.maximum(m_prev, jnp.max(s, axis=-1, keepdims=True))
        alpha = jnp.exp2(m_prev - m_new)
        p = jnp.exp2(s - m_new)
        l_sc[...] = alpha * l_sc[...] + jnp.sum(p, axis=-1, keepdims=True)
        v_t = vt_ref[0, 0:HEAD_DIM, pl.ds(kv0, tk)]
        acc_sc[...] = alpha * acc_sc[...] + lax.dot_general(p.astype(BF16), v_t, _NT,
                                                            preferred_element_type=F32)
        m_sc[...] = m_new
        return carry

    lax.fori_loop(0, n_kv, body, 0)
    o_ref[...] = _unstack_query_heads(acc_sc[...] / l_sc[...], tq)


def _attention_call(kern, scratch, name, q, k, vt, batch, seq, tq, tk):
    t = q.shape[0]
    n_q = seq // tq
    gw = Q_PER_KV * HEAD_DIM
    return pl.pallas_call(
        functools.partial(kern, tq=tq, tk=tk, n_kv=seq // tk),
        grid=(batch, KV_HEADS, n_q),
        in_specs=[
            pl.BlockSpec((tq, gw), lambda b, h, i: (b * n_q + i, h)),
            pl.BlockSpec((1, seq, HEAD_DIM), lambda b, h, i: (h, b, 0)),
            pl.BlockSpec((1, VT_ROWS, seq), lambda b, h, i: (h, 0, b)),
        ],
        out_specs=pl.BlockSpec((tq, gw), lambda b, h, i: (b * n_q + i, h)),
        out_shape=jax.ShapeDtypeStruct((t, N_HEADS * HEAD_DIM), BF16),
        scratch_shapes=[pltpu.VMEM((Q_PER_KV * tq, HEAD_DIM), BF16)] + scratch,
        compiler_params=_params("parallel", "parallel", "parallel"),
        name=name,
    )(q, k, vt)


def _attention(q, k, vt, logit_bound, batch, seq, tq, tk, tk_online):
    m = Q_PER_KV * tq

    def unshifted(q, k, vt):
        return _attention_call(_attn_unshifted_kernel, [pltpu.VMEM((VT_ROWS, m), F32)],
                               "attention", q, k, vt, batch, seq, tq, tk)

    def online(q, k, vt):
        scratch = [pltpu.VMEM((m, 1), F32), pltpu.VMEM((m, 1), F32), pltpu.VMEM((m, HEAD_DIM), F32)]
        return _attention_call(_attn_online_kernel, scratch, "attention_online",
                               q, k, vt, batch, seq, tq, tk_online)

    return lax.cond(logit_bound <= MAX_UNSHIFTED_LOGIT, unshifted, online, q, k, vt)


def _conv_kernel(prev_ref, main_ref, next_ref, w_ref, b_ref, g_ref, beta_ref, o_ref, ext_sc, y_sc,
                 *, tc, n_blocks):
    i = pl.program_id(1)
    ch = main_ref.shape[1]
    prev = prev_ref[...].astype(F32)
    nxt = next_ref[...].astype(F32)
    ext_sc[0:CONV_HALO, :] = jnp.where(i == 0, 0.0, prev)
    ext_sc[CONV_HALO:CONV_HALO + tc, :] = main_ref[...].astype(F32)
    ext_sc[CONV_HALO + tc:, :] = jnp.where(i == n_blocks - 1, 0.0, nxt)
    base = CONV_HALO - CONV_WIDTH // 2
    for c in range(ch // LANES):
        cols = slice(c * LANES, (c + 1) * LANES)
        acc = jnp.zeros((tc, LANES), F32) + b_ref[:, cols]
        for kk in range(CONV_WIDTH):
            acc = acc + ext_sc[base + kk:base + kk + tc, cols] * w_ref[kk:kk + 1, cols]
        y_sc[:, cols] = acc
    y = y_sc[...]
    mu = jnp.mean(y, axis=-1, keepdims=True)
    yc = y - mu
    var = jnp.mean(yc * yc, axis=-1, keepdims=True)
    z = yc * lax.rsqrt(var + EPS) * g_ref[...] + beta_ref[...]
    o_ref[...] = (z * _sigmoid(z)).astype(BF16)


def _conv_module(u, conv_w, conv_b, ln_g, ln_b, batch, seq, tc):
    t, ch = u.shape
    n_blocks = seq // tc
    hb = tc // CONV_HALO
    n_halo = seq // CONV_HALO
    kern = functools.partial(_conv_kernel, tc=tc, n_blocks=n_blocks)
    return pl.pallas_call(
        kern,
        grid=(batch, n_blocks),
        in_specs=[
            pl.BlockSpec((CONV_HALO, ch), lambda b, i: (b * n_halo + jnp.maximum(i * hb - 1, 0), 0)),
            pl.BlockSpec((tc, ch), lambda b, i: (b * n_blocks + i, 0)),
            pl.BlockSpec((CONV_HALO, ch), lambda b, i: (b * n_halo + jnp.minimum((i + 1) * hb, n_halo - 1), 0)),
            pl.BlockSpec((CONV_WIDTH, ch), lambda b, i: (0, 0)),
            pl.BlockSpec((1, ch), lambda b, i: (0, 0)),
            pl.BlockSpec((1, ch), lambda b, i: (0, 0)),
            pl.BlockSpec((1, ch), lambda b, i: (0, 0)),
        ],
        out_specs=pl.BlockSpec((tc, ch), lambda b, i: (b * n_blocks + i, 0)),
        out_shape=jax.ShapeDtypeStruct((t, ch), BF16),
        scratch_shapes=[
            pltpu.VMEM((tc + 2 * CONV_HALO, ch), F32),
            pltpu.VMEM((tc, ch), F32),
        ],
        compiler_params=_params("parallel", "parallel"),
        name="conv_module",
    )(u, u, u, conv_w, conv_b, ln_g, ln_b)


def _outproj_kernel(h_ref, a_ref, c_ref, wa_ref, wc_ref, o_ref):
    o_ref[...] = h_ref[...] + _dot(a_ref[...], wa_ref[...]) + _dot(c_ref[...], wc_ref[...])


def _out_proj(h, a, c, w_a, w_c, tm):
    t, d = h.shape
    return pl.pallas_call(
        _outproj_kernel,
        grid=(t // tm,),
        in_specs=[
            pl.BlockSpec((tm, d), lambda i: (i, 0)),
            pl.BlockSpec((tm, a.shape[1]), lambda i: (i, 0)),
            pl.BlockSpec((tm, c.shape[1]), lambda i: (i, 0)),
            pl.BlockSpec(w_a.shape, lambda i: (0, 0)),
            pl.BlockSpec(w_c.shape, lambda i: (0, 0)),
        ],
        out_specs=pl.BlockSpec((tm, d), lambda i: (i, 0)),
        out_shape=jax.ShapeDtypeStruct((t, d), F32),
        compiler_params=_params("parallel"),
        name="out_proj",
    )(h, a, c, w_a, w_c)


def _ffn_kernel(h_ref, g_ref, wg_ref, wu_ref, wd_ref, o_ref, xn_sc, acc_sc):
    f = pl.program_id(1)

    @pl.when(f == 0)
    def _():
        xn_sc[...] = _rms(h_ref[...], g_ref[...]).astype(BF16)
        acc_sc[...] = jnp.zeros(acc_sc.shape, F32)

    xn = xn_sc[...]
    a = _dot(xn, wg_ref[...])
    b = _dot(xn, wu_ref[...])
    mid = (a * _sigmoid(a) * b).astype(BF16)
    acc_sc[...] += _dot(mid, wd_ref[...])

    @pl.when(f == pl.num_programs(1) - 1)
    def _():
        o_ref[...] = h_ref[...] + acc_sc[...]


def _ffn(h, g, wg, wu, wd, tm, tf):
    t, d = h.shape
    f_dim = wg.shape[1]
    return pl.pallas_call(
        _ffn_kernel,
        grid=(t // tm, f_dim // tf),
        in_specs=[
            pl.BlockSpec((tm, d), lambda i, f: (i, 0)),
            pl.BlockSpec((1, d), lambda i, f: (0, 0)),
            pl.BlockSpec((d, tf), lambda i, f: (0, f)),
            pl.BlockSpec((d, tf), lambda i, f: (0, f)),
            pl.BlockSpec((tf, d), lambda i, f: (f, 0)),
        ],
        out_specs=pl.BlockSpec((tm, d), lambda i, f: (i, 0)),
        out_shape=jax.ShapeDtypeStruct((t, d), F32),
        scratch_shapes=[pltpu.VMEM((tm, d), BF16), pltpu.VMEM((tm, d), F32)],
        compiler_params=_params("parallel", "arbitrary"),
        name="dense_swiglu",
    )(h, g, wg, wu, wd)


def _pool_kernel(prev_ref, main_ref, next_ref, g_ref, w_ref, sc_ref, o_ref, *, tp, seq):
    i = pl.program_id(1)
    g = g_ref[...]
    h_main = main_ref[...]
    hn_main = _rms(h_main, g)
    ext = jnp.concatenate([_rms(prev_ref[...], g), hn_main, _rms(next_ref[...], g)], axis=0).astype(BF16)
    rows = tp + 2 * POOL_HALO
    t_pos = i * tp + lax.broadcasted_iota(jnp.int32, (tp, rows), 0)
    j_pos = i * tp - POOL_HALO + lax.broadcasted_iota(jnp.int32, (tp, rows), 1)
    in_seq = (j_pos >= 0) & (j_pos < seq)
    t_col = i * tp + lax.broadcasted_iota(jnp.int32, (tp, 1), 0)
    gc = w_ref.shape[1]
    for gi, win in enumerate(POOL_WINDOWS):
        left = win // 2
        right = win - 1 - left
        band = (in_seq & (j_pos >= t_pos - left) & (j_pos <= t_pos + right)).astype(BF16)
        cnt = jnp.minimum(t_col + right, seq - 1) - jnp.maximum(t_col - left, 0) + 1
        cols = slice(gi * gc, (gi + 1) * gc)
        win_sum = _dot(band, ext[:, cols])
        y = (win_sum / cnt.astype(F32) - hn_main[:, cols]).astype(BF16)
        o_ref[:, cols] = h_main[:, cols] + _dot(y, w_ref[gi]) * sc_ref[:, cols]


def _pool_mixer(h, g, pool_w, pool_scale, batch, seq, tp):
    t, d = h.shape
    n_blocks = seq // tp
    hb = tp // POOL_HALO
    n_halo = seq // POOL_HALO
    kern = functools.partial(_pool_kernel, tp=tp, seq=seq)
    return pl.pallas_call(
        kern,
        grid=(batch, n_blocks),
        in_specs=[
            pl.BlockSpec((POOL_HALO, d), lambda b, i: (b * n_halo + jnp.maximum(i * hb - 1, 0), 0)),
            pl.BlockSpec((tp, d), lambda b, i: (b * n_blocks + i, 0)),
            pl.BlockSpec((POOL_HALO, d), lambda b, i: (b * n_halo + jnp.minimum((i + 1) * hb, n_halo - 1), 0)),
            pl.BlockSpec((1, d), lambda b, i: (0, 0)),
            pl.BlockSpec(pool_w.shape, lambda b, i: (0, 0, 0)),
            pl.BlockSpec((1, d), lambda b, i: (0, 0)),
        ],
        out_specs=pl.BlockSpec((tp, d), lambda b, i: (b * n_blocks + i, 0)),
        out_shape=jax.ShapeDtypeStruct((t, d), F32),
        compiler_params=_params("parallel", "parallel"),
        name="pool_mixer",
    )(h, h, h, g, pool_w, pool_scale)


SUBLANES = 8
META_E1, META_E2, META_RANK1, META_RANK2, META_P1, META_P2 = range(6)


def _to_token_tiles(ref, x):
    tm, d = x.shape
    for a in range(d // LANES):
        ref[pl.ds(a, tm, stride=SUBLANES), :] = x[:, a * LANES:(a + 1) * LANES]


def _from_token_tiles(ref, tm, d):
    return [ref[pl.ds(a, tm, stride=SUBLANES), :] for a in range(d // LANES)]


def _router_kernel(h_ref, g_ref, wr_ref, xn_ref, meta_ref, counts_ref):
    @pl.when(pl.program_id(0) == 0)
    def _():
        counts_ref[...] = jnp.zeros(counts_ref.shape, F32)

    xn = _rms(h_ref[...], g_ref[...])
    _to_token_tiles(xn_ref, xn)
    logits = jnp.dot(xn, wr_ref[...], preferred_element_type=F32, precision=lax.Precision.HIGHEST)
    tm = logits.shape[0]
    lane = lax.broadcasted_iota(jnp.int32, logits.shape, 1)
    logits = jnp.where(lane < N_EXPERTS, logits, -jnp.inf)
    v1 = jnp.max(logits, axis=-1, keepdims=True)
    i1 = jnp.min(jnp.where(logits == v1, lane, LANES), axis=-1, keepdims=True)
    rest = jnp.where(lane == i1, -jnp.inf, logits)
    v2 = jnp.max(rest, axis=-1, keepdims=True)
    i2 = jnp.min(jnp.where(rest == v2, lane, LANES), axis=-1, keepdims=True)
    e2 = jnp.exp(v2 - v1)
    p1 = 1.0 / (1.0 + e2)
    p2 = e2 * p1
    chosen = (lane == i1) | (lane == i2)
    earlier = (lax.broadcasted_iota(jnp.int32, (tm, tm), 0) > lax.broadcasted_iota(jnp.int32, (tm, tm), 1))
    before = _dot(earlier.astype(BF16), chosen.astype(BF16)) + counts_ref[...]
    rank1 = jnp.sum(jnp.where(lane == i1, before, 0.0), axis=-1, keepdims=True)
    rank2 = jnp.sum(jnp.where(lane == i2, before, 0.0), axis=-1, keepdims=True)
    counts_ref[...] += jnp.sum(chosen.astype(F32), axis=0, keepdims=True)
    meta = jnp.zeros(logits.shape, F32)
    for col, val in ((META_E1, i1.astype(F32)), (META_E2, i2.astype(F32)), (META_RANK1, rank1),
                     (META_RANK2, rank2), (META_P1, p1), (META_P2, p2)):
        meta = jnp.where(lane == col, val, meta)
    meta_ref[...] = meta


def _router(h, g, wr_pad, tm):
    t, d = h.shape
    return pl.pallas_call(
        _router_kernel,
        grid=(t // tm,),
        in_specs=[
            pl.BlockSpec((tm, d), lambda i: (i, 0)),
            pl.BlockSpec((1, d), lambda i: (0, 0)),
            pl.BlockSpec(wr_pad.shape, lambda i: (0, 0)),
        ],
        out_specs=[
            pl.BlockSpec((tm * SUBLANES, LANES), lambda i: (i, 0)),
            pl.BlockSpec((tm, LANES), lambda i: (i, 0)),
            pl.BlockSpec((1, LANES), lambda i: (0, 0)),
        ],
        out_shape=[
            jax.ShapeDtypeStruct((t * SUBLANES, LANES), F32),
            jax.ShapeDtypeStruct((t, LANES), F32),
            jax.ShapeDtypeStruct((1, LANES), F32),
        ],
        compiler_params=_params("arbitrary"),
        name="router",
    )(h, g, wr_pad)


def _routing_tables(meta, counts, tile_rows, n_tiles):
    cnt = counts[0, :N_EXPERTS].astype(jnp.int32)
    padded = (cnt + tile_rows - 1) // tile_rows * tile_rows
    ends = jnp.cumsum(padded)
    starts = ends - padded
    pos1 = starts[meta[:, META_E1].astype(jnp.int32)] + meta[:, META_RANK1].astype(jnp.int32)
    pos2 = starts[meta[:, META_E2].astype(jnp.int32)] + meta[:, META_RANK2].astype(jnp.int32)
    n_used = ends[-1] // tile_rows
    tile_start = jnp.minimum(jnp.arange(n_tiles, dtype=jnp.int32), n_used - 1) * tile_rows
    tile_expert = jnp.sum((tile_start[:, None] >= ends[None, :]).astype(jnp.int32), axis=-1)
    return pos1, pos2, tile_expert, n_used.reshape(1)


def _token_rows(ref, i, n=1):
    return ref.at[pl.ds(pl.multiple_of(i * SUBLANES, SUBLANES), n * SUBLANES)]


def _dispatch_kernel(pos1_ref, pos2_ref, xn_ref, zeros_hbm, xs_hbm, sem, *, chunk):
    del zeros_hbm
    base = pl.program_id(0) * chunk

    def issue(i, carry):
        src = _token_rows(xn_ref, i)
        pltpu.make_async_copy(src, _token_rows(xs_hbm, pos1_ref[base + i]), sem).start()
        pltpu.make_async_copy(src, _token_rows(xs_hbm, pos2_ref[base + i]), sem).start()
        return carry

    lax.fori_loop(0, chunk, issue, 0, unroll=8)
    for _ in range(TOP_K):
        pltpu.make_async_copy(xn_ref, _token_rows(xs_hbm, 0, chunk), sem).wait()


def _dispatch(pos1, pos2, xn_tiles, n_rows, chunk):
    t = pos1.shape[0]
    zeros = jnp.zeros((n_rows * SUBLANES, LANES), F32)
    return pl.pallas_call(
        functools.partial(_dispatch_kernel, chunk=chunk),
        grid_spec=pltpu.PrefetchScalarGridSpec(
            num_scalar_prefetch=2,
            grid=(t // chunk,),
            in_specs=[pl.BlockSpec((chunk * SUBLANES, LANES), lambda i, p1, p2: (i, 0)),
                      pl.BlockSpec(memory_space=pl.ANY)],
            out_specs=pl.BlockSpec(memory_space=pl.ANY),
            scratch_shapes=[pltpu.SemaphoreType.DMA],
        ),
        out_shape=jax.ShapeDtypeStruct(zeros.shape, F32),
        input_output_aliases={3: 0},
        compiler_params=_params("arbitrary"),
        name="moe_dispatch",
    )(pos1, pos2, xn_tiles, zeros)


def _expert_kernel(tile_expert_ref, n_used_ref, xs_ref, wg_ref, wu_ref, wd_ref, ys_ref, x_sc, acc_sc,
                   *, tm, d):
    del tile_expert_ref
    r = pl.program_id(0)
    f = pl.program_id(1)
    last = pl.num_programs(1) - 1
    used = r < n_used_ref[0]

    @pl.when(used & (f == 0))
    def _():
        x_sc[...] = jnp.concatenate(_from_token_tiles(xs_ref, tm, d), axis=-1).astype(BF16)

    @pl.when(used)
    def _():
        x = x_sc[...]
        a = _dot(x, wg_ref[0])
        b = _dot(x, wu_ref[0])
        y = _dot((a * _sigmoid(a) * b).astype(BF16), wd_ref[0])

        @pl.when(f == 0)
        def _():
            acc_sc[...] = y

        @pl.when((f > 0) & (f < last))
        def _():
            acc_sc[...] += y

        @pl.when(f == last)
        def _():
            _to_token_tiles(ys_ref, acc_sc[...] + y)

    @pl.when(jnp.logical_not(used) & (f == last))
    def _():
        ys_ref[...] = jnp.zeros(ys_ref.shape, F32)


def _experts(tile_expert, n_used, xs, wg, wu, wd, tm, tf):
    n_e, d, f_dim = wg.shape
    n_tiles = xs.shape[0] // (tm * SUBLANES)
    assert f_dim // tf >= 2

    def row_tile(r, f, te, nu):
        return (jnp.minimum(r, nu[0] - 1), 0)

    return pl.pallas_call(
        functools.partial(_expert_kernel, tm=tm, d=d),
        grid_spec=pltpu.PrefetchScalarGridSpec(
            num_scalar_prefetch=2,
            grid=(n_tiles, f_dim // tf),
            in_specs=[
                pl.BlockSpec((tm * SUBLANES, LANES), row_tile),
                pl.BlockSpec((1, d, tf), lambda r, f, te, nu: (te[r], 0, f)),
                pl.BlockSpec((1, d, tf), lambda r, f, te, nu: (te[r], 0, f)),
                pl.BlockSpec((1, tf, d), lambda r, f, te, nu: (te[r], f, 0)),
            ],
            out_specs=pl.BlockSpec((tm * SUBLANES, LANES), lambda r, f, te, nu: (r, 0)),
            scratch_shapes=[pltpu.VMEM((tm, d), BF16), pltpu.VMEM((tm, d), F32)],
        ),
        out_shape=jax.ShapeDtypeStruct(xs.shape, F32),
        compiler_params=_params("arbitrary", "arbitrary"),
        name="moe_experts",
    )(tile_expert, n_used, xs, wg, wu, wd)


def _combine_kernel(pos1_ref, pos2_ref, h_ref, meta_ref, ys_hbm, o_ref, y1_sc, y2_sc, sem, *, tm, d):
    base = pl.program_id(0) * tm

    def issue(i, carry):
        tok = base + i
        pltpu.make_async_copy(_token_rows(ys_hbm, pos1_ref[tok]), _token_rows(y1_sc, i), sem).start()
        pltpu.make_async_copy(_token_rows(ys_hbm, pos2_ref[tok]), _token_rows(y2_sc, i), sem).start()
        return carry

    lax.fori_loop(0, tm, issue, 0, unroll=8)
    pltpu.make_async_copy(_token_rows(ys_hbm, 0, tm), y1_sc, sem).wait()
    pltpu.make_async_copy(_token_rows(ys_hbm, 0, tm), y2_sc, sem).wait()
    meta = meta_ref[...]
    p1 = meta[:, META_P1:META_P1 + 1]
    p2 = meta[:, META_P2:META_P2 + 1]
    y1 = _from_token_tiles(y1_sc, tm, d)
    y2 = _from_token_tiles(y2_sc, tm, d)
    for a in range(d // LANES):
        cols = slice(a * LANES, (a + 1) * LANES)
        o_ref[:, cols] = h_ref[:, cols] + p1 * y1[a] + p2 * y2[a]


def _combine(pos1, pos2, h, meta, ys, tm):
    t, d = h.shape
    return pl.pallas_call(
        functools.partial(_combine_kernel, tm=tm, d=d),
        grid_spec=pltpu.PrefetchScalarGridSpec(
            num_scalar_prefetch=2,
            grid=(t // tm,),
            in_specs=[
                pl.BlockSpec((tm, d), lambda i, p1, p2: (i, 0)),
                pl.BlockSpec((tm, LANES), lambda i, p1, p2: (i, 0)),
                pl.BlockSpec(memory_space=pl.ANY),
            ],
            out_specs=pl.BlockSpec((tm, d), lambda i, p1, p2: (i, 0)),
            scratch_shapes=[pltpu.VMEM((tm * SUBLANES, LANES), F32), pltpu.VMEM((tm * SUBLANES, LANES), F32),
                            pltpu.SemaphoreType.DMA],
        ),
        out_shape=jax.ShapeDtypeStruct((t, d), F32),
        compiler_params=_params("arbitrary"),
        name="moe_combine",
    )(pos1, pos2, h, meta, ys)


def _ple_kernel(h_ref, p_ref, g_ref, wgate_ref, wproj_ref, o_ref):
    h = h_ref[...]
    gate = _sigmoid(_dot(_rms(h, g_ref[...]).astype(BF16), wgate_ref[...]))
    o_ref[...] = h + gate * _dot(p_ref[...].astype(BF16), wproj_ref[...])


def _ple(h, p, g, w_gate, w_proj, tm):
    t, d = h.shape
    return pl.pallas_call(
        _ple_kernel,
        grid=(t // tm,),
        in_specs=[
            pl.BlockSpec((tm, d), lambda i: (i, 0)),
            pl.BlockSpec((tm, p.shape[1]), lambda i: (i, 0)),
            pl.BlockSpec((1, d), lambda i: (0, 0)),
            pl.BlockSpec(w_gate.shape, lambda i: (0, 0)),
            pl.BlockSpec(w_proj.shape, lambda i: (0, 0)),
        ],
        out_specs=pl.BlockSpec((tm, d), lambda i: (i, 0)),
        out_shape=jax.ShapeDtypeStruct((t, d), F32),
        compiler_params=_params("parallel"),
        name="per_layer_input",
    )(h, p, g, w_gate, w_proj)


def _rope_tables(seq):
    rows = seq // GRID_W
    r = jnp.broadcast_to(jnp.arange(rows, dtype=F32)[:, None], (rows, GRID_W)).reshape(seq)
    c = jnp.broadcast_to(jnp.arange(GRID_W, dtype=F32)[None, :], (rows, GRID_W)).reshape(seq)
    inv = ROPE_THETA ** (-jnp.arange(0, AXIS_DIM, 2, dtype=F32) / AXIS_DIM)
    ang = jnp.concatenate([r[:, None] * inv, c[:, None] * inv], axis=-1)
    cos, sin = jnp.cos(ang), jnp.sin(ang)
    reps = LANES // HEAD_DIM
    return (jnp.tile(jnp.concatenate([cos, cos], axis=-1), (1, reps)),
            jnp.tile(jnp.concatenate([-sin, sin], axis=-1), (1, reps)))


def _tile(n, want):
    t = min(n, want)
    assert n % t == 0, (n, t)
    return t


def kernel(x, p, norm_mix, norm_ffn, w_in, q_norm, k_norm, conv_w, conv_b, conv_ln_g, conv_ln_b, w_out,
           ffn_wg, ffn_wu, ffn_wd, pool_w, pool_scale, router_w, moe_wg, moe_wu, moe_wd, ple_norm,
           ple_gate_w, ple_proj):
    batch, seq, d = x.shape
    depth = p.shape[0]
    t = batch * seq
    q_dim = N_HEADS * HEAD_DIM
    assert seq % GRID_W == 0 and d % LANES == 0

    tm = _tile(seq, 512)
    tm_ffn = _tile(t, 1024)
    tq = _tile(seq, 256)
    tk = _tile(seq, 2048)
    tk_online = _tile(seq, 512)
    tc = _tile(seq, 256)
    tp = _tile(seq, 256)
    tf_ffn = ffn_wg.shape[2] // 2
    tf_moe = moe_wg.shape[3] // 2
    tm_moe = _tile(t, 1024)
    n_moe_tiles = TOP_K * t // tm_moe + N_EXPERTS
    dispatch_chunk = _tile(t, 2048)

    cos_t, sin_t = _rope_tables(seq)
    row = lambda v: v.reshape(1, -1)
    tile_heads = lambda v: jnp.tile(v, LANES // HEAD_DIM).reshape(1, LANES)

    h = x.reshape(t, d)
    for i in range(depth):
        j = i // 2
        if i % 2 == 0:
            q, k, vt, u = _in_proj(h, row(norm_mix[i]), w_in[j].astype(BF16), tile_heads(q_norm[j]),
                                   tile_heads(k_norm[j]), cos_t, sin_t, seq, tm)
            logit_bound = (HEAD_DIM ** 0.5 * LOG2E) * jnp.max(jnp.abs(q_norm[j])) * jnp.max(jnp.abs(k_norm[j]))
            a = _attention(q, k, vt, logit_bound, batch, seq, tq, tk, tk_online)
            c = _conv_module(u, conv_w[j], row(conv_b[j]), row(conv_ln_g[j]), row(conv_ln_b[j]),
                             batch, seq, tc)
            wo = w_out[j].astype(BF16)
            h = _out_proj(h, a, c, wo[:q_dim], wo[q_dim:], tm)
            h = _ffn(h, row(norm_ffn[i]), ffn_wg[j].astype(BF16), ffn_wu[j].astype(BF16),
                     ffn_wd[j].astype(BF16), tm_ffn, tf_ffn)
        else:
            h = _pool_mixer(h, row(norm_mix[i]), pool_w[j].astype(BF16), row(pool_scale[j]),
                            batch, seq, tp)
            wr_pad = jnp.pad(router_w[j], ((0, 0), (0, LANES - N_EXPERTS)))
            xn_tiles, meta, counts = _router(h, row(norm_ffn[i]), wr_pad, tm)
            pos1, pos2, tile_expert, n_used = _routing_tables(meta, counts, tm_moe, n_moe_tiles)
            xs = _dispatch(pos1, pos2, xn_tiles, n_moe_tiles * tm_moe, dispatch_chunk)
            ys = _experts(tile_expert, n_used, xs, moe_wg[j].astype(BF16), moe_wu[j].astype(BF16),
                          moe_wd[j].astype(BF16), tm_moe, tf_moe)
            h = _combine(pos1, pos2, h, meta, ys, tm)
        h = _ple(h, p[i].reshape(t, -1), row(ple_norm[i]), ple_gate_w[i].astype(BF16),
                 ple_proj[i].astype(BF16), tm)
    return h.reshape(batch, seq, d)
```

```python
import functools

import jax
import jax.numpy as jnp
from jax import lax
from jax.experimental import pallas as pl
from jax.experimental.pallas import tpu as pltpu

F32 = jnp.float32
BF16 = jnp.bfloat16

GRID_W = 64
N_HEADS = 8
KV_HEADS = 2
HEAD_DIM = 64
Q_PER_KV = N_HEADS // KV_HEADS
AXIS_DIM = HEAD_DIM // 2
ROPE_THETA = 10000.0
CONV_WIDTH = 31
POOL_WINDOWS = (2, 4, 8, 16)
N_EXPERTS = 8
TOP_K = 2
EPS = 1e-6

LANES = 128
VMEM_LIMIT = 56 * 1024 * 1024
BF16_SUBLANES = 16
VT_ROWS = HEAD_DIM + BF16_SUBLANES
LOG2E = 1.4426950408889634
Q_SCALE = HEAD_DIM ** -0.5 * LOG2E
MAX_UNSHIFTED_LOGIT = 80.0
CONV_HALO = 16
POOL_HALO = 8


def _params(*sem):
    return pltpu.CompilerParams(dimension_semantics=sem, vmem_limit_bytes=VMEM_LIMIT)


def _rms(x, g):
    return x * lax.rsqrt(jnp.mean(x * x, axis=-1, keepdims=True) + EPS) * g


def _sigmoid(x):
    return 1.0 / (1.0 + jnp.exp(-x))


def _dot(a, b):
    return jnp.dot(a, b, preferred_element_type=F32)


def _ple_update(h, p_ref, g_ref, wgate_ref, wproj_ref):
    gate = _sigmoid(_dot(_rms(h, g_ref[...]).astype(BF16), wgate_ref[...]))
    return h + gate * _dot(p_ref[...].astype(BF16), wproj_ref[...])


def _inproj_kernel(h_ref, g_ref, w_ref, qg_ref, kg_ref, cos_ref, sin_ref,
                   q_ref, k_ref, vt_ref, u_ref, *, q_dim, kv_dim, conv_ch):
    xn = _rms(h_ref[...], g_ref[...]).astype(BF16)
    proj = _dot(xn, w_ref[...])
    tm = proj.shape[0]
    cos = cos_ref[...]
    sin = sin_ref[...]
    lane = lax.broadcasted_iota(jnp.int32, (tm, LANES), 1)
    head0 = lane < HEAD_DIM
    first_half = (lane % HEAD_DIM) < (HEAD_DIM // 2)

    def norm_rope(x, g, scale):
        sq = x * x
        s0 = jnp.sum(jnp.where(head0, sq, 0.0), axis=-1, keepdims=True)
        s1 = jnp.sum(jnp.where(head0, 0.0, sq), axis=-1, keepdims=True)
        ms = jnp.where(head0, s0, s1) * (1.0 / HEAD_DIM)
        y = x * lax.rsqrt(ms + EPS) * g
        partner = jnp.where(first_half,
                            pltpu.roll(y, LANES - HEAD_DIM // 2, 1),
                            pltpu.roll(y, HEAD_DIM // 2, 1))
        return (y * cos + partner * sin) * scale

    for c in range(q_dim // LANES):
        x = proj[:, c * LANES:(c + 1) * LANES]
        q_ref[:, c * LANES:(c + 1) * LANES] = norm_rope(x, qg_ref[...], Q_SCALE).astype(BF16)
    sub = lax.broadcasted_iota(jnp.int32, (VT_ROWS - HEAD_DIM, tm), 0)
    ones_rows = jnp.where(sub == 0, 1.0, 0.0).astype(BF16)
    for c in range(kv_dim // LANES):
        x = proj[:, q_dim + c * LANES:q_dim + (c + 1) * LANES]
        kk = norm_rope(x, kg_ref[...], 1.0).astype(BF16)
        vv_t = proj[:, q_dim + kv_dim + c * LANES:q_dim + kv_dim + (c + 1) * LANES].T
        for j in range(LANES // HEAD_DIM):
            head = c * (LANES // HEAD_DIM) + j
            k_ref[head] = kk[:, j * HEAD_DIM:(j + 1) * HEAD_DIM]
            vt_ref[head, 0:HEAD_DIM, :] = vv_t[j * HEAD_DIM:(j + 1) * HEAD_DIM, :].astype(BF16)
            vt_ref[head, HEAD_DIM:VT_ROWS, :] = ones_rows
    u0 = q_dim + 2 * kv_dim
    u_ref[...] = (proj[:, u0:u0 + conv_ch] * _sigmoid(proj[:, u0 + conv_ch:u0 + 2 * conv_ch])).astype(BF16)


def _in_proj(h, g, w_in, layer, qg, kg, cos_t, sin_t, seq, tm):
    t, d = h.shape
    w_in_dim = w_in.shape[-1]
    q_dim = N_HEADS * HEAD_DIM
    kv_dim = KV_HEADS * HEAD_DIM
    conv_ch = (w_in_dim - q_dim - 2 * kv_dim) // 2
    n_seq_blocks = seq // tm
    kern = functools.partial(_inproj_kernel, q_dim=q_dim, kv_dim=kv_dim, conv_ch=conv_ch)
    return pl.pallas_call(
        kern,
        grid=(t // tm,),
        in_specs=[
            pl.BlockSpec((tm, d), lambda i: (i, 0)),
            pl.BlockSpec((1, d), lambda i: (0, 0)),
            pl.BlockSpec((None, d, w_in_dim), lambda i: (layer, 0, 0)),
            pl.BlockSpec((1, LANES), lambda i: (0, 0)),
            pl.BlockSpec((1, LANES), lambda i: (0, 0)),
            pl.BlockSpec((tm, LANES), lambda i: (i % n_seq_blocks, 0)),
            pl.BlockSpec((tm, LANES), lambda i: (i % n_seq_blocks, 0)),
        ],
        out_specs=[
            pl.BlockSpec((tm, q_dim), lambda i: (i, 0)),
            pl.BlockSpec((KV_HEADS, tm, HEAD_DIM), lambda i: (0, i, 0)),
            pl.BlockSpec((KV_HEADS, VT_ROWS, tm), lambda i: (0, 0, i)),
            pl.BlockSpec((tm, conv_ch), lambda i: (i, 0)),
        ],
        out_shape=[
            jax.ShapeDtypeStruct((t, q_dim), BF16),
            jax.ShapeDtypeStruct((KV_HEADS, t, HEAD_DIM), BF16),
            jax.ShapeDtypeStruct((KV_HEADS, VT_ROWS, t), BF16),
            jax.ShapeDtypeStruct((t, conv_ch), BF16),
        ],
        compiler_params=_params("parallel"),
        name="in_proj",
    )(h, g, w_in, qg, kg, cos_t, sin_t)


def _stack_query_heads(q_ref, q_sc, tq):
    for g in range(Q_PER_KV):
        q_sc[g * tq:(g + 1) * tq, :] = q_ref[:, g * HEAD_DIM:(g + 1) * HEAD_DIM]


def _unstack_query_heads(out, tq):
    return jnp.concatenate([out[g * tq:(g + 1) * tq, :] for g in range(Q_PER_KV)], axis=-1).astype(BF16)


_NT = (((1,), (1,)), ((), ()))


def _attn_unshifted_kernel(q_ref, k_ref, vt_ref, o_ref, q_sc, acc_sc, *, tq, tk, n_kv):
    _stack_query_heads(q_ref, q_sc, tq)
    acc_sc[...] = jnp.zeros(acc_sc.shape, F32)

    def body(j, carry):
        kv0 = pl.multiple_of(j * tk, tk)
        s_t = lax.dot_general(k_ref[0, pl.ds(kv0, tk), :], q_sc[...], _NT,
                              preferred_element_type=F32)
        p_t = jnp.exp2(s_t).astype(BF16)
        acc_sc[...] += _dot(vt_ref[0, :, pl.ds(kv0, tk)], p_t)
        return carry

    lax.fori_loop(0, n_kv, body, 0)
    acc = acc_sc[...]
    out_t = acc[:HEAD_DIM, :] / acc[HEAD_DIM:HEAD_DIM + 1, :]
    o_ref[...] = _unstack_query_heads(out_t.T, tq)


def _attn_online_kernel(q_ref, k_ref, vt_ref, o_ref, q_sc, m_sc, l_sc, acc_sc, *, tq, tk, n_kv):
    _stack_query_heads(q_ref, q_sc, tq)
    m_sc[...] = jnp.full(m_sc.shape, -jnp.inf, F32)
    l_sc[...] = jnp.zeros(l_sc.shape, F32)
    acc_sc[...] = jnp.zeros(acc_sc.shape, F32)

    def body(j, carry):
        kv0 = pl.multiple_of(j * tk, tk)
        s = lax.dot_general(q_sc[...], k_ref[0, pl.ds(kv0, tk), :], _NT, preferred_element_type=F32)
        m_prev = m_sc[...]
        m_new = jnp.maximum(m_prev, jnp.max(s, axis=-1, keepdims=True))
        alpha = jnp.exp2(m_prev - m_new)
        p = jnp.exp2(s - m_new)
        l_sc[...] = alpha * l_sc[...] + jnp.sum(p, axis=-1, keepdims=True)
        v_t = vt_ref[0, 0:HEAD_DIM, pl.ds(kv0, tk)]
        acc_sc[...] = alpha * acc_sc[...] + lax.dot_general(p.astype(BF16), v_t, _NT,
                                                            preferred_element_type=F32)
        m_sc[...] = m_new
        return carry

    lax.fori_loop(0, n_kv, body, 0)
    o_ref[...] = _unstack_query_heads(acc_sc[...] / l_sc[...], tq)


def _attention_call(kern, scratch, name, q, k, vt, batch, seq, tq, tk):
    t = q.shape[0]
    n_q = seq // tq
    gw = Q_PER_KV * HEAD_DIM
    return pl.pallas_call(
        functools.partial(kern, tq=tq, tk=tk, n_kv=seq // tk),
        grid=(batch, KV_HEADS, n_q),
        in_specs=[
            pl.BlockSpec((tq, gw), lambda b, h, i: (b * n_q + i, h)),
            pl.BlockSpec((1, seq, HEAD_DIM), lambda b, h, i: (h, b, 0)),
            pl.BlockSpec((1, VT_ROWS, seq), lambda b, h, i: (h, 0, b)),
        ],
        out_specs=pl.BlockSpec((tq, gw), lambda b, h, i: (b * n_q + i, h)),
        out_shape=jax.ShapeDtypeStruct((t, N_HEADS * HEAD_DIM), BF16),
        scratch_shapes=[pltpu.VMEM((Q_PER_KV * tq, HEAD_DIM), BF16)] + scratch,
        compiler_params=_params("parallel", "parallel", "parallel"),
        name=name,
    )(q, k, vt)


def _attention(q, k, vt, logit_bound, batch, seq, tq, tk, tk_online):
    m = Q_PER_KV * tq

    def unshifted(q, k, vt):
        return _attention_call(_attn_unshifted_kernel, [pltpu.VMEM((VT_ROWS, m), F32)],
                               "attention", q, k, vt, batch, seq, tq, tk)

    def online(q, k, vt):
        scratch = [pltpu.VMEM((m, 1), F32), pltpu.VMEM((m, 1), F32), pltpu.VMEM((m, HEAD_DIM), F32)]
        return _attention_call(_attn_online_kernel, scratch, "attention_online",
                               q, k, vt, batch, seq, tq, tk_online)

    return lax.cond(logit_bound <= MAX_UNSHIFTED_LOGIT, unshifted, online, q, k, vt)


def _conv_kernel(prev_ref, main_ref, next_ref, w_ref, b_ref, g_ref, beta_ref, o_ref, ext_sc, y_sc,
                 *, tc, n_blocks):
    i = pl.program_id(1)
    ch = main_ref.shape[1]
    prev = prev_ref[...].astype(F32)
    nxt = next_ref[...].astype(F32)
    ext_sc[0:CONV_HALO, :] = jnp.where(i == 0, 0.0, prev)
    ext_sc[CONV_HALO:CONV_HALO + tc, :] = main_ref[...].astype(F32)
    ext_sc[CONV_HALO + tc:, :] = jnp.where(i == n_blocks - 1, 0.0, nxt)
    base = CONV_HALO - CONV_WIDTH // 2
    for c in range(ch // LANES):
        cols = slice(c * LANES, (c + 1) * LANES)
        acc = jnp.zeros((tc, LANES), F32) + b_ref[:, cols]
        for kk in range(CONV_WIDTH):
            acc = acc + ext_sc[base + kk:base + kk + tc, cols] * w_ref[kk:kk + 1, cols]
        y_sc[:, cols] = acc
    y = y_sc[...]
    mu = jnp.mean(y, axis=-1, keepdims=True)
    yc = y - mu
    var = jnp.mean(yc * yc, axis=-1, keepdims=True)
    z = yc * lax.rsqrt(var + EPS) * g_ref[...] + beta_ref[...]
    o_ref[...] = (z * _sigmoid(z)).astype(BF16)


def _conv_module(u, conv_w, conv_b, ln_g, ln_b, batch, seq, tc):
    t, ch = u.shape
    n_blocks = seq // tc
    hb = tc // CONV_HALO
    n_halo = seq // CONV_HALO
    kern = functools.partial(_conv_kernel, tc=tc, n_blocks=n_blocks)
    return pl.pallas_call(
        kern,
        grid=(batch, n_blocks),
        in_specs=[
            pl.BlockSpec((CONV_HALO, ch), lambda b, i: (b * n_halo + jnp.maximum(i * hb - 1, 0), 0)),
            pl.BlockSpec((tc, ch), lambda b, i: (b * n_blocks + i, 0)),
            pl.BlockSpec((CONV_HALO, ch), lambda b, i: (b * n_halo + jnp.minimum((i + 1) * hb, n_halo - 1), 0)),
            pl.BlockSpec((CONV_WIDTH, ch), lambda b, i: (0, 0)),
            pl.BlockSpec((1, ch), lambda b, i: (0, 0)),
            pl.BlockSpec((1, ch), lambda b, i: (0, 0)),
            pl.BlockSpec((1, ch), lambda b, i: (0, 0)),
        ],
        out_specs=pl.BlockSpec((tc, ch), lambda b, i: (b * n_blocks + i, 0)),
        out_shape=jax.ShapeDtypeStruct((t, ch), BF16),
        scratch_shapes=[
            pltpu.VMEM((tc + 2 * CONV_HALO, ch), F32),
            pltpu.VMEM((tc, ch), F32),
        ],
        compiler_params=_params("parallel", "parallel"),
        name="conv_module",
    )(u, u, u, conv_w, conv_b, ln_g, ln_b)


def _outproj_kernel(h_ref, a_ref, c_ref, wa_ref, wc_ref, o_ref):
    o_ref[...] = h_ref[...] + _dot(a_ref[...], wa_ref[...]) + _dot(c_ref[...], wc_ref[...])


def _out_proj(h, a, c, w_out, layer, tm):
    t, d = h.shape
    assert a.shape[1] == c.shape[1] and a.shape[1] + c.shape[1] == w_out.shape[1]
    return pl.pallas_call(
        _outproj_kernel,
        grid=(t // tm,),
        in_specs=[
            pl.BlockSpec((tm, d), lambda i: (i, 0)),
            pl.BlockSpec((tm, a.shape[1]), lambda i: (i, 0)),
            pl.BlockSpec((tm, c.shape[1]), lambda i: (i, 0)),
            pl.BlockSpec((None, a.shape[1], d), lambda i: (layer, 0, 0)),
            pl.BlockSpec((None, c.shape[1], d), lambda i: (layer, 1, 0)),
        ],
        out_specs=pl.BlockSpec((tm, d), lambda i: (i, 0)),
        out_shape=jax.ShapeDtypeStruct((t, d), F32),
        compiler_params=_params("parallel"),
        name="out_proj",
    )(h, a, c, w_out, w_out)


def _ffn_kernel(h_ref, g_ref, wg_ref, wu_ref, wd_ref, o_ref, xn_sc, acc_sc):
    f = pl.program_id(1)

    @pl.when(f == 0)
    def _():
        xn_sc[...] = _rms(h_ref[...], g_ref[...]).astype(BF16)
        acc_sc[...] = jnp.zeros(acc_sc.shape, F32)

    xn = xn_sc[...]
    a = _dot(xn, wg_ref[...])
    b = _dot(xn, wu_ref[...])
    mid = (a * _sigmoid(a) * b).astype(BF16)
    acc_sc[...] += _dot(mid, wd_ref[...])

    @pl.when(f == pl.num_programs(1) - 1)
    def _():
        o_ref[...] = h_ref[...] + acc_sc[...]


def _ffn(h, g, wg, wu, wd, layer, tm, tf):
    t, d = h.shape
    f_dim = wg.shape[-1]
    return pl.pallas_call(
        _ffn_kernel,
        grid=(t // tm, f_dim // tf),
        in_specs=[
            pl.BlockSpec((tm, d), lambda i, f: (i, 0)),
            pl.BlockSpec((1, d), lambda i, f: (0, 0)),
            pl.BlockSpec((None, d, tf), lambda i, f: (layer, 0, f)),
            pl.BlockSpec((None, d, tf), lambda i, f: (layer, 0, f)),
            pl.BlockSpec((None, tf, d), lambda i, f: (layer, f, 0)),
        ],
        out_specs=pl.BlockSpec((tm, d), lambda i, f: (i, 0)),
        out_shape=jax.ShapeDtypeStruct((t, d), F32),
        scratch_shapes=[pltpu.VMEM((tm, d), BF16), pltpu.VMEM((tm, d), F32)],
        compiler_params=_params("parallel", "arbitrary"),
        name="dense_swiglu",
    )(h, g, wg, wu, wd)


def _pool_kernel(prev_ref, main_ref, next_ref, g_ref, w_ref, sc_ref, o_ref, *, tp, seq):
    i = pl.program_id(1)
    g = g_ref[...]
    h_main = main_ref[...]
    hn_main = _rms(h_main, g)
    ext = jnp.concatenate([_rms(prev_ref[...], g), hn_main, _rms(next_ref[...], g)], axis=0).astype(BF16)
    rows = tp + 2 * POOL_HALO
    t_pos = i * tp + lax.broadcasted_iota(jnp.int32, (tp, rows), 0)
    j_pos = i * tp - POOL_HALO + lax.broadcasted_iota(jnp.int32, (tp, rows), 1)
    in_seq = (j_pos >= 0) & (j_pos < seq)
    t_col = i * tp + lax.broadcasted_iota(jnp.int32, (tp, 1), 0)
    gc = w_ref.shape[1]
    for gi, win in enumerate(POOL_WINDOWS):
        left = win // 2
        right = win - 1 - left
        band = (in_seq & (j_pos >= t_pos - left) & (j_pos <= t_pos + right)).astype(BF16)
        cnt = jnp.minimum(t_col + right, seq - 1) - jnp.maximum(t_col - left, 0) + 1
        cols = slice(gi * gc, (gi + 1) * gc)
        win_sum = _dot(band, ext[:, cols])
        y = (win_sum / cnt.astype(F32) - hn_main[:, cols]).astype(BF16)
        o_ref[:, cols] = h_main[:, cols] + _dot(y, w_ref[gi]) * sc_ref[:, cols]


def _pool_mixer(h, g, pool_w, layer, pool_scale, batch, seq, tp):
    t, d = h.shape
    n_blocks = seq // tp
    hb = tp // POOL_HALO
    n_halo = seq // POOL_HALO
    kern = functools.partial(_pool_kernel, tp=tp, seq=seq)
    return pl.pallas_call(
        kern,
        grid=(batch, n_blocks),
        in_specs=[
            pl.BlockSpec((POOL_HALO, d), lambda b, i: (b * n_halo + jnp.maximum(i * hb - 1, 0), 0)),
            pl.BlockSpec((tp, d), lambda b, i: (b * n_blocks + i, 0)),
            pl.BlockSpec((POOL_HALO, d), lambda b, i: (b * n_halo + jnp.minimum((i + 1) * hb, n_halo - 1), 0)),
            pl.BlockSpec((1, d), lambda b, i: (0, 0)),
            pl.BlockSpec((None,) + pool_w.shape[1:], lambda b, i: (layer, 0, 0, 0)),
            pl.BlockSpec((1, d), lambda b, i: (0, 0)),
        ],
        out_specs=pl.BlockSpec((tp, d), lambda b, i: (b * n_blocks + i, 0)),
        out_shape=jax.ShapeDtypeStruct((t, d), F32),
        compiler_params=_params("parallel", "parallel"),
        name="pool_mixer",
    )(h, h, h, g, pool_w, pool_scale)


SUBLANES = 8
META_E1, META_E2, META_RANK1, META_RANK2, META_P1, META_P2 = range(6)


def _to_token_tiles(ref, x):
    tm, d = x.shape
    for a in range(d // LANES):
        ref[pl.ds(a, tm, stride=SUBLANES), :] = x[:, a * LANES:(a + 1) * LANES]


def _from_token_tiles(ref, tm, d):
    return [ref[pl.ds(a, tm, stride=SUBLANES), :] for a in range(d // LANES)]


def _router_kernel(h_ref, g_ref, wr_ref, xn_ref, meta_ref, counts_ref):
    @pl.when(pl.program_id(0) == 0)
    def _():
        counts_ref[...] = jnp.zeros(counts_ref.shape, F32)

    xn = _rms(h_ref[...], g_ref[...])
    _to_token_tiles(xn_ref, xn)
    logits = jnp.dot(xn, wr_ref[...], preferred_element_type=F32, precision=lax.Precision.HIGHEST)
    tm = logits.shape[0]
    lane = lax.broadcasted_iota(jnp.int32, logits.shape, 1)
    logits = jnp.where(lane < N_EXPERTS, logits, -jnp.inf)
    v1 = jnp.max(logits, axis=-1, keepdims=True)
    i1 = jnp.min(jnp.where(logits == v1, lane, LANES), axis=-1, keepdims=True)
    rest = jnp.where(lane == i1, -jnp.inf, logits)
    v2 = jnp.max(rest, axis=-1, keepdims=True)
    i2 = jnp.min(jnp.where(rest == v2, lane, LANES), axis=-1, keepdims=True)
    e2 = jnp.exp(v2 - v1)
    p1 = 1.0 / (1.0 + e2)
    p2 = e2 * p1
    chosen = (lane == i1) | (lane == i2)
    earlier = (lax.broadcasted_iota(jnp.int32, (tm, tm), 0) > lax.broadcasted_iota(jnp.int32, (tm, tm), 1))
    before = _dot(earlier.astype(BF16), chosen.astype(BF16)) + counts_ref[...]
    rank1 = jnp.sum(jnp.where(lane == i1, before, 0.0), axis=-1, keepdims=True)
    rank2 = jnp.sum(jnp.where(lane == i2, before, 0.0), axis=-1, keepdims=True)
    counts_ref[...] += jnp.sum(chosen.astype(F32), axis=0, keepdims=True)
    meta = jnp.zeros(logits.shape, F32)
    for col, val in ((META_E1, i1.astype(F32)), (META_E2, i2.astype(F32)), (META_RANK1, rank1),
                     (META_RANK2, rank2), (META_P1, p1), (META_P2, p2)):
        meta = jnp.where(lane == col, val, meta)
    meta_ref[...] = meta


def _router(h, g, wr_pad, tm):
    t, d = h.shape
    return pl.pallas_call(
        _router_kernel,
        grid=(t // tm,),
        in_specs=[
            pl.BlockSpec((tm, d), lambda i: (i, 0)),
            pl.BlockSpec((1, d), lambda i: (0, 0)),
            pl.BlockSpec(wr_pad.shape, lambda i: (0, 0)),
        ],
        out_specs=[
            pl.BlockSpec((tm * SUBLANES, LANES), lambda i: (i, 0)),
            pl.BlockSpec((tm, LANES), lambda i: (i, 0)),
            pl.BlockSpec((1, LANES), lambda i: (0, 0)),
        ],
        out_shape=[
            jax.ShapeDtypeStruct((t * SUBLANES, LANES), F32),
            jax.ShapeDtypeStruct((t, LANES), F32),
            jax.ShapeDtypeStruct((1, LANES), F32),
        ],
        compiler_params=_params("arbitrary"),
        name="router",
    )(h, g, wr_pad)


def _routing_tables(meta, counts, tile_rows, n_tiles):
    cnt = counts[0, :N_EXPERTS].astype(jnp.int32)
    padded = (cnt + tile_rows - 1) // tile_rows * tile_rows
    ends = jnp.cumsum(padded)
    starts = ends - padded
    pos1 = starts[meta[:, META_E1].astype(jnp.int32)] + meta[:, META_RANK1].astype(jnp.int32)
    pos2 = starts[meta[:, META_E2].astype(jnp.int32)] + meta[:, META_RANK2].astype(jnp.int32)
    n_used = ends[-1] // tile_rows
    tile_start = jnp.minimum(jnp.arange(n_tiles, dtype=jnp.int32), n_used - 1) * tile_rows
    tile_expert = jnp.sum((tile_start[:, None] >= ends[None, :]).astype(jnp.int32), axis=-1)
    return pos1, pos2, tile_expert, n_used.reshape(1)


def _token_rows(ref, i, n=1):
    return ref.at[pl.ds(pl.multiple_of(i * SUBLANES, SUBLANES), n * SUBLANES)]


def _dispatch_kernel(pos1_ref, pos2_ref, xn_ref, zeros_hbm, xs_hbm, sem, *, chunk):
    del zeros_hbm
    base = pl.program_id(0) * chunk

    def issue(i, carry):
        src = _token_rows(xn_ref, i)
        pltpu.make_async_copy(src, _token_rows(xs_hbm, pos1_ref[base + i]), sem).start()
        pltpu.make_async_copy(src, _token_rows(xs_hbm, pos2_ref[base + i]), sem).start()
        return carry

    lax.fori_loop(0, chunk, issue, 0, unroll=8)
    for _ in range(TOP_K):
        pltpu.make_async_copy(xn_ref, _token_rows(xs_hbm, 0, chunk), sem).wait()


def _dispatch(pos1, pos2, xn_tiles, n_rows, chunk):
    t = pos1.shape[0]
    zeros = jnp.zeros((n_rows * SUBLANES, LANES), F32)
    return pl.pallas_call(
        functools.partial(_dispatch_kernel, chunk=chunk),
        grid_spec=pltpu.PrefetchScalarGridSpec(
            num_scalar_prefetch=2,
            grid=(t // chunk,),
            in_specs=[pl.BlockSpec((chunk * SUBLANES, LANES), lambda i, p1, p2: (i, 0)),
                      pl.BlockSpec(memory_space=pl.ANY)],
            out_specs=pl.BlockSpec(memory_space=pl.ANY),
            scratch_shapes=[pltpu.SemaphoreType.DMA],
        ),
        out_shape=jax.ShapeDtypeStruct(zeros.shape, F32),
        input_output_aliases={3: 0},
        compiler_params=_params("arbitrary"),
        name="moe_dispatch",
    )(pos1, pos2, xn_tiles, zeros)


def _expert_kernel(tile_expert_ref, n_used_ref, xs_ref, wg_ref, wu_ref, wd_ref, ys_ref, x_sc, acc_sc,
                   *, tm, d):
    del tile_expert_ref
    r = pl.program_id(0)
    f = pl.program_id(1)
    last = pl.num_programs(1) - 1
    used = r < n_used_ref[0]

    @pl.when(used & (f == 0))
    def _():
        x_sc[...] = jnp.concatenate(_from_token_tiles(xs_ref, tm, d), axis=-1).astype(BF16)

    @pl.when(used)
    def _():
        x = x_sc[...]
        a = _dot(x, wg_ref[...])
        b = _dot(x, wu_ref[...])
        y = _dot((a * _sigmoid(a) * b).astype(BF16), wd_ref[...])

        @pl.when(f == 0)
        def _():
            acc_sc[...] = y

        @pl.when((f > 0) & (f < last))
        def _():
            acc_sc[...] += y

        @pl.when(f == last)
        def _():
            _to_token_tiles(ys_ref, acc_sc[...] + y)

    @pl.when(jnp.logical_not(used) & (f == last))
    def _():
        ys_ref[...] = jnp.zeros(ys_ref.shape, F32)


def _experts(tile_expert, n_used, xs, wg, wu, wd, layer, tm, tf):
    _, n_e, d, f_dim = wg.shape
    n_tiles = xs.shape[0] // (tm * SUBLANES)
    assert f_dim // tf >= 2

    def row_tile(r, f, te, nu):
        return (jnp.minimum(r, nu[0] - 1), 0)

    return pl.pallas_call(
        functools.partial(_expert_kernel, tm=tm, d=d),
        grid_spec=pltpu.PrefetchScalarGridSpec(
            num_scalar_prefetch=2,
            grid=(n_tiles, f_dim // tf),
            in_specs=[
                pl.BlockSpec((tm * SUBLANES, LANES), row_tile),
                pl.BlockSpec((None, None, d, tf), lambda r, f, te, nu: (layer, te[r], 0, f)),
                pl.BlockSpec((None, None, d, tf), lambda r, f, te, nu: (layer, te[r], 0, f)),
                pl.BlockSpec((None, None, tf, d), lambda r, f, te, nu: (layer, te[r], f, 0)),
            ],
            out_specs=pl.BlockSpec((tm * SUBLANES, LANES), lambda r, f, te, nu: (r, 0)),
            scratch_shapes=[pltpu.VMEM((tm, d), BF16), pltpu.VMEM((tm, d), F32)],
        ),
        out_shape=jax.ShapeDtypeStruct(xs.shape, F32),
        compiler_params=_params("arbitrary", "arbitrary"),
        name="moe_experts",
    )(tile_expert, n_used, xs, wg, wu, wd)


def _combine_kernel(pos1_ref, pos2_ref, h_ref, meta_ref, ys_hbm, p_ref, g_ref, wgate_ref, wproj_ref,
                    o_ref, y1_sc, y2_sc, sem, *, tm, d):
    base = pl.program_id(0) * tm

    def issue(i, carry):
        tok = base + i
        pltpu.make_async_copy(_token_rows(ys_hbm, pos1_ref[tok]), _token_rows(y1_sc, i), sem).start()
        pltpu.make_async_copy(_token_rows(ys_hbm, pos2_ref[tok]), _token_rows(y2_sc, i), sem).start()
        return carry

    lax.fori_loop(0, tm, issue, 0, unroll=8)
    pltpu.make_async_copy(_token_rows(ys_hbm, 0, tm), y1_sc, sem).wait()
    pltpu.make_async_copy(_token_rows(ys_hbm, 0, tm), y2_sc, sem).wait()
    meta = meta_ref[...]
    p1 = meta[:, META_P1:META_P1 + 1]
    p2 = meta[:, META_P2:META_P2 + 1]
    y1 = _from_token_tiles(y1_sc, tm, d)
    y2 = _from_token_tiles(y2_sc, tm, d)
    for a in range(d // LANES):
        cols = slice(a * LANES, (a + 1) * LANES)
        o_ref[:, cols] = h_ref[:, cols] + p1 * y1[a] + p2 * y2[a]
    o_ref[...] = _ple_update(o_ref[...], p_ref, g_ref, wgate_ref, wproj_ref)


def _combine(pos1, pos2, h, meta, ys, p, g, w_gate, w_proj, layer, tm):
    t, d = h.shape
    return pl.pallas_call(
        functools.partial(_combine_kernel, tm=tm, d=d),
        grid_spec=pltpu.PrefetchScalarGridSpec(
            num_scalar_prefetch=2,
            grid=(t // tm,),
            in_specs=[
                pl.BlockSpec((tm, d), lambda i, p1, p2: (i, 0)),
                pl.BlockSpec((tm, LANES), lambda i, p1, p2: (i, 0)),
                pl.BlockSpec(memory_space=pl.ANY),
            ] + _ple_specs(p, w_gate, w_proj, layer, tm, d),
            out_specs=pl.BlockSpec((tm, d), lambda i, p1, p2: (i, 0)),
            scratch_shapes=[pltpu.VMEM((tm * SUBLANES, LANES), F32), pltpu.VMEM((tm * SUBLANES, LANES), F32),
                            pltpu.SemaphoreType.DMA],
        ),
        out_shape=jax.ShapeDtypeStruct((t, d), F32),
        compiler_params=_params("arbitrary"),
        name="moe_combine",
    )(pos1, pos2, h, meta, ys, p, g, w_gate, w_proj)


def _ple_kernel(h_ref, p_ref, g_ref, wgate_ref, wproj_ref, o_ref):
    o_ref[...] = _ple_update(h_ref[...], p_ref, g_ref, wgate_ref, wproj_ref)


def _ple_specs(p, w_gate, w_proj, layer, tm, d):
    def spec(block, index):
        return pl.BlockSpec(block, lambda i, *prefetch: index(i))
    return [
        spec((None, tm, p.shape[-1]), lambda i: (layer, i, 0)),
        spec((1, d), lambda i: (0, 0)),
        spec((None,) + w_gate.shape[1:], lambda i: (layer, 0, 0)),
        spec((None,) + w_proj.shape[1:], lambda i: (layer, 0, 0)),
    ]


def _ple(h, p, g, w_gate, w_proj, layer, tm):
    t, d = h.shape
    return pl.pallas_call(
        _ple_kernel,
        grid=(t // tm,),
        in_specs=[pl.BlockSpec((tm, d), lambda i: (i, 0))] + _ple_specs(p, w_gate, w_proj, layer, tm, d),
        out_specs=pl.BlockSpec((tm, d), lambda i: (i, 0)),
        out_shape=jax.ShapeDtypeStruct((t, d), F32),
        compiler_params=_params("parallel"),
        name="per_layer_input",
    )(h, p, g, w_gate, w_proj)


def _rope_tables(seq):
    rows = seq // GRID_W
    r = jnp.broadcast_to(jnp.arange(rows, dtype=F32)[:, None], (rows, GRID_W)).reshape(seq)
    c = jnp.broadcast_to(jnp.arange(GRID_W, dtype=F32)[None, :], (rows, GRID_W)).reshape(seq)
    inv = ROPE_THETA ** (-jnp.arange(0, AXIS_DIM, 2, dtype=F32) / AXIS_DIM)
    ang = jnp.concatenate([r[:, None] * inv, c[:, None] * inv], axis=-1)
    cos, sin = jnp.cos(ang), jnp.sin(ang)
    reps = LANES // HEAD_DIM
    return (jnp.tile(jnp.concatenate([cos, cos], axis=-1), (1, reps)),
            jnp.tile(jnp.concatenate([-sin, sin], axis=-1), (1, reps)))


def _tile(n, want):
    t = min(n, want)
    assert n % t == 0, (n, t)
    return t


def kernel(x, p, norm_mix, norm_ffn, w_in, q_norm, k_norm, conv_w, conv_b, conv_ln_g, conv_ln_b, w_out,
           ffn_wg, ffn_wu, ffn_wd, pool_w, pool_scale, router_w, moe_wg, moe_wu, moe_wd, ple_norm,
           ple_gate_w, ple_proj):
    batch, seq, d = x.shape
    depth = p.shape[0]
    t = batch * seq
    q_dim = N_HEADS * HEAD_DIM
    assert seq % GRID_W == 0 and d % LANES == 0

    tm = _tile(seq, 512)
    tm_ffn = _tile(t, 1024)
    tq = _tile(seq, 256)
    tk = _tile(seq, 8192)
    tk_online = _tile(seq, 512)
    tc = _tile(seq, 256)
    tp = _tile(seq, 256)
    tf_ffn = ffn_wg.shape[2] // 2
    tf_moe = moe_wg.shape[3] // 2
    tm_moe = _tile(t, 1024)
    n_moe_tiles = TOP_K * t // tm_moe + N_EXPERTS
    dispatch_chunk = _tile(t, 2048)

    cos_t, sin_t = _rope_tables(seq)
    row = lambda v: v.reshape(1, -1)
    tile_heads = lambda v: jnp.tile(v, LANES // HEAD_DIM).reshape(1, LANES)

    bf = lambda w: w.astype(BF16)
    w_in_b, w_out_b = bf(w_in), bf(w_out)
    ffn_wg_b, ffn_wu_b, ffn_wd_b = bf(ffn_wg), bf(ffn_wu), bf(ffn_wd)
    pool_w_b = bf(pool_w)
    moe_wg_b, moe_wu_b, moe_wd_b = bf(moe_wg), bf(moe_wu), bf(moe_wd)
    ple_gate_b, ple_proj_b = bf(ple_gate_w), bf(ple_proj)
    p_rows = p.reshape(depth, t, -1)

    h = x.reshape(t, d)
    for i in range(depth):
        j = i // 2
        ple_args = (p_rows, row(ple_norm[i]), ple_gate_b, ple_proj_b, i)
        if i % 2 == 0:
            q, k, vt, u = _in_proj(h, row(norm_mix[i]), w_in_b, j, tile_heads(q_norm[j]),
                                   tile_heads(k_norm[j]), cos_t, sin_t, seq, tm)
            logit_bound = (HEAD_DIM ** 0.5 * LOG2E) * jnp.max(jnp.abs(q_norm[j])) * jnp.max(jnp.abs(k_norm[j]))
            a = _attention(q, k, vt, logit_bound, batch, seq, tq, tk, tk_online)
            c = _conv_module(u, conv_w[j], row(conv_b[j]), row(conv_ln_g[j]), row(conv_ln_b[j]),
                             batch, seq, tc)
            h = _out_proj(h, a, c, w_out_b, j, tm)
            h = _ffn(h, row(norm_ffn[i]), ffn_wg_b, ffn_wu_b, ffn_wd_b, j, tm_ffn, tf_ffn)
            h = _ple(h, *ple_args, tm)
        else:
            h = _pool_mixer(h, row(norm_mix[i]), pool_w_b, j, row(pool_scale[j]), batch, seq, tp)
            wr_pad = jnp.pad(router_w[j], ((0, 0), (0, LANES - N_EXPERTS)))
            xn_tiles, meta, counts = _router(h, row(norm_ffn[i]), wr_pad, tm)
            pos1, pos2, tile_expert, n_used = _routing_tables(meta, counts, tm_moe, n_moe_tiles)
            xs = _dispatch(pos1, pos2, xn_tiles, n_moe_tiles * tm_moe, dispatch_chunk)
            ys = _experts(tile_expert, n_used, xs, moe_wg_b, moe_wu_b, moe_wd_b, j, tm_moe, tf_moe)
            h = _combine(pos1, pos2, h, meta, ys, *ple_args, tm)
    return h.reshape(batch, seq, d)
```

```python
import functools

import jax
import jax.numpy as jnp
from jax import lax
from jax.experimental import pallas as pl
from jax.experimental.pallas import tpu as pltpu

F32 = jnp.float32
BF16 = jnp.bfloat16

GRID_W = 64
N_HEADS = 8
KV_HEADS = 2
HEAD_DIM = 64
Q_PER_KV = N_HEADS // KV_HEADS
AXIS_DIM = HEAD_DIM // 2
ROPE_THETA = 10000.0
CONV_WIDTH = 31
POOL_WINDOWS = (2, 4, 8, 16)
N_EXPERTS = 8
TOP_K = 2
EPS = 1e-6

LANES = 128
VMEM_LIMIT = 56 * 1024 * 1024
BF16_SUBLANES = 16
VT_ROWS = HEAD_DIM + BF16_SUBLANES
LOG2E = 1.4426950408889634
Q_SCALE = HEAD_DIM ** -0.5 * LOG2E
MAX_UNSHIFTED_LOGIT = 80.0
CONV_HALO = 16
POOL_HALO = 8


def _params(*sem):
    return pltpu.CompilerParams(dimension_semantics=sem, vmem_limit_bytes=VMEM_LIMIT)


def _rms(x, g):
    return x * lax.rsqrt(jnp.mean(x * x, axis=-1, keepdims=True) + EPS) * g


def _sigmoid(x):
    return 1.0 / (1.0 + jnp.exp(-x))


def _dot(a, b):
    return jnp.dot(a, b, preferred_element_type=F32)


def _ple_update(h, p_ref, g_ref, wgate_ref, wproj_ref):
    gate = _sigmoid(_dot(_rms(h, g_ref[...]).astype(BF16), wgate_ref[...]))
    return h + gate * _dot(p_ref[...].astype(BF16), wproj_ref[...])


def _inproj_kernel(h_ref, g_ref, w_ref, qg_ref, kg_ref, cos_ref, sin_ref,
                   q_ref, k_ref, vt_ref, u_ref, *, q_dim, kv_dim, conv_ch):
    xn = _rms(h_ref[...], g_ref[...]).astype(BF16)
    proj = _dot(xn, w_ref[...])
    tm = proj.shape[0]
    cos = cos_ref[...]
    sin = sin_ref[...]
    lane = lax.broadcasted_iota(jnp.int32, (tm, LANES), 1)
    head0 = lane < HEAD_DIM
    first_half = (lane % HEAD_DIM) < (HEAD_DIM // 2)

    def norm_rope(x, g, scale):
        sq = x * x
        s0 = jnp.sum(jnp.where(head0, sq, 0.0), axis=-1, keepdims=True)
        s1 = jnp.sum(jnp.where(head0, 0.0, sq), axis=-1, keepdims=True)
        ms = jnp.where(head0, s0, s1) * (1.0 / HEAD_DIM)
        y = x * lax.rsqrt(ms + EPS) * g
        partner = jnp.where(first_half,
                            pltpu.roll(y, LANES - HEAD_DIM // 2, 1),
                            pltpu.roll(y, HEAD_DIM // 2, 1))
        return (y * cos + partner * sin) * scale

    for c in range(q_dim // LANES):
        x = proj[:, c * LANES:(c + 1) * LANES]
        q_ref[:, c * LANES:(c + 1) * LANES] = norm_rope(x, qg_ref[...], Q_SCALE).astype(BF16)
    sub = lax.broadcasted_iota(jnp.int32, (VT_ROWS - HEAD_DIM, tm), 0)
    ones_rows = jnp.where(sub == 0, 1.0, 0.0).astype(BF16)
    for c in range(kv_dim // LANES):
        x = proj[:, q_dim + c * LANES:q_dim + (c + 1) * LANES]
        kk = norm_rope(x, kg_ref[...], 1.0).astype(BF16)
        vv_t = proj[:, q_dim + kv_dim + c * LANES:q_dim + kv_dim + (c + 1) * LANES].T
        for j in range(LANES // HEAD_DIM):
            head = c * (LANES // HEAD_DIM) + j
            k_ref[head] = kk[:, j * HEAD_DIM:(j + 1) * HEAD_DIM]
            vt_ref[head, 0:HEAD_DIM, :] = vv_t[j * HEAD_DIM:(j + 1) * HEAD_DIM, :].astype(BF16)
            vt_ref[head, HEAD_DIM:VT_ROWS, :] = ones_rows
    u0 = q_dim + 2 * kv_dim
    u_ref[...] = (proj[:, u0:u0 + conv_ch] * _sigmoid(proj[:, u0 + conv_ch:u0 + 2 * conv_ch])).astype(BF16)


def _in_proj(h, g, w_in, layer, qg, kg, cos_t, sin_t, seq, tm):
    t, d = h.shape
    w_in_dim = w_in.shape[-1]
    q_dim = N_HEADS * HEAD_DIM
    kv_dim = KV_HEADS * HEAD_DIM
    conv_ch = (w_in_dim - q_dim - 2 * kv_dim) // 2
    n_seq_blocks = seq // tm
    kern = functools.partial(_inproj_kernel, q_dim=q_dim, kv_dim=kv_dim, conv_ch=conv_ch)
    return pl.pallas_call(
        kern,
        grid=(t // tm,),
        in_specs=[
            pl.BlockSpec((tm, d), lambda i: (i, 0)),
            pl.BlockSpec((1, d), lambda i: (0, 0)),
            pl.BlockSpec((None, d, w_in_dim), lambda i: (layer, 0, 0)),
            pl.BlockSpec((1, LANES), lambda i: (0, 0)),
            pl.BlockSpec((1, LANES), lambda i: (0, 0)),
            pl.BlockSpec((tm, LANES), lambda i: (i % n_seq_blocks, 0)),
            pl.BlockSpec((tm, LANES), lambda i: (i % n_seq_blocks, 0)),
        ],
        out_specs=[
            pl.BlockSpec((tm, q_dim), lambda i: (i, 0)),
            pl.BlockSpec((KV_HEADS, tm, HEAD_DIM), lambda i: (0, i, 0)),
            pl.BlockSpec((KV_HEADS, VT_ROWS, tm), lambda i: (0, 0, i)),
            pl.BlockSpec((tm, conv_ch), lambda i: (i, 0)),
        ],
        out_shape=[
            jax.ShapeDtypeStruct((t, q_dim), BF16),
            jax.ShapeDtypeStruct((KV_HEADS, t, HEAD_DIM), BF16),
            jax.ShapeDtypeStruct((KV_HEADS, VT_ROWS, t), BF16),
            jax.ShapeDtypeStruct((t, conv_ch), BF16),
        ],
        compiler_params=_params("parallel"),
        name="in_proj",
    )(h, g, w_in, qg, kg, cos_t, sin_t)


def _stack_query_heads(q_ref, q_sc, tq):
    for g in range(Q_PER_KV):
        q_sc[g * tq:(g + 1) * tq, :] = q_ref[:, g * HEAD_DIM:(g + 1) * HEAD_DIM]


def _unstack_query_heads(out, tq):
    return jnp.concatenate([out[g * tq:(g + 1) * tq, :] for g in range(Q_PER_KV)], axis=-1).astype(BF16)


_NT = (((1,), (1,)), ((), ()))


def _attn_unshifted_kernel(q_ref, k_ref, vt_ref, o_ref, q_sc, acc_sc, *, tq, tk, n_kv):
    _stack_query_heads(q_ref, q_sc, tq)
    acc_sc[...] = jnp.zeros(acc_sc.shape, F32)

    def body(j, carry):
        kv0 = pl.multiple_of(j * tk, tk)
        s_t = lax.dot_general(k_ref[0, pl.ds(kv0, tk), :], q_sc[...], _NT,
                              preferred_element_type=F32)
        p_t = jnp.exp2(s_t).astype(BF16)
        acc_sc[...] += _dot(vt_ref[0, :, pl.ds(kv0, tk)], p_t)
        return carry

    lax.fori_loop(0, n_kv, body, 0)
    acc = acc_sc[...]
    out_t = acc[:HEAD_DIM, :] / acc[HEAD_DIM:HEAD_DIM + 1, :]
    o_ref[...] = _unstack_query_heads(out_t.T, tq)


def _attn_online_kernel(q_ref, k_ref, vt_ref, o_ref, q_sc, m_sc, l_sc, acc_sc, *, tq, tk, n_kv):
    _stack_query_heads(q_ref, q_sc, tq)
    m_sc[...] = jnp.full(m_sc.shape, -jnp.inf, F32)
    l_sc[...] = jnp.zeros(l_sc.shape, F32)
    acc_sc[...] = jnp.zeros(acc_sc.shape, F32)

    def body(j, carry):
        kv0 = pl.multiple_of(j * tk, tk)
        s = lax.dot_general(q_sc[...], k_ref[0, pl.ds(kv0, tk), :], _NT, preferred_element_type=F32)
        m_prev = m_sc[...]
        m_new = jnp.maximum(m_prev, jnp.max(s, axis=-1, keepdims=True))
        alpha = jnp.exp2(m_prev - m_new)
        p = jnp.exp2(s - m_new)
        l_sc[...] = alpha * l_sc[...] + jnp.sum(p, axis=-1, keepdims=True)
        v_t = vt_ref[0, 0:HEAD_DIM, pl.ds(kv0, tk)]
        acc_sc[...] = alpha * acc_sc[...] + lax.dot_general(p.astype(BF16), v_t, _NT,
                                                            preferred_element_type=F32)
        m_sc[...] = m_new
        return carry

    lax.fori_loop(0, n_kv, body, 0)
    o_ref[...] = _unstack_query_heads(acc_sc[...] / l_sc[...], tq)


def _attention_call(kern, scratch, name, q, k, vt, batch, seq, tq, tk):
    t = q.shape[0]
    n_q = seq // tq
    gw = Q_PER_KV * HEAD_DIM
    return pl.pallas_call(
        functools.partial(kern, tq=tq, tk=tk, n_kv=seq // tk),
        grid=(batch, KV_HEADS, n_q),
        in_specs=[
            pl.BlockSpec((tq, gw), lambda b, h, i: (b * n_q + i, h)),
            pl.BlockSpec((1, seq, HEAD_DIM), lambda b, h, i: (h, b, 0)),
            pl.BlockSpec((1, VT_ROWS, seq), lambda b, h, i: (h, 0, b)),
        ],
        out_specs=pl.BlockSpec((tq, gw), lambda b, h, i: (b * n_q + i, h)),
        out_shape=jax.ShapeDtypeStruct((t, N_HEADS * HEAD_DIM), BF16),
        scratch_shapes=[pltpu.VMEM((Q_PER_KV * tq, HEAD_DIM), BF16)] + scratch,
        compiler_params=_params("parallel", "parallel", "parallel"),
        name=name,
    )(q, k, vt)


def _attention(q, k, vt, logit_bound, batch, seq, tq, tk, tk_online):
    m = Q_PER_KV * tq

    def unshifted(q, k, vt):
        return _attention_call(_attn_unshifted_kernel, [pltpu.VMEM((VT_ROWS, m), F32)],
                               "attention", q, k, vt, batch, seq, tq, tk)

    def online(q, k, vt):
        scratch = [pltpu.VMEM((m, 1), F32), pltpu.VMEM((m, 1), F32), pltpu.VMEM((m, HEAD_DIM), F32)]
        return _attention_call(_attn_online_kernel, scratch, "attention_online",
                               q, k, vt, batch, seq, tq, tk_online)

    return lax.cond(logit_bound <= MAX_UNSHIFTED_LOGIT, unshifted, online, q, k, vt)


def _conv_kernel(prev_ref, main_ref, next_ref, w_ref, b_ref, g_ref, beta_ref, o_ref, ext_sc, y_sc,
                 *, tc, n_blocks):
    i = pl.program_id(1)
    ch = main_ref.shape[1]
    prev = prev_ref[...].astype(F32)
    nxt = next_ref[...].astype(F32)
    ext_sc[0:CONV_HALO, :] = jnp.where(i == 0, 0.0, prev)
    ext_sc[CONV_HALO:CONV_HALO + tc, :] = main_ref[...].astype(F32)
    ext_sc[CONV_HALO + tc:, :] = jnp.where(i == n_blocks - 1, 0.0, nxt)
    base = CONV_HALO - CONV_WIDTH // 2
    for c in range(ch // LANES):
        cols = slice(c * LANES, (c + 1) * LANES)
        acc = jnp.zeros((tc, LANES), F32) + b_ref[:, cols]
        for kk in range(CONV_WIDTH):
            acc = acc + ext_sc[base + kk:base + kk + tc, cols] * w_ref[kk:kk + 1, cols]
        y_sc[:, cols] = acc
    y = y_sc[...]
    mu = jnp.mean(y, axis=-1, keepdims=True)
    yc = y - mu
    var = jnp.mean(yc * yc, axis=-1, keepdims=True)
    z = yc * lax.rsqrt(var + EPS) * g_ref[...] + beta_ref[...]
    o_ref[...] = (z * _sigmoid(z)).astype(BF16)


def _conv_module(u, conv_w, conv_b, ln_g, ln_b, batch, seq, tc):
    t, ch = u.shape
    n_blocks = seq // tc
    hb = tc // CONV_HALO
    n_halo = seq // CONV_HALO
    kern = functools.partial(_conv_kernel, tc=tc, n_blocks=n_blocks)
    return pl.pallas_call(
        kern,
        grid=(batch, n_blocks),
        in_specs=[
            pl.BlockSpec((CONV_HALO, ch), lambda b, i: (b * n_halo + jnp.maximum(i * hb - 1, 0), 0)),
            pl.BlockSpec((tc, ch), lambda b, i: (b * n_blocks + i, 0)),
            pl.BlockSpec((CONV_HALO, ch), lambda b, i: (b * n_halo + jnp.minimum((i + 1) * hb, n_halo - 1), 0)),
            pl.BlockSpec((CONV_WIDTH, ch), lambda b, i: (0, 0)),
            pl.BlockSpec((1, ch), lambda b, i: (0, 0)),
            pl.BlockSpec((1, ch), lambda b, i: (0, 0)),
            pl.BlockSpec((1, ch), lambda b, i: (0, 0)),
        ],
        out_specs=pl.BlockSpec((tc, ch), lambda b, i: (b * n_blocks + i, 0)),
        out_shape=jax.ShapeDtypeStruct((t, ch), BF16),
        scratch_shapes=[
            pltpu.VMEM((tc + 2 * CONV_HALO, ch), F32),
            pltpu.VMEM((tc, ch), F32),
        ],
        compiler_params=_params("parallel", "parallel"),
        name="conv_module",
    )(u, u, u, conv_w, conv_b, ln_g, ln_b)


def _outproj_kernel(h_ref, a_ref, c_ref, wa_ref, wc_ref, o_ref):
    o_ref[...] = h_ref[...] + _dot(a_ref[...], wa_ref[...]) + _dot(c_ref[...], wc_ref[...])


def _out_proj(h, a, c, w_out, layer, tm):
    t, d = h.shape
    assert a.shape[1] == c.shape[1] and a.shape[1] + c.shape[1] == w_out.shape[1]
    return pl.pallas_call(
        _outproj_kernel,
        grid=(t // tm,),
        in_specs=[
            pl.BlockSpec((tm, d), lambda i: (i, 0)),
            pl.BlockSpec((tm, a.shape[1]), lambda i: (i, 0)),
            pl.BlockSpec((tm, c.shape[1]), lambda i: (i, 0)),
            pl.BlockSpec((None, a.shape[1], d), lambda i: (layer, 0, 0)),
            pl.BlockSpec((None, c.shape[1], d), lambda i: (layer, 1, 0)),
        ],
        out_specs=pl.BlockSpec((tm, d), lambda i: (i, 0)),
        out_shape=jax.ShapeDtypeStruct((t, d), F32),
        compiler_params=_params("parallel"),
        name="out_proj",
    )(h, a, c, w_out, w_out)


def _ffn_kernel(h_ref, g_ref, wg_ref, wu_ref, wd_ref, o_ref, xn_sc, acc_sc):
    f = pl.program_id(1)

    @pl.when(f == 0)
    def _():
        xn_sc[...] = _rms(h_ref[...], g_ref[...]).astype(BF16)
        acc_sc[...] = jnp.zeros(acc_sc.shape, F32)

    xn = xn_sc[...]
    a = _dot(xn, wg_ref[...])
    b = _dot(xn, wu_ref[...])
    mid = (a * _sigmoid(a) * b).astype(BF16)
    acc_sc[...] += _dot(mid, wd_ref[...])

    @pl.when(f == pl.num_programs(1) - 1)
    def _():
        o_ref[...] = h_ref[...] + acc_sc[...]


def _ffn(h, g, wg, wu, wd, layer, tm, tf):
    t, d = h.shape
    f_dim = wg.shape[-1]
    return pl.pallas_call(
        _ffn_kernel,
        grid=(t // tm, f_dim // tf),
        in_specs=[
            pl.BlockSpec((tm, d), lambda i, f: (i, 0)),
            pl.BlockSpec((1, d), lambda i, f: (0, 0)),
            pl.BlockSpec((None, d, tf), lambda i, f: (layer, 0, f)),
            pl.BlockSpec((None, d, tf), lambda i, f: (layer, 0, f)),
            pl.BlockSpec((None, tf, d), lambda i, f: (layer, f, 0)),
        ],
        out_specs=pl.BlockSpec((tm, d), lambda i, f: (i, 0)),
        out_shape=jax.ShapeDtypeStruct((t, d), F32),
        scratch_shapes=[pltpu.VMEM((tm, d), BF16), pltpu.VMEM((tm, d), F32)],
        compiler_params=_params("parallel", "arbitrary"),
        name="dense_swiglu",
    )(h, g, wg, wu, wd)


def _pool_kernel(prev_ref, main_ref, next_ref, g_ref, w_ref, sc_ref, o_ref, *, tp, seq):
    i = pl.program_id(1)
    g = g_ref[...]
    h_main = main_ref[...]
    hn_main = _rms(h_main, g)
    ext = jnp.concatenate([_rms(prev_ref[...], g), hn_main, _rms(next_ref[...], g)], axis=0).astype(BF16)
    rows = tp + 2 * POOL_HALO
    t_pos = i * tp + lax.broadcasted_iota(jnp.int32, (tp, rows), 0)
    j_pos = i * tp - POOL_HALO + lax.broadcasted_iota(jnp.int32, (tp, rows), 1)
    in_seq = (j_pos >= 0) & (j_pos < seq)
    t_col = i * tp + lax.broadcasted_iota(jnp.int32, (tp, 1), 0)
    gc = w_ref.shape[1]
    for gi, win in enumerate(POOL_WINDOWS):
        left = win // 2
        right = win - 1 - left
        band = (in_seq & (j_pos >= t_pos - left) & (j_pos <= t_pos + right)).astype(BF16)
        cnt = jnp.minimum(t_col + right, seq - 1) - jnp.maximum(t_col - left, 0) + 1
        cols = slice(gi * gc, (gi + 1) * gc)
        win_sum = _dot(band, ext[:, cols])
        y = (win_sum / cnt.astype(F32) - hn_main[:, cols]).astype(BF16)
        o_ref[:, cols] = h_main[:, cols] + _dot(y, w_ref[gi]) * sc_ref[:, cols]


def _pool_mixer(h, g, pool_w, layer, pool_scale, batch, seq, tp):
    t, d = h.shape
    n_blocks = seq // tp
    hb = tp // POOL_HALO
    n_halo = seq // POOL_HALO
    kern = functools.partial(_pool_kernel, tp=tp, seq=seq)
    return pl.pallas_call(
        kern,
        grid=(batch, n_blocks),
        in_specs=[
            pl.BlockSpec((POOL_HALO, d), lambda b, i: (b * n_halo + jnp.maximum(i * hb - 1, 0), 0)),
            pl.BlockSpec((tp, d), lambda b, i: (b * n_blocks + i, 0)),
            pl.BlockSpec((POOL_HALO, d), lambda b, i: (b * n_halo + jnp.minimum((i + 1) * hb, n_halo - 1), 0)),
            pl.BlockSpec((1, d), lambda b, i: (0, 0)),
            pl.BlockSpec((None,) + pool_w.shape[1:], lambda b, i: (layer, 0, 0, 0)),
            pl.BlockSpec((1, d), lambda b, i: (0, 0)),
        ],
        out_specs=pl.BlockSpec((tp, d), lambda b, i: (b * n_blocks + i, 0)),
        out_shape=jax.ShapeDtypeStruct((t, d), F32),
        compiler_params=_params("parallel", "parallel"),
        name="pool_mixer",
    )(h, h, h, g, pool_w, pool_scale)


SUBLANES = 8
META_E1, META_E2, META_RANK1, META_RANK2, META_P1, META_P2 = range(6)


def _to_token_tiles(ref, x):
    tm, d = x.shape
    for a in range(d // LANES):
        ref[pl.ds(a, tm, stride=SUBLANES), :] = x[:, a * LANES:(a + 1) * LANES]


def _from_token_tiles(ref, tm, d):
    return [ref[pl.ds(a, tm, stride=SUBLANES), :] for a in range(d // LANES)]


def _router_kernel(h_ref, g_ref, wr_ref, xn_ref, meta_ref, counts_ref):
    @pl.when(pl.program_id(0) == 0)
    def _():
        counts_ref[...] = jnp.zeros(counts_ref.shape, F32)

    xn = _rms(h_ref[...], g_ref[...])
    _to_token_tiles(xn_ref, xn)
    logits = jnp.dot(xn, wr_ref[...], preferred_element_type=F32, precision=lax.Precision.HIGHEST)
    tm = logits.shape[0]
    lane = lax.broadcasted_iota(jnp.int32, logits.shape, 1)
    logits = jnp.where(lane < N_EXPERTS, logits, -jnp.inf)
    v1 = jnp.max(logits, axis=-1, keepdims=True)
    i1 = jnp.min(jnp.where(logits == v1, lane, LANES), axis=-1, keepdims=True)
    rest = jnp.where(lane == i1, -jnp.inf, logits)
    v2 = jnp.max(rest, axis=-1, keepdims=True)
    i2 = jnp.min(jnp.where(rest == v2, lane, LANES), axis=-1, keepdims=True)
    e2 = jnp.exp(v2 - v1)
    p1 = 1.0 / (1.0 + e2)
    p2 = e2 * p1
    chosen = (lane == i1) | (lane == i2)
    earlier = (lax.broadcasted_iota(jnp.int32, (tm, tm), 0) > lax.broadcasted_iota(jnp.int32, (tm, tm), 1))
    before = _dot(earlier.astype(BF16), chosen.astype(BF16)) + counts_ref[...]
    rank1 = jnp.sum(jnp.where(lane == i1, before, 0.0), axis=-1, keepdims=True)
    rank2 = jnp.sum(jnp.where(lane == i2, before, 0.0), axis=-1, keepdims=True)
    counts_ref[...] += jnp.sum(chosen.astype(F32), axis=0, keepdims=True)
    meta = jnp.zeros(logits.shape, F32)
    for col, val in ((META_E1, i1.astype(F32)), (META_E2, i2.astype(F32)), (META_RANK1, rank1),
                     (META_RANK2, rank2), (META_P1, p1), (META_P2, p2)):
        meta = jnp.where(lane == col, val, meta)
    meta_ref[...] = meta


def _router(h, g, wr_pad, tm):
    t, d = h.shape
    return pl.pallas_call(
        _router_kernel,
        grid=(t // tm,),
        in_specs=[
            pl.BlockSpec((tm, d), lambda i: (i, 0)),
            pl.BlockSpec((1, d), lambda i: (0, 0)),
            pl.BlockSpec(wr_pad.shape, lambda i: (0, 0)),
        ],
        out_specs=[
            pl.BlockSpec((tm * SUBLANES, LANES), lambda i: (i, 0)),
            pl.BlockSpec((tm, LANES), lambda i: (i, 0)),
            pl.BlockSpec((1, LANES), lambda i: (0, 0)),
        ],
        out_shape=[
            jax.ShapeDtypeStruct((t * SUBLANES, LANES), F32),
            jax.ShapeDtypeStruct((t, LANES), F32),
            jax.ShapeDtypeStruct((1, LANES), F32),
        ],
        compiler_params=_params("arbitrary"),
        name="router",
    )(h, g, wr_pad)


def _routing_tables(meta, counts, tile_rows, n_tiles):
    cnt = counts[0, :N_EXPERTS].astype(jnp.int32)
    padded = (cnt + tile_rows - 1) // tile_rows * tile_rows
    ends = jnp.cumsum(padded)
    starts = ends - padded
    pos1 = starts[meta[:, META_E1].astype(jnp.int32)] + meta[:, META_RANK1].astype(jnp.int32)
    pos2 = starts[meta[:, META_E2].astype(jnp.int32)] + meta[:, META_RANK2].astype(jnp.int32)
    n_used = ends[-1] // tile_rows
    tile_start = jnp.minimum(jnp.arange(n_tiles, dtype=jnp.int32), n_used - 1) * tile_rows
    tile_expert = jnp.sum((tile_start[:, None] >= ends[None, :]).astype(jnp.int32), axis=-1)
    token = jnp.arange(meta.shape[0], dtype=jnp.int32)
    src_token = jnp.zeros((n_tiles * tile_rows,), jnp.int32)
    src_token = src_token.at[pos1].set(token, unique_indices=True).at[pos2].set(token, unique_indices=True)
    return pos1, pos2, src_token, tile_expert, n_used.reshape(1)


def _token_rows(ref, i, n=1):
    return ref.at[pl.ds(pl.multiple_of(i * SUBLANES, SUBLANES), n * SUBLANES)]


def _expert_kernel(tile_expert_ref, n_used_ref, src_ref, xn_hbm, wg_ref, wu_ref, wd_ref, ys_ref,
                   xg_sc, x_sc, acc_sc, sems, *, tm, d):
    del tile_expert_ref
    r = pl.program_id(0)
    f = pl.program_id(1)
    last = pl.num_programs(1) - 1
    n_used = n_used_ref[0]
    used = r < n_used
    slot = r % 2

    def gather(tile, slot):
        base = tile * tm

        def issue(i, carry):
            pltpu.make_async_copy(_token_rows(xn_hbm, src_ref[base + i]),
                                  _token_rows(xg_sc.at[slot], i), sems.at[slot]).start()
            return carry

        lax.fori_loop(0, tm, issue, 0, unroll=8)

    @pl.when((r == 0) & (f == 0))
    def _():
        gather(0, 0)

    @pl.when((r + 1 < n_used) & (f == 0))
    def _():
        gather(r + 1, 1 - slot)

    @pl.when(used & (f == 0))
    def _():
        pltpu.make_async_copy(_token_rows(xn_hbm, 0, tm), xg_sc.at[slot], sems.at[slot]).wait()
        x_sc[...] = jnp.concatenate(_from_token_tiles(xg_sc.at[slot], tm, d), axis=-1).astype(BF16)

    @pl.when(used)
    def _():
        x = x_sc[...]
        a = _dot(x, wg_ref[...])
        b = _dot(x, wu_ref[...])
        y = _dot((a * _sigmoid(a) * b).astype(BF16), wd_ref[...])

        @pl.when(f == 0)
        def _():
            acc_sc[...] = y

        @pl.when((f > 0) & (f < last))
        def _():
            acc_sc[...] += y

        @pl.when(f == last)
        def _():
            _to_token_tiles(ys_ref, acc_sc[...] + y)

    @pl.when(jnp.logical_not(used) & (f == last))
    def _():
        ys_ref[...] = jnp.zeros(ys_ref.shape, F32)


def _experts(tile_expert, n_used, src_token, xn_tiles, wg, wu, wd, layer, tm, tf):
    _, n_e, d, f_dim = wg.shape
    n_tiles = src_token.shape[0] // tm
    assert f_dim // tf >= 2
    return pl.pallas_call(
        functools.partial(_expert_kernel, tm=tm, d=d),
        grid_spec=pltpu.PrefetchScalarGridSpec(
            num_scalar_prefetch=3,
            grid=(n_tiles, f_dim // tf),
            in_specs=[
                pl.BlockSpec(memory_space=pl.ANY),
                pl.BlockSpec((None, None, d, tf), lambda r, f, te, nu, src: (layer, te[r], 0, f)),
                pl.BlockSpec((None, None, d, tf), lambda r, f, te, nu, src: (layer, te[r], 0, f)),
                pl.BlockSpec((None, None, tf, d), lambda r, f, te, nu, src: (layer, te[r], f, 0)),
            ],
            out_specs=pl.BlockSpec((tm * SUBLANES, LANES), lambda r, f, te, nu, src: (r, 0)),
            scratch_shapes=[pltpu.VMEM((2, tm * SUBLANES, LANES), F32), pltpu.VMEM((tm, d), BF16),
                            pltpu.VMEM((tm, d), F32), pltpu.SemaphoreType.DMA((2,))],
        ),
        out_shape=jax.ShapeDtypeStruct((n_tiles * tm * SUBLANES, LANES), F32),
        compiler_params=_params("arbitrary", "arbitrary"),
        name="moe_experts",
    )(tile_expert, n_used, src_token, xn_tiles, wg, wu, wd)


def _combine_kernel(pos1_ref, pos2_ref, h_ref, meta_ref, ys_hbm, p_ref, g_ref, wgate_ref, wproj_ref,
                    o_ref, y_sc, sems, *, tm, d):
    i = pl.program_id(0)
    slot = i % 2

    def gather(tile, slot):
        base = tile * tm

        def issue(r, carry):
            for k, pos_ref in enumerate((pos1_ref, pos2_ref)):
                pltpu.make_async_copy(_token_rows(ys_hbm, pos_ref[base + r]),
                                      _token_rows(y_sc.at[TOP_K * slot + k], r), sems.at[slot]).start()
            return carry

        lax.fori_loop(0, tm, issue, 0, unroll=8)

    @pl.when(i == 0)
    def _():
        gather(0, 0)

    @pl.when(i + 1 < pl.num_programs(0))
    def _():
        gather(i + 1, 1 - slot)

    for k in range(TOP_K):
        pltpu.make_async_copy(_token_rows(ys_hbm, 0, tm), y_sc.at[TOP_K * slot + k], sems.at[slot]).wait()
    meta = meta_ref[...]
    p1 = meta[:, META_P1:META_P1 + 1]
    p2 = meta[:, META_P2:META_P2 + 1]
    y1 = _from_token_tiles(y_sc.at[TOP_K * slot], tm, d)
    y2 = _from_token_tiles(y_sc.at[TOP_K * slot + 1], tm, d)
    for a in range(d // LANES):
        cols = slice(a * LANES, (a + 1) * LANES)
        o_ref[:, cols] = h_ref[:, cols] + p1 * y1[a] + p2 * y2[a]
    o_ref[...] = _ple_update(o_ref[...], p_ref, g_ref, wgate_ref, wproj_ref)


def _combine(pos1, pos2, h, meta, ys, p, g, w_gate, w_proj, layer, tm):
    t, d = h.shape
    return pl.pallas_call(
        functools.partial(_combine_kernel, tm=tm, d=d),
        grid_spec=pltpu.PrefetchScalarGridSpec(
            num_scalar_prefetch=2,
            grid=(t // tm,),
            in_specs=[
                pl.BlockSpec((tm, d), lambda i, p1, p2: (i, 0)),
                pl.BlockSpec((tm, LANES), lambda i, p1, p2: (i, 0)),
                pl.BlockSpec(memory_space=pl.ANY),
            ] + _ple_specs(p, w_gate, w_proj, layer, tm, d),
            out_specs=pl.BlockSpec((tm, d), lambda i, p1, p2: (i, 0)),
            scratch_shapes=[pltpu.VMEM((2 * TOP_K, tm * SUBLANES, LANES), F32),
                            pltpu.SemaphoreType.DMA((2,))],
        ),
        out_shape=jax.ShapeDtypeStruct((t, d), F32),
        compiler_params=_params("arbitrary"),
        name="moe_combine",
    )(pos1, pos2, h, meta, ys, p, g, w_gate, w_proj)


def _ple_kernel(h_ref, p_ref, g_ref, wgate_ref, wproj_ref, o_ref):
    o_ref[...] = _ple_update(h_ref[...], p_ref, g_ref, wgate_ref, wproj_ref)


def _ple_specs(p, w_gate, w_proj, layer, tm, d):
    def spec(block, index):
        return pl.BlockSpec(block, lambda i, *prefetch: index(i))
    return [
        spec((None, tm, p.shape[-1]), lambda i: (layer, i, 0)),
        spec((1, d), lambda i: (0, 0)),
        spec((None,) + w_gate.shape[1:], lambda i: (layer, 0, 0)),
        spec((None,) + w_proj.shape[1:], lambda i: (layer, 0, 0)),
    ]


def _ple(h, p, g, w_gate, w_proj, layer, tm):
    t, d = h.shape
    return pl.pallas_call(
        _ple_kernel,
        grid=(t // tm,),
        in_specs=[pl.BlockSpec((tm, d), lambda i: (i, 0))] + _ple_specs(p, w_gate, w_proj, layer, tm, d),
        out_specs=pl.BlockSpec((tm, d), lambda i: (i, 0)),
        out_shape=jax.ShapeDtypeStruct((t, d), F32),
        compiler_params=_params("parallel"),
        name="per_layer_input",
    )(h, p, g, w_gate, w_proj)


def _rope_tables(seq):
    rows = seq // GRID_W
    r = jnp.broadcast_to(jnp.arange(rows, dtype=F32)[:, None], (rows, GRID_W)).reshape(seq)
    c = jnp.broadcast_to(jnp.arange(GRID_W, dtype=F32)[None, :], (rows, GRID_W)).reshape(seq)
    inv = ROPE_THETA ** (-jnp.arange(0, AXIS_DIM, 2, dtype=F32) / AXIS_DIM)
    ang = jnp.concatenate([r[:, None] * inv, c[:, None] * inv], axis=-1)
    cos, sin = jnp.cos(ang), jnp.sin(ang)
    reps = LANES // HEAD_DIM
    return (jnp.tile(jnp.concatenate([cos, cos], axis=-1), (1, reps)),
            jnp.tile(jnp.concatenate([-sin, sin], axis=-1), (1, reps)))


def _tile(n, want):
    t = min(n, want)
    assert n % t == 0, (n, t)
    return t


def kernel(x, p, norm_mix, norm_ffn, w_in, q_norm, k_norm, conv_w, conv_b, conv_ln_g, conv_ln_b, w_out,
           ffn_wg, ffn_wu, ffn_wd, pool_w, pool_scale, router_w, moe_wg, moe_wu, moe_wd, ple_norm,
           ple_gate_w, ple_proj):
    batch, seq, d = x.shape
    depth = p.shape[0]
    t = batch * seq
    q_dim = N_HEADS * HEAD_DIM
    assert seq % GRID_W == 0 and d % LANES == 0

    tm = _tile(seq, 512)
    tm_ffn = _tile(t, 1024)
    tq = _tile(seq, 256)
    tk = _tile(seq, 8192)
    tk_online = _tile(seq, 512)
    tc = _tile(seq, 256)
    tp = _tile(seq, 256)
    tf_ffn = ffn_wg.shape[2] // 2
    tf_moe = moe_wg.shape[3] // 2
    tm_moe = _tile(t, 1024)
    n_moe_tiles = TOP_K * t // tm_moe + N_EXPERTS

    cos_t, sin_t = _rope_tables(seq)
    row = lambda v: v.reshape(1, -1)
    tile_heads = lambda v: jnp.tile(v, LANES // HEAD_DIM).reshape(1, LANES)

    bf = lambda w: w.astype(BF16)
    w_in_b, w_out_b = bf(w_in), bf(w_out)
    ffn_wg_b, ffn_wu_b, ffn_wd_b = bf(ffn_wg), bf(ffn_wu), bf(ffn_wd)
    pool_w_b = bf(pool_w)
    moe_wg_b, moe_wu_b, moe_wd_b = bf(moe_wg), bf(moe_wu), bf(moe_wd)
    ple_gate_b, ple_proj_b = bf(ple_gate_w), bf(ple_proj)
    p_rows = p.reshape(depth, t, -1)

    h = x.reshape(t, d)
    for i in range(depth):
        j = i // 2
        ple_args = (p_rows, row(ple_norm[i]), ple_gate_b, ple_proj_b, i)
        if i % 2 == 0:
            q, k, vt, u = _in_proj(h, row(norm_mix[i]), w_in_b, j, tile_heads(q_norm[j]),
                                   tile_heads(k_norm[j]), cos_t, sin_t, seq, tm)
            logit_bound = (HEAD_DIM ** 0.5 * LOG2E) * jnp.max(jnp.abs(q_norm[j])) * jnp.max(jnp.abs(k_norm[j]))
            a = _attention(q, k, vt, logit_bound, batch, seq, tq, tk, tk_online)
            c = _conv_module(u, conv_w[j], row(conv_b[j]), row(conv_ln_g[j]), row(conv_ln_b[j]),
                             batch, seq, tc)
            h = _out_proj(h, a, c, w_out_b, j, tm)
            h = _ffn(h, row(norm_ffn[i]), ffn_wg_b, ffn_wu_b, ffn_wd_b, j, tm_ffn, tf_ffn)
            h = _ple(h, *ple_args, tm)
        else:
            h = _pool_mixer(h, row(norm_mix[i]), pool_w_b, j, row(pool_scale[j]), batch, seq, tp)
            wr_pad = jnp.pad(router_w[j], ((0, 0), (0, LANES - N_EXPERTS)))
            xn_tiles, meta, counts = _router(h, row(norm_ffn[i]), wr_pad, tm)
            pos1, pos2, src_token, tile_expert, n_used = _routing_tables(meta, counts, tm_moe, n_moe_tiles)
            ys = _experts(tile_expert, n_used, src_token, xn_tiles, moe_wg_b, moe_wu_b, moe_wd_b,
                          j, tm_moe, tf_moe)
            h = _combine(pos1, pos2, h, meta, ys, *ple_args, tm)
    return h.reshape(batch, seq, d)
```

```python
import functools

import jax
import jax.numpy as jnp
from jax import lax
from jax.experimental import pallas as pl
from jax.experimental.pallas import tpu as pltpu

F32 = jnp.float32
BF16 = jnp.bfloat16

GRID_W = 64
N_HEADS = 8
KV_HEADS = 2
HEAD_DIM = 64
Q_PER_KV = N_HEADS // KV_HEADS
AXIS_DIM = HEAD_DIM // 2
ROPE_THETA = 10000.0
CONV_WIDTH = 31
POOL_WINDOWS = (2, 4, 8, 16)
N_EXPERTS = 8
TOP_K = 2
EPS = 1e-6

LANES = 128
SUBLANES = 8
VMEM_LIMIT = 56 * 1024 * 1024
BF16_SUBLANES = 16
VT_ROWS = HEAD_DIM + BF16_SUBLANES
LOG2E = 1.4426950408889634
Q_SCALE = HEAD_DIM ** -0.5 * LOG2E
MAX_UNSHIFTED_LOGIT = 80.0
CONV_HALO = 16
CONV_ROWS = 64
POOL_HALO = 8


def _params(*sem):
    return pltpu.CompilerParams(dimension_semantics=sem, vmem_limit_bytes=VMEM_LIMIT)


def _rms(x, g):
    return x * lax.rsqrt(jnp.mean(x * x, axis=-1, keepdims=True) + EPS) * g


def _sigmoid(x):
    return 1.0 / (1.0 + jnp.exp(-x))


def _dot(a, b):
    return jnp.dot(a, b, preferred_element_type=F32)


def _ple_update(h, p_ref, g_ref, wgate_ref, wproj_ref):
    gate = _sigmoid(_dot(_rms(h, g_ref[...]).astype(BF16), wgate_ref[...]))
    return h + gate * _dot(p_ref[...].astype(BF16), wproj_ref[...])


def _inproj_kernel(h_ref, g_ref, w_ref, qg_ref, kg_ref, cos_ref, sin_ref,
                   q_ref, k_ref, vt_ref, u_ref, *, q_dim, kv_dim, conv_ch):
    xn = _rms(h_ref[...], g_ref[...]).astype(BF16)
    proj = _dot(xn, w_ref[...])
    tm = proj.shape[0]
    cos = cos_ref[...]
    sin = sin_ref[...]
    lane = lax.broadcasted_iota(jnp.int32, (tm, LANES), 1)
    head0 = lane < HEAD_DIM
    first_half = (lane % HEAD_DIM) < (HEAD_DIM // 2)

    def norm_rope(x, g, scale):
        sq = x * x
        s0 = jnp.sum(jnp.where(head0, sq, 0.0), axis=-1, keepdims=True)
        s1 = jnp.sum(jnp.where(head0, 0.0, sq), axis=-1, keepdims=True)
        ms = jnp.where(head0, s0, s1) * (1.0 / HEAD_DIM)
        y = x * lax.rsqrt(ms + EPS) * g
        partner = jnp.where(first_half,
                            pltpu.roll(y, LANES - HEAD_DIM // 2, 1),
                            pltpu.roll(y, HEAD_DIM // 2, 1))
        return (y * cos + partner * sin) * scale

    for c in range(q_dim // LANES):
        x = proj[:, c * LANES:(c + 1) * LANES]
        q_ref[:, c * LANES:(c + 1) * LANES] = norm_rope(x, qg_ref[...], Q_SCALE).astype(BF16)
    sub = lax.broadcasted_iota(jnp.int32, (VT_ROWS - HEAD_DIM, tm), 0)
    ones_rows = jnp.where(sub == 0, 1.0, 0.0).astype(BF16)
    for c in range(kv_dim // LANES):
        x = proj[:, q_dim + c * LANES:q_dim + (c + 1) * LANES]
        kk = norm_rope(x, kg_ref[...], 1.0).astype(BF16)
        vv_t = proj[:, q_dim + kv_dim + c * LANES:q_dim + kv_dim + (c + 1) * LANES].T
        for j in range(LANES // HEAD_DIM):
            head = c * (LANES // HEAD_DIM) + j
            k_ref[head] = kk[:, j * HEAD_DIM:(j + 1) * HEAD_DIM]
            vt_ref[head, 0:HEAD_DIM, :] = vv_t[j * HEAD_DIM:(j + 1) * HEAD_DIM, :].astype(BF16)
            vt_ref[head, HEAD_DIM:VT_ROWS, :] = ones_rows
    u0 = q_dim + 2 * kv_dim
    u_ref[...] = (proj[:, u0:u0 + conv_ch] * _sigmoid(proj[:, u0 + conv_ch:u0 + 2 * conv_ch])).astype(BF16)


def _in_proj(h, g, w_in, layer, qg, kg, cos_t, sin_t, seq, tm):
    t, d = h.shape
    w_in_dim = w_in.shape[-1]
    q_dim = N_HEADS * HEAD_DIM
    kv_dim = KV_HEADS * HEAD_DIM
    conv_ch = (w_in_dim - q_dim - 2 * kv_dim) // 2
    n_seq_blocks = seq // tm
    kern = functools.partial(_inproj_kernel, q_dim=q_dim, kv_dim=kv_dim, conv_ch=conv_ch)
    return pl.pallas_call(
        kern,
        grid=(t // tm,),
        in_specs=[
            pl.BlockSpec((tm, d), lambda i: (i, 0)),
            pl.BlockSpec((1, d), lambda i: (0, 0)),
            pl.BlockSpec((None, d, w_in_dim), lambda i: (layer, 0, 0)),
            pl.BlockSpec((1, LANES), lambda i: (0, 0)),
            pl.BlockSpec((1, LANES), lambda i: (0, 0)),
            pl.BlockSpec((tm, LANES), lambda i: (i % n_seq_blocks, 0)),
            pl.BlockSpec((tm, LANES), lambda i: (i % n_seq_blocks, 0)),
        ],
        out_specs=[
            pl.BlockSpec((tm, q_dim), lambda i: (i, 0)),
            pl.BlockSpec((KV_HEADS, tm, HEAD_DIM), lambda i: (0, i, 0)),
            pl.BlockSpec((KV_HEADS, VT_ROWS, tm), lambda i: (0, 0, i)),
            pl.BlockSpec((tm, conv_ch), lambda i: (i, 0)),
        ],
        out_shape=[
            jax.ShapeDtypeStruct((t, q_dim), BF16),
            jax.ShapeDtypeStruct((KV_HEADS, t, HEAD_DIM), BF16),
            jax.ShapeDtypeStruct((KV_HEADS, VT_ROWS, t), BF16),
            jax.ShapeDtypeStruct((t, conv_ch), BF16),
        ],
        compiler_params=_params("parallel"),
        name="in_proj",
    )(h, g, w_in, qg, kg, cos_t, sin_t)


def _stack_query_heads(q_ref, q_sc, tq):
    for g in range(Q_PER_KV):
        q_sc[g * tq:(g + 1) * tq, :] = q_ref[:, g * HEAD_DIM:(g + 1) * HEAD_DIM]


def _unstack_query_heads(out, tq):
    return jnp.concatenate([out[g * tq:(g + 1) * tq, :] for g in range(Q_PER_KV)], axis=-1).astype(BF16)


_NT = (((1,), (1,)), ((), ()))


def _attn_unshifted_kernel(q_ref, k_ref, vt_ref, o_ref, q_sc, acc_sc, *, tq, tk, n_kv):
    _stack_query_heads(q_ref, q_sc, tq)
    acc_sc[...] = jnp.zeros(acc_sc.shape, F32)

    def body(j, carry):
        kv0 = pl.multiple_of(j * tk, tk)
        s_t = lax.dot_general(k_ref[0, pl.ds(kv0, tk), :], q_sc[...], _NT,
                              preferred_element_type=F32)
        p_t = jnp.exp2(s_t).astype(BF16)
        acc_sc[...] += _dot(vt_ref[0, :, pl.ds(kv0, tk)], p_t)
        return carry

    lax.fori_loop(0, n_kv, body, 0)
    acc = acc_sc[...]
    out_t = acc[:HEAD_DIM, :] / acc[HEAD_DIM:HEAD_DIM + 1, :]
    o_ref[...] = _unstack_query_heads(out_t.T, tq)


def _attn_online_kernel(q_ref, k_ref, vt_ref, o_ref, q_sc, m_sc, l_sc, acc_sc, *, tq, tk, n_kv):
    _stack_query_heads(q_ref, q_sc, tq)
    m_sc[...] = jnp.full(m_sc.shape, -jnp.inf, F32)
    l_sc[...] = jnp.zeros(l_sc.shape, F32)
    acc_sc[...] = jnp.zeros(acc_sc.shape, F32)

    def body(j, carry):
        kv0 = pl.multiple_of(j * tk, tk)
        s = lax.dot_general(q_sc[...], k_ref[0, pl.ds(kv0, tk), :], _NT, preferred_element_type=F32)
        m_prev = m_sc[...]
        m_new = jnp.maximum(m_prev, jnp.max(s, axis=-1, keepdims=True))
        alpha = jnp.exp2(m_prev - m_new)
        p = jnp.exp2(s - m_new)
        l_sc[...] = alpha * l_sc[...] + jnp.sum(p, axis=-1, keepdims=True)
        v_t = vt_ref[0, 0:HEAD_DIM, pl.ds(kv0, tk)]
        acc_sc[...] = alpha * acc_sc[...] + lax.dot_general(p.astype(BF16), v_t, _NT,
                                                            preferred_element_type=F32)
        m_sc[...] = m_new
        return carry

    lax.fori_loop(0, n_kv, body, 0)
    o_ref[...] = _unstack_query_heads(acc_sc[...] / l_sc[...], tq)


def _attention_call(kern, scratch, name, q, k, vt, batch, seq, tq, tk):
    t = q.shape[0]
    n_q = seq // tq
    gw = Q_PER_KV * HEAD_DIM
    return pl.pallas_call(
        functools.partial(kern, tq=tq, tk=tk, n_kv=seq // tk),
        grid=(batch, KV_HEADS, n_q),
        in_specs=[
            pl.BlockSpec((tq, gw), lambda b, h, i: (b * n_q + i, h)),
            pl.BlockSpec((1, seq, HEAD_DIM), lambda b, h, i: (h, b, 0)),
            pl.BlockSpec((1, VT_ROWS, seq), lambda b, h, i: (h, 0, b)),
        ],
        out_specs=pl.BlockSpec((tq, gw), lambda b, h, i: (b * n_q + i, h)),
        out_shape=jax.ShapeDtypeStruct((t, N_HEADS * HEAD_DIM), BF16),
        scratch_shapes=[pltpu.VMEM((Q_PER_KV * tq, HEAD_DIM), BF16)] + scratch,
        compiler_params=_params("parallel", "parallel", "parallel"),
        name=name,
    )(q, k, vt)


def _attention(q, k, vt, logit_bound, batch, seq, tq, tk, tk_online):
    m = Q_PER_KV * tq

    def unshifted(q, k, vt):
        return _attention_call(_attn_unshifted_kernel, [pltpu.VMEM((VT_ROWS, m), F32)],
                               "attention", q, k, vt, batch, seq, tq, tk)

    def online(q, k, vt):
        scratch = [pltpu.VMEM((m, 1), F32), pltpu.VMEM((m, 1), F32), pltpu.VMEM((m, HEAD_DIM), F32)]
        return _attention_call(_attn_online_kernel, scratch, "attention_online",
                               q, k, vt, batch, seq, tq, tk_online)

    return lax.cond(logit_bound <= MAX_UNSHIFTED_LOGIT, unshifted, online, q, k, vt)


def _conv_kernel(prev_ref, main_ref, next_ref, w_ref, b_ref, g_ref, beta_ref, o_ref, ext_sc, shift_sc, y_sc,
                 *, tc, n_blocks):
    i = pl.program_id(1)
    ch = main_ref.shape[1]
    prev = prev_ref[...].astype(F32)
    nxt = next_ref[...].astype(F32)
    ext_sc[0:CONV_HALO, :] = jnp.where(i == 0, 0.0, prev)
    ext_sc[CONV_HALO:CONV_HALO + tc, :] = main_ref[...].astype(F32)
    ext_sc[CONV_HALO + tc:, :] = jnp.where(i == n_blocks - 1, 0.0, nxt)
    rows = shift_sc.shape[1]
    for b in range(SUBLANES):
        shift_sc[b] = ext_sc[b:b + rows, :]
    base = CONV_HALO - CONV_WIDTH // 2
    for r0 in range(0, tc, CONV_ROWS):
        for c in range(ch // LANES):
            cols = slice(c * LANES, (c + 1) * LANES)
            acc = jnp.zeros((CONV_ROWS, LANES), F32) + b_ref[:, cols]
            for kk in range(CONV_WIDTH):
                a, b = divmod(base + kk, SUBLANES)
                row0 = a * SUBLANES + r0
                acc = acc + shift_sc[b, row0:row0 + CONV_ROWS, cols] * w_ref[kk:kk + 1, cols]
            y_sc[r0:r0 + CONV_ROWS, cols] = acc
    y = y_sc[...]
    mu = jnp.mean(y, axis=-1, keepdims=True)
    yc = y - mu
    var = jnp.mean(yc * yc, axis=-1, keepdims=True)
    z = yc * lax.rsqrt(var + EPS) * g_ref[...] + beta_ref[...]
    o_ref[...] = (z * _sigmoid(z)).astype(BF16)


def _conv_module(u, conv_w, conv_b, ln_g, ln_b, batch, seq, tc):
    t, ch = u.shape
    n_blocks = seq // tc
    hb = tc // CONV_HALO
    n_halo = seq // CONV_HALO
    kern = functools.partial(_conv_kernel, tc=tc, n_blocks=n_blocks)
    return pl.pallas_call(
        kern,
        grid=(batch, n_blocks),
        in_specs=[
            pl.BlockSpec((CONV_HALO, ch), lambda b, i: (b * n_halo + jnp.maximum(i * hb - 1, 0), 0)),
            pl.BlockSpec((tc, ch), lambda b, i: (b * n_blocks + i, 0)),
            pl.BlockSpec((CONV_HALO, ch), lambda b, i: (b * n_halo + jnp.minimum((i + 1) * hb, n_halo - 1), 0)),
            pl.BlockSpec((CONV_WIDTH, ch), lambda b, i: (0, 0)),
            pl.BlockSpec((1, ch), lambda b, i: (0, 0)),
            pl.BlockSpec((1, ch), lambda b, i: (0, 0)),
            pl.BlockSpec((1, ch), lambda b, i: (0, 0)),
        ],
        out_specs=pl.BlockSpec((tc, ch), lambda b, i: (b * n_blocks + i, 0)),
        out_shape=jax.ShapeDtypeStruct((t, ch), BF16),
        scratch_shapes=[
            pltpu.VMEM((tc + 2 * CONV_HALO, ch), F32),
            pltpu.VMEM((SUBLANES, tc + 2 * CONV_HALO - SUBLANES, ch), F32),
            pltpu.VMEM((tc, ch), F32),
        ],
        compiler_params=_params("parallel", "parallel"),
        name="conv_module",
    )(u, u, u, conv_w, conv_b, ln_g, ln_b)


def _outproj_kernel(h_ref, a_ref, c_ref, wa_ref, wc_ref, o_ref):
    o_ref[...] = h_ref[...] + _dot(a_ref[...], wa_ref[...]) + _dot(c_ref[...], wc_ref[...])


def _out_proj(h, a, c, w_out, layer, tm):
    t, d = h.shape
    assert a.shape[1] == c.shape[1] and a.shape[1] + c.shape[1] == w_out.shape[1]
    return pl.pallas_call(
        _outproj_kernel,
        grid=(t // tm,),
        in_specs=[
            pl.BlockSpec((tm, d), lambda i: (i, 0)),
            pl.BlockSpec((tm, a.shape[1]), lambda i: (i, 0)),
            pl.BlockSpec((tm, c.shape[1]), lambda i: (i, 0)),
            pl.BlockSpec((None, a.shape[1], d), lambda i: (layer, 0, 0)),
            pl.BlockSpec((None, c.shape[1], d), lambda i: (layer, 1, 0)),
        ],
        out_specs=pl.BlockSpec((tm, d), lambda i: (i, 0)),
        out_shape=jax.ShapeDtypeStruct((t, d), F32),
        compiler_params=_params("parallel"),
        name="out_proj",
    )(h, a, c, w_out, w_out)


def _ffn_kernel(h_ref, g_ref, wg_ref, wu_ref, wd_ref, o_ref, xn_sc, acc_sc):
    f = pl.program_id(1)

    @pl.when(f == 0)
    def _():
        xn_sc[...] = _rms(h_ref[...], g_ref[...]).astype(BF16)
        acc_sc[...] = jnp.zeros(acc_sc.shape, F32)

    xn = xn_sc[...]
    a = _dot(xn, wg_ref[...])
    b = _dot(xn, wu_ref[...])
    mid = (a * _sigmoid(a) * b).astype(BF16)
    acc_sc[...] += _dot(mid, wd_ref[...])

    @pl.when(f == pl.num_programs(1) - 1)
    def _():
        o_ref[...] = h_ref[...] + acc_sc[...]


def _ffn(h, g, wg, wu, wd, layer, tm, tf):
    t, d = h.shape
    f_dim = wg.shape[-1]
    return pl.pallas_call(
        _ffn_kernel,
        grid=(t // tm, f_dim // tf),
        in_specs=[
            pl.BlockSpec((tm, d), lambda i, f: (i, 0)),
            pl.BlockSpec((1, d), lambda i, f: (0, 0)),
            pl.BlockSpec((None, d, tf), lambda i, f: (layer, 0, f)),
            pl.BlockSpec((None, d, tf), lambda i, f: (layer, 0, f)),
            pl.BlockSpec((None, tf, d), lambda i, f: (layer, f, 0)),
        ],
        out_specs=pl.BlockSpec((tm, d), lambda i, f: (i, 0)),
        out_shape=jax.ShapeDtypeStruct((t, d), F32),
        scratch_shapes=[pltpu.VMEM((tm, d), BF16), pltpu.VMEM((tm, d), F32)],
        compiler_params=_params("parallel", "arbitrary"),
        name="dense_swiglu",
    )(h, g, wg, wu, wd)


def _pool_kernel(prev_ref, main_ref, next_ref, g_ref, w_ref, sc_ref, o_ref, *, tp, seq):
    i = pl.program_id(1)
    g = g_ref[...]
    h_main = main_ref[...]
    hn_main = _rms(h_main, g)
    ext = jnp.concatenate([_rms(prev_ref[...], g), hn_main, _rms(next_ref[...], g)], axis=0).astype(BF16)
    rows = tp + 2 * POOL_HALO
    t_pos = i * tp + lax.broadcasted_iota(jnp.int32, (tp, rows), 0)
    j_pos = i * tp - POOL_HALO + lax.broadcasted_iota(jnp.int32, (tp, rows), 1)
    in_seq = (j_pos >= 0) & (j_pos < seq)
    t_col = i * tp + lax.broadcasted_iota(jnp.int32, (tp, 1), 0)
    gc = w_ref.shape[1]
    for gi, win in enumerate(POOL_WINDOWS):
        left = win // 2
        right = win - 1 - left
        band = (in_seq & (j_pos >= t_pos - left) & (j_pos <= t_pos + right)).astype(BF16)
        cnt = jnp.minimum(t_col + right, seq - 1) - jnp.maximum(t_col - left, 0) + 1
        cols = slice(gi * gc, (gi + 1) * gc)
        win_sum = _dot(band, ext[:, cols])
        y = (win_sum / cnt.astype(F32) - hn_main[:, cols]).astype(BF16)
        o_ref[:, cols] = h_main[:, cols] + _dot(y, w_ref[gi]) * sc_ref[:, cols]


def _pool_mixer(h, g, pool_w, layer, pool_scale, batch, seq, tp):
    t, d = h.shape
    n_blocks = seq // tp
    hb = tp // POOL_HALO
    n_halo = seq // POOL_HALO
    kern = functools.partial(_pool_kernel, tp=tp, seq=seq)
    return pl.pallas_call(
        kern,
        grid=(batch, n_blocks),
        in_specs=[
            pl.BlockSpec((POOL_HALO, d), lambda b, i: (b * n_halo + jnp.maximum(i * hb - 1, 0), 0)),
            pl.BlockSpec((tp, d), lambda b, i: (b * n_blocks + i, 0)),
            pl.BlockSpec((POOL_HALO, d), lambda b, i: (b * n_halo + jnp.minimum((i + 1) * hb, n_halo - 1), 0)),
            pl.BlockSpec((1, d), lambda b, i: (0, 0)),
            pl.BlockSpec((None,) + pool_w.shape[1:], lambda b, i: (layer, 0, 0, 0)),
            pl.BlockSpec((1, d), lambda b, i: (0, 0)),
        ],
        out_specs=pl.BlockSpec((tp, d), lambda b, i: (b * n_blocks + i, 0)),
        out_shape=jax.ShapeDtypeStruct((t, d), F32),
        compiler_params=_params("parallel", "parallel"),
        name="pool_mixer",
    )(h, h, h, g, pool_w, pool_scale)


META_E1, META_E2, META_RANK1, META_RANK2, META_P1, META_P2 = range(6)


def _to_token_tiles(ref, x):
    tm, d = x.shape
    for a in range(d // LANES):
        ref[pl.ds(a, tm, stride=SUBLANES), :] = x[:, a * LANES:(a + 1) * LANES]


def _from_token_tiles(ref, tm, d):
    return [ref[pl.ds(a, tm, stride=SUBLANES), :] for a in range(d // LANES)]


def _router_kernel(h_ref, g_ref, wr_ref, xn_ref, meta_ref, counts_ref):
    @pl.when(pl.program_id(0) == 0)
    def _():
        counts_ref[...] = jnp.zeros(counts_ref.shape, F32)

    xn = _rms(h_ref[...], g_ref[...])
    _to_token_tiles(xn_ref, xn)
    logits = jnp.dot(xn, wr_ref[...], preferred_element_type=F32, precision=lax.Precision.HIGHEST)
    tm = logits.shape[0]
    lane = lax.broadcasted_iota(jnp.int32, logits.shape, 1)
    logits = jnp.where(lane < N_EXPERTS, logits, -jnp.inf)
    v1 = jnp.max(logits, axis=-1, keepdims=True)
    i1 = jnp.min(jnp.where(logits == v1, lane, LANES), axis=-1, keepdims=True)
    rest = jnp.where(lane == i1, -jnp.inf, logits)
    v2 = jnp.max(rest, axis=-1, keepdims=True)
    i2 = jnp.min(jnp.where(rest == v2, lane, LANES), axis=-1, keepdims=True)
    e2 = jnp.exp(v2 - v1)
    p1 = 1.0 / (1.0 + e2)
    p2 = e2 * p1
    chosen = (lane == i1) | (lane == i2)
    earlier = (lax.broadcasted_iota(jnp.int32, (tm, tm), 0) > lax.broadcasted_iota(jnp.int32, (tm, tm), 1))
    before = _dot(earlier.astype(BF16), chosen.astype(BF16)) + counts_ref[...]
    rank1 = jnp.sum(jnp.where(lane == i1, before, 0.0), axis=-1, keepdims=True)
    rank2 = jnp.sum(jnp.where(lane == i2, before, 0.0), axis=-1, keepdims=True)
    counts_ref[...] += jnp.sum(chosen.astype(F32), axis=0, keepdims=True)
    meta = jnp.zeros(logits.shape, F32)
    for col, val in ((META_E1, i1.astype(F32)), (META_E2, i2.astype(F32)), (META_RANK1, rank1),
                     (META_RANK2, rank2), (META_P1, p1), (META_P2, p2)):
        meta = jnp.where(lane == col, val, meta)
    meta_ref[...] = meta


def _router(h, g, wr_pad, tm):
    t, d = h.shape
    return pl.pallas_call(
        _router_kernel,
        grid=(t // tm,),
        in_specs=[
            pl.BlockSpec((tm, d), lambda i: (i, 0)),
            pl.BlockSpec((1, d), lambda i: (0, 0)),
            pl.BlockSpec(wr_pad.shape, lambda i: (0, 0)),
        ],
        out_specs=[
            pl.BlockSpec((tm * SUBLANES, LANES), lambda i: (i, 0)),
            pl.BlockSpec((tm, LANES), lambda i: (i, 0)),
            pl.BlockSpec((1, LANES), lambda i: (0, 0)),
        ],
        out_shape=[
            jax.ShapeDtypeStruct((t * SUBLANES, LANES), F32),
            jax.ShapeDtypeStruct((t, LANES), F32),
            jax.ShapeDtypeStruct((1, LANES), F32),
        ],
        compiler_params=_params("arbitrary"),
        name="router",
    )(h, g, wr_pad)


def _routing_tables(meta, counts, tile_rows, n_tiles):
    cnt = counts[0, :N_EXPERTS].astype(jnp.int32)
    padded = (cnt + tile_rows - 1) // tile_rows * tile_rows
    ends = jnp.cumsum(padded)
    starts = ends - padded
    pos1 = starts[meta[:, META_E1].astype(jnp.int32)] + meta[:, META_RANK1].astype(jnp.int32)
    pos2 = starts[meta[:, META_E2].astype(jnp.int32)] + meta[:, META_RANK2].astype(jnp.int32)
    n_used = ends[-1] // tile_rows
    tile_start = jnp.minimum(jnp.arange(n_tiles, dtype=jnp.int32), n_used - 1) * tile_rows
    tile_expert = jnp.sum((tile_start[:, None] >= ends[None, :]).astype(jnp.int32), axis=-1)
    return pos1, pos2, tile_expert, n_used.reshape(1)


def _token_rows(ref, i, n=1):
    return ref.at[pl.ds(pl.multiple_of(i * SUBLANES, SUBLANES), n * SUBLANES)]


def _dispatch_kernel(pos1_ref, pos2_ref, xn_ref, zeros_hbm, xs_hbm, sem, *, chunk):
    del zeros_hbm
    base = pl.program_id(0) * chunk

    def issue(i, carry):
        src = _token_rows(xn_ref, i)
        pltpu.make_async_copy(src, _token_rows(xs_hbm, pos1_ref[base + i]), sem).start()
        pltpu.make_async_copy(src, _token_rows(xs_hbm, pos2_ref[base + i]), sem).start()
        return carry

    lax.fori_loop(0, chunk, issue, 0, unroll=8)
    for _ in range(TOP_K):
        pltpu.make_async_copy(xn_ref, _token_rows(xs_hbm, 0, chunk), sem).wait()


def _dispatch(pos1, pos2, xn_tiles, n_rows, chunk):
    t = pos1.shape[0]
    zeros = jnp.zeros((n_rows * SUBLANES, LANES), F32)
    return pl.pallas_call(
        functools.partial(_dispatch_kernel, chunk=chunk),
        grid_spec=pltpu.PrefetchScalarGridSpec(
            num_scalar_prefetch=2,
            grid=(t // chunk,),
            in_specs=[pl.BlockSpec((chunk * SUBLANES, LANES), lambda i, p1, p2: (i, 0)),
                      pl.BlockSpec(memory_space=pl.ANY)],
            out_specs=pl.BlockSpec(memory_space=pl.ANY),
            scratch_shapes=[pltpu.SemaphoreType.DMA],
        ),
        out_shape=jax.ShapeDtypeStruct(zeros.shape, F32),
        input_output_aliases={3: 0},
        compiler_params=_params("arbitrary"),
        name="moe_dispatch",
    )(pos1, pos2, xn_tiles, zeros)


def _expert_kernel(tile_expert_ref, n_used_ref, xs_ref, wg_ref, wu_ref, wd_ref, ys_ref, x_sc, acc_sc,
                   *, tm, d):
    del tile_expert_ref
    r = pl.program_id(0)
    f = pl.program_id(1)
    last = pl.num_programs(1) - 1
    used = r < n_used_ref[0]

    @pl.when(used & (f == 0))
    def _():
        x_sc[...] = jnp.concatenate(_from_token_tiles(xs_ref, tm, d), axis=-1).astype(BF16)

    @pl.when(used)
    def _():
        x = x_sc[...]
        a = _dot(x, wg_ref[...])
        b = _dot(x, wu_ref[...])
        y = _dot((a * _sigmoid(a) * b).astype(BF16), wd_ref[...])

        @pl.when(f == 0)
        def _():
            acc_sc[...] = y

        @pl.when((f > 0) & (f < last))
        def _():
            acc_sc[...] += y

        @pl.when(f == last)
        def _():
            _to_token_tiles(ys_ref, acc_sc[...] + y)

    @pl.when(jnp.logical_not(used) & (f == last))
    def _():
        ys_ref[...] = jnp.zeros(ys_ref.shape, F32)


def _experts(tile_expert, n_used, xs, wg, wu, wd, layer, tm, tf):
    _, n_e, d, f_dim = wg.shape
    n_tiles = xs.shape[0] // (tm * SUBLANES)
    assert f_dim // tf >= 2

    def row_tile(r, f, te, nu):
        return (jnp.minimum(r, nu[0] - 1), 0)

    return pl.pallas_call(
        functools.partial(_expert_kernel, tm=tm, d=d),
        grid_spec=pltpu.PrefetchScalarGridSpec(
            num_scalar_prefetch=2,
            grid=(n_tiles, f_dim // tf),
            in_specs=[
                pl.BlockSpec((tm * SUBLANES, LANES), row_tile),
                pl.BlockSpec((None, None, d, tf), lambda r, f, te, nu: (layer, te[r], 0, f)),
                pl.BlockSpec((None, None, d, tf), lambda r, f, te, nu: (layer, te[r], 0, f)),
                pl.BlockSpec((None, None, tf, d), lambda r, f, te, nu: (layer, te[r], f, 0)),
            ],
            out_specs=pl.BlockSpec((tm * SUBLANES, LANES), lambda r, f, te, nu: (r, 0)),
            scratch_shapes=[pltpu.VMEM((tm, d), BF16), pltpu.VMEM((tm, d), F32)],
        ),
        out_shape=jax.ShapeDtypeStruct(xs.shape, F32),
        compiler_params=_params("arbitrary", "arbitrary"),
        name="moe_experts",
    )(tile_expert, n_used, xs, wg, wu, wd)


def _combine_kernel(pos1_ref, pos2_ref, h_ref, meta_ref, ys_hbm, p_ref, g_ref, wgate_ref, wproj_ref,
                    o_ref, y_sc, sems, *, tm, d):
    i = pl.program_id(0)
    slot = i % 2

    def gather(tile, slot):
        base = tile * tm

        def issue(r, carry):
            for k, pos_ref in enumerate((pos1_ref, pos2_ref)):
                pltpu.make_async_copy(_token_rows(ys_hbm, pos_ref[base + r]),
                                      _token_rows(y_sc.at[TOP_K * slot + k], r), sems.at[slot]).start()
            return carry

        lax.fori_loop(0, tm, issue, 0, unroll=8)

    @pl.when(i == 0)
    def _():
        gather(0, 0)

    @pl.when(i + 1 < pl.num_programs(0))
    def _():
        gather(i + 1, 1 - slot)

    for k in range(TOP_K):
        pltpu.make_async_copy(_token_rows(ys_hbm, 0, tm), y_sc.at[TOP_K * slot + k], sems.at[slot]).wait()
    meta = meta_ref[...]
    p1 = meta[:, META_P1:META_P1 + 1]
    p2 = meta[:, META_P2:META_P2 + 1]
    y1 = _from_token_tiles(y_sc.at[TOP_K * slot], tm, d)
    y2 = _from_token_tiles(y_sc.at[TOP_K * slot + 1], tm, d)
    for a in range(d // LANES):
        cols = slice(a * LANES, (a + 1) * LANES)
        o_ref[:, cols] = h_ref[:, cols] + p1 * y1[a] + p2 * y2[a]
    o_ref[...] = _ple_update(o_ref[...], p_ref, g_ref, wgate_ref, wproj_ref)


def _combine(pos1, pos2, h, meta, ys, p, g, w_gate, w_proj, layer, tm):
    t, d = h.shape
    return pl.pallas_call(
        functools.partial(_combine_kernel, tm=tm, d=d),
        grid_spec=pltpu.PrefetchScalarGridSpec(
            num_scalar_prefetch=2,
            grid=(t // tm,),
            in_specs=[
                pl.BlockSpec((tm, d), lambda i, p1, p2: (i, 0)),
                pl.BlockSpec((tm, LANES), lambda i, p1, p2: (i, 0)),
                pl.BlockSpec(memory_space=pl.ANY),
            ] + _ple_specs(p, w_gate, w_proj, layer, tm, d),
            out_specs=pl.BlockSpec((tm, d), lambda i, p1, p2: (i, 0)),
            scratch_shapes=[pltpu.VMEM((2 * TOP_K, tm * SUBLANES, LANES), F32),
                            pltpu.SemaphoreType.DMA((2,))],
        ),
        out_shape=jax.ShapeDtypeStruct((t, d), F32),
        compiler_params=_params("arbitrary"),
        name="moe_combine",
    )(pos1, pos2, h, meta, ys, p, g, w_gate, w_proj)


def _ple_kernel(h_ref, p_ref, g_ref, wgate_ref, wproj_ref, o_ref):
    o_ref[...] = _ple_update(h_ref[...], p_ref, g_ref, wgate_ref, wproj_ref)


def _ple_specs(p, w_gate, w_proj, layer, tm, d):
    def spec(block, index):
        return pl.BlockSpec(block, lambda i, *prefetch: index(i))
    return [
        spec((None, tm, p.shape[-1]), lambda i: (layer, i, 0)),
        spec((1, d), lambda i: (0, 0)),
        spec((None,) + w_gate.shape[1:], lambda i: (layer, 0, 0)),
        spec((None,) + w_proj.shape[1:], lambda i: (layer, 0, 0)),
    ]


def _ple(h, p, g, w_gate, w_proj, layer, tm):
    t, d = h.shape
    return pl.pallas_call(
        _ple_kernel,
        grid=(t // tm,),
        in_specs=[pl.BlockSpec((tm, d), lambda i: (i, 0))] + _ple_specs(p, w_gate, w_proj, layer, tm, d),
        out_specs=pl.BlockSpec((tm, d), lambda i: (i, 0)),
        out_shape=jax.ShapeDtypeStruct((t, d), F32),
        compiler_params=_params("parallel"),
        name="per_layer_input",
    )(h, p, g, w_gate, w_proj)


def _rope_tables(seq):
    rows = seq // GRID_W
    r = jnp.broadcast_to(jnp.arange(rows, dtype=F32)[:, None], (rows, GRID_W)).reshape(seq)
    c = jnp.broadcast_to(jnp.arange(GRID_W, dtype=F32)[None, :], (rows, GRID_W)).reshape(seq)
    inv = ROPE_THETA ** (-jnp.arange(0, AXIS_DIM, 2, dtype=F32) / AXIS_DIM)
    ang = jnp.concatenate([r[:, None] * inv, c[:, None] * inv], axis=-1)
    cos, sin = jnp.cos(ang), jnp.sin(ang)
    reps = LANES // HEAD_DIM
    return (jnp.tile(jnp.concatenate([cos, cos], axis=-1), (1, reps)),
            jnp.tile(jnp.concatenate([-sin, sin], axis=-1), (1, reps)))


def _tile(n, want):
    t = min(n, want)
    assert n % t == 0, (n, t)
    return t


def kernel(x, p, norm_mix, norm_ffn, w_in, q_norm, k_norm, conv_w, conv_b, conv_ln_g, conv_ln_b, w_out,
           ffn_wg, ffn_wu, ffn_wd, pool_w, pool_scale, router_w, moe_wg, moe_wu, moe_wd, ple_norm,
           ple_gate_w, ple_proj):
    batch, seq, d = x.shape
    depth = p.shape[0]
    t = batch * seq
    q_dim = N_HEADS * HEAD_DIM
    assert seq % GRID_W == 0 and d % LANES == 0

    tm = _tile(seq, 512)
    tm_ffn = _tile(t, 1024)
    tq = _tile(seq, 256)
    tk = _tile(seq, 8192)
    tk_online = _tile(seq, 512)
    tc = _tile(seq, 256)
    tp = _tile(seq, 256)
    tf_ffn = ffn_wg.shape[2] // 2
    tf_moe = moe_wg.shape[3] // 2
    tm_moe = _tile(t, 1024)
    n_moe_tiles = TOP_K * t // tm_moe + N_EXPERTS
    dispatch_chunk = _tile(t, 2048)

    cos_t, sin_t = _rope_tables(seq)
    row = lambda v: v.reshape(1, -1)
    tile_heads = lambda v: jnp.tile(v, LANES // HEAD_DIM).reshape(1, LANES)

    bf = lambda w: w.astype(BF16)
    w_in_b, w_out_b = bf(w_in), bf(w_out)
    ffn_wg_b, ffn_wu_b, ffn_wd_b = bf(ffn_wg), bf(ffn_wu), bf(ffn_wd)
    pool_w_b = bf(pool_w)
    moe_wg_b, moe_wu_b, moe_wd_b = bf(moe_wg), bf(moe_wu), bf(moe_wd)
    ple_gate_b, ple_proj_b = bf(ple_gate_w), bf(ple_proj)
    p_rows = p.reshape(depth, t, -1)

    h = x.reshape(t, d)
    for i in range(depth):
        j = i // 2
        ple_args = (p_rows, row(ple_norm[i]), ple_gate_b, ple_proj_b, i)
        if i % 2 == 0:
            q, k, vt, u = _in_proj(h, row(norm_mix[i]), w_in_b, j, tile_heads(q_norm[j]),
                                   tile_heads(k_norm[j]), cos_t, sin_t, seq, tm)
            logit_bound = (HEAD_DIM ** 0.5 * LOG2E) * jnp.max(jnp.abs(q_norm[j])) * jnp.max(jnp.abs(k_norm[j]))
            a = _attention(q, k, vt, logit_bound, batch, seq, tq, tk, tk_online)
            c = _conv_module(u, conv_w[j], row(conv_b[j]), row(conv_ln_g[j]), row(conv_ln_b[j]),
                             batch, seq, tc)
            h = _out_proj(h, a, c, w_out_b, j, tm)
            h = _ffn(h, row(norm_ffn[i]), ffn_wg_b, ffn_wu_b, ffn_wd_b, j, tm_ffn, tf_ffn)
            h = _ple(h, *ple_args, tm)
        else:
            h = _pool_mixer(h, row(norm_mix[i]), pool_w_b, j, row(pool_scale[j]), batch, seq, tp)
            wr_pad = jnp.pad(router_w[j], ((0, 0), (0, LANES - N_EXPERTS)))
            xn_tiles, meta, counts = _router(h, row(norm_ffn[i]), wr_pad, tm)
            pos1, pos2, tile_expert, n_used = _routing_tables(meta, counts, tm_moe, n_moe_tiles)
            xs = _dispatch(pos1, pos2, xn_tiles, n_moe_tiles * tm_moe, dispatch_chunk)
            ys = _experts(tile_expert, n_used, xs, moe_wg_b, moe_wu_b, moe_wd_b, j, tm_moe, tf_moe)
            h = _combine(pos1, pos2, h, meta, ys, *ple_args, tm)
    return h.reshape(batch, seq, d)
```

```python
import functools

import jax
import jax.numpy as jnp
from jax import lax
from jax.experimental import pallas as pl
from jax.experimental.pallas import tpu as pltpu

F32 = jnp.float32
BF16 = jnp.bfloat16

GRID_W = 64
N_HEADS = 8
KV_HEADS = 2
HEAD_DIM = 64
Q_PER_KV = N_HEADS // KV_HEADS
AXIS_DIM = HEAD_DIM // 2
ROPE_THETA = 10000.0
CONV_WIDTH = 31
POOL_WINDOWS = (2, 4, 8, 16)
N_EXPERTS = 8
TOP_K = 2
EPS = 1e-6

LANES = 128
SUBLANES = 8
VMEM_LIMIT = 56 * 1024 * 1024
BF16_SUBLANES = 16
VT_ROWS = HEAD_DIM + BF16_SUBLANES
LOG2E = 1.4426950408889634
Q_SCALE = HEAD_DIM ** -0.5 * LOG2E
MAX_UNSHIFTED_LOGIT = 80.0
CONV_HALO = 16
CONV_ROWS = 64
POOL_HALO = 8


def _params(*sem):
    return pltpu.CompilerParams(dimension_semantics=sem, vmem_limit_bytes=VMEM_LIMIT)


def _rms(x, g):
    return x * lax.rsqrt(jnp.mean(x * x, axis=-1, keepdims=True) + EPS) * g


def _sigmoid(x):
    return 1.0 / (1.0 + jnp.exp(-x))


def _dot(a, b):
    return jnp.dot(a, b, preferred_element_type=F32)


def _ple_update(h, p_ref, g_ref, wgate_ref, wproj_ref):
    gate = _sigmoid(_dot(_rms(h, g_ref[...]).astype(BF16), wgate_ref[...]))
    return h + gate * _dot(p_ref[...].astype(BF16), wproj_ref[...])


def _inproj_kernel(h_ref, g_ref, w_ref, qg_ref, kg_ref, cos_ref, sin_ref,
                   q_ref, k_ref, vt_ref, u_ref, *, q_dim, kv_dim, conv_ch):
    xn = _rms(h_ref[...], g_ref[...]).astype(BF16)
    proj = _dot(xn, w_ref[...])
    tm = proj.shape[0]
    cos = cos_ref[...]
    sin = sin_ref[...]
    lane = lax.broadcasted_iota(jnp.int32, (tm, LANES), 1)
    head0 = lane < HEAD_DIM
    first_half = (lane % HEAD_DIM) < (HEAD_DIM // 2)

    def norm_rope(x, g, scale):
        sq = x * x
        s0 = jnp.sum(jnp.where(head0, sq, 0.0), axis=-1, keepdims=True)
        s1 = jnp.sum(jnp.where(head0, 0.0, sq), axis=-1, keepdims=True)
        ms = jnp.where(head0, s0, s1) * (1.0 / HEAD_DIM)
        y = x * lax.rsqrt(ms + EPS) * g
        partner = jnp.where(first_half,
                            pltpu.roll(y, LANES - HEAD_DIM // 2, 1),
                            pltpu.roll(y, HEAD_DIM // 2, 1))
        return (y * cos + partner * sin) * scale

    for c in range(q_dim // LANES):
        x = proj[:, c * LANES:(c + 1) * LANES]
        q_ref[:, c * LANES:(c + 1) * LANES] = norm_rope(x, qg_ref[...], Q_SCALE).astype(BF16)
    sub = lax.broadcasted_iota(jnp.int32, (VT_ROWS - HEAD_DIM, tm), 0)
    ones_rows = jnp.where(sub == 0, 1.0, 0.0).astype(BF16)
    for c in range(kv_dim // LANES):
        x = proj[:, q_dim + c * LANES:q_dim + (c + 1) * LANES]
        kk = norm_rope(x, kg_ref[...], 1.0).astype(BF16)
        vv_t = proj[:, q_dim + kv_dim + c * LANES:q_dim + kv_dim + (c + 1) * LANES].T
        for j in range(LANES // HEAD_DIM):
            head = c * (LANES // HEAD_DIM) + j
            k_ref[head] = kk[:, j * HEAD_DIM:(j + 1) * HEAD_DIM]
            vt_ref[head, 0:HEAD_DIM, :] = vv_t[j * HEAD_DIM:(j + 1) * HEAD_DIM, :].astype(BF16)
            vt_ref[head, HEAD_DIM:VT_ROWS, :] = ones_rows
    u0 = q_dim + 2 * kv_dim
    u_ref[...] = (proj[:, u0:u0 + conv_ch] * _sigmoid(proj[:, u0 + conv_ch:u0 + 2 * conv_ch])).astype(BF16)


def _in_proj(h, g, w_in, layer, qg, kg, cos_t, sin_t, seq, tm):
    t, d = h.shape
    w_in_dim = w_in.shape[-1]
    q_dim = N_HEADS * HEAD_DIM
    kv_dim = KV_HEADS * HEAD_DIM
    conv_ch = (w_in_dim - q_dim - 2 * kv_dim) // 2
    n_seq_blocks = seq // tm
    kern = functools.partial(_inproj_kernel, q_dim=q_dim, kv_dim=kv_dim, conv_ch=conv_ch)
    return pl.pallas_call(
        kern,
        grid=(t // tm,),
        in_specs=[
            pl.BlockSpec((tm, d), lambda i: (i, 0)),
            pl.BlockSpec((1, d), lambda i: (0, 0)),
            pl.BlockSpec((None, d, w_in_dim), lambda i: (layer, 0, 0)),
            pl.BlockSpec((1, LANES), lambda i: (0, 0)),
            pl.BlockSpec((1, LANES), lambda i: (0, 0)),
            pl.BlockSpec((tm, LANES), lambda i: (i % n_seq_blocks, 0)),
            pl.BlockSpec((tm, LANES), lambda i: (i % n_seq_blocks, 0)),
        ],
        out_specs=[
            pl.BlockSpec((tm, q_dim), lambda i: (i, 0)),
            pl.BlockSpec((KV_HEADS, tm, HEAD_DIM), lambda i: (0, i, 0)),
            pl.BlockSpec((KV_HEADS, VT_ROWS, tm), lambda i: (0, 0, i)),
            pl.BlockSpec((tm, conv_ch), lambda i: (i, 0)),
        ],
        out_shape=[
            jax.ShapeDtypeStruct((t, q_dim), BF16),
            jax.ShapeDtypeStruct((KV_HEADS, t, HEAD_DIM), BF16),
            jax.ShapeDtypeStruct((KV_HEADS, VT_ROWS, t), BF16),
            jax.ShapeDtypeStruct((t, conv_ch), BF16),
        ],
        compiler_params=_params("parallel"),
        name="in_proj",
    )(h, g, w_in, qg, kg, cos_t, sin_t)


def _stack_query_heads(q_ref, q_sc, tq):
    for g in range(Q_PER_KV):
        q_sc[g * tq:(g + 1) * tq, :] = q_ref[:, g * HEAD_DIM:(g + 1) * HEAD_DIM]


def _unstack_query_heads(out, tq):
    return jnp.concatenate([out[g * tq:(g + 1) * tq, :] for g in range(Q_PER_KV)], axis=-1).astype(BF16)


_NT = (((1,), (1,)), ((), ()))


def _attn_unshifted_kernel(q_ref, k_ref, vt_ref, o_ref, q_sc, acc_sc, *, tq, tk, n_kv):
    _stack_query_heads(q_ref, q_sc, tq)
    acc_sc[...] = jnp.zeros(acc_sc.shape, F32)

    def body(j, carry):
        kv0 = pl.multiple_of(j * tk, tk)
        s_t = lax.dot_general(k_ref[0, pl.ds(kv0, tk), :], q_sc[...], _NT,
                              preferred_element_type=F32)
        p_t = jnp.exp2(s_t).astype(BF16)
        acc_sc[...] += _dot(vt_ref[0, :, pl.ds(kv0, tk)], p_t)
        return carry

    lax.fori_loop(0, n_kv, body, 0)
    acc = acc_sc[...]
    out_t = acc[:HEAD_DIM, :] / acc[HEAD_DIM:HEAD_DIM + 1, :]
    o_ref[...] = _unstack_query_heads(out_t.T, tq)


def _attn_online_kernel(q_ref, k_ref, vt_ref, o_ref, q_sc, m_sc, l_sc, acc_sc, *, tq, tk, n_kv):
    _stack_query_heads(q_ref, q_sc, tq)
    m_sc[...] = jnp.full(m_sc.shape, -jnp.inf, F32)
    l_sc[...] = jnp.zeros(l_sc.shape, F32)
    acc_sc[...] = jnp.zeros(acc_sc.shape, F32)

    def body(j, carry):
        kv0 = pl.multiple_of(j * tk, tk)
        s = lax.dot_general(q_sc[...], k_ref[0, pl.ds(kv0, tk), :], _NT, preferred_element_type=F32)
        m_prev = m_sc[...]
        m_new = jnp.maximum(m_prev, jnp.max(s, axis=-1, keepdims=True))
        alpha = jnp.exp2(m_prev - m_new)
        p = jnp.exp2(s - m_new)
        l_sc[...] = alpha * l_sc[...] + jnp.sum(p, axis=-1, keepdims=True)
        v_t = vt_ref[0, 0:HEAD_DIM, pl.ds(kv0, tk)]
        acc_sc[...] = alpha * acc_sc[...] + lax.dot_general(p.astype(BF16), v_t, _NT,
                                                            preferred_element_type=F32)
        m_sc[...] = m_new
        return carry

    lax.fori_loop(0, n_kv, body, 0)
    o_ref[...] = _unstack_query_heads(acc_sc[...] / l_sc[...], tq)


def _attention_call(kern, scratch, name, q, k, vt, batch, seq, tq, tk):
    t = q.shape[0]
    n_q = seq // tq
    gw = Q_PER_KV * HEAD_DIM
    return pl.pallas_call(
        functools.partial(kern, tq=tq, tk=tk, n_kv=seq // tk),
        grid=(batch, KV_HEADS, n_q),
        in_specs=[
            pl.BlockSpec((tq, gw), lambda b, h, i: (b * n_q + i, h)),
            pl.BlockSpec((1, seq, HEAD_DIM), lambda b, h, i: (h, b, 0)),
            pl.BlockSpec((1, VT_ROWS, seq), lambda b, h, i: (h, 0, b)),
        ],
        out_specs=pl.BlockSpec((tq, gw), lambda b, h, i: (b * n_q + i, h)),
        out_shape=jax.ShapeDtypeStruct((t, N_HEADS * HEAD_DIM), BF16),
        scratch_shapes=[pltpu.VMEM((Q_PER_KV * tq, HEAD_DIM), BF16)] + scratch,
        compiler_params=_params("parallel", "parallel", "parallel"),
        name=name,
    )(q, k, vt)


def _attention(q, k, vt, logit_bound, batch, seq, tq, tk, tk_online):
    m = Q_PER_KV * tq

    def unshifted(q, k, vt):
        return _attention_call(_attn_unshifted_kernel, [pltpu.VMEM((VT_ROWS, m), F32)],
                               "attention", q, k, vt, batch, seq, tq, tk)

    def online(q, k, vt):
        scratch = [pltpu.VMEM((m, 1), F32), pltpu.VMEM((m, 1), F32), pltpu.VMEM((m, HEAD_DIM), F32)]
        return _attention_call(_attn_online_kernel, scratch, "attention_online",
                               q, k, vt, batch, seq, tq, tk_online)

    return lax.cond(logit_bound <= MAX_UNSHIFTED_LOGIT, unshifted, online, q, k, vt)


def _conv_kernel(prev_ref, main_ref, next_ref, w_ref, b_ref, g_ref, beta_ref, o_ref, ext_sc, shift_sc, y_sc,
                 *, tc, n_blocks):
    i = pl.program_id(1)
    ch = main_ref.shape[1]
    prev = prev_ref[...].astype(F32)
    nxt = next_ref[...].astype(F32)
    ext_sc[0:CONV_HALO, :] = jnp.where(i == 0, 0.0, prev)
    ext_sc[CONV_HALO:CONV_HALO + tc, :] = main_ref[...].astype(F32)
    ext_sc[CONV_HALO + tc:, :] = jnp.where(i == n_blocks - 1, 0.0, nxt)
    rows = shift_sc.shape[1]
    for b in range(SUBLANES):
        shift_sc[b] = ext_sc[b:b + rows, :]
    base = CONV_HALO - CONV_WIDTH // 2
    for r0 in range(0, tc, CONV_ROWS):
        for c in range(ch // LANES):
            cols = slice(c * LANES, (c + 1) * LANES)
            acc = jnp.zeros((CONV_ROWS, LANES), F32) + b_ref[:, cols]
            for kk in range(CONV_WIDTH):
                a, b = divmod(base + kk, SUBLANES)
                row0 = a * SUBLANES + r0
                acc = acc + shift_sc[b, row0:row0 + CONV_ROWS, cols] * w_ref[kk:kk + 1, cols]
            y_sc[r0:r0 + CONV_ROWS, cols] = acc
    y = y_sc[...]
    mu = jnp.mean(y, axis=-1, keepdims=True)
    yc = y - mu
    var = jnp.mean(yc * yc, axis=-1, keepdims=True)
    z = yc * lax.rsqrt(var + EPS) * g_ref[...] + beta_ref[...]
    o_ref[...] = (z * _sigmoid(z)).astype(BF16)


def _conv_module(u, conv_w, conv_b, ln_g, ln_b, batch, seq, tc):
    t, ch = u.shape
    n_blocks = seq // tc
    hb = tc // CONV_HALO
    n_halo = seq // CONV_HALO
    kern = functools.partial(_conv_kernel, tc=tc, n_blocks=n_blocks)
    return pl.pallas_call(
        kern,
        grid=(batch, n_blocks),
        in_specs=[
            pl.BlockSpec((CONV_HALO, ch), lambda b, i: (b * n_halo + jnp.maximum(i * hb - 1, 0), 0)),
            pl.BlockSpec((tc, ch), lambda b, i: (b * n_blocks + i, 0)),
            pl.BlockSpec((CONV_HALO, ch), lambda b, i: (b * n_halo + jnp.minimum((i + 1) * hb, n_halo - 1), 0)),
            pl.BlockSpec((CONV_WIDTH, ch), lambda b, i: (0, 0)),
            pl.BlockSpec((1, ch), lambda b, i: (0, 0)),
            pl.BlockSpec((1, ch), lambda b, i: (0, 0)),
            pl.BlockSpec((1, ch), lambda b, i: (0, 0)),
        ],
        out_specs=pl.BlockSpec((tc, ch), lambda b, i: (b * n_blocks + i, 0)),
        out_shape=jax.ShapeDtypeStruct((t, ch), BF16),
        scratch_shapes=[
            pltpu.VMEM((tc + 2 * CONV_HALO, ch), F32),
            pltpu.VMEM((SUBLANES, tc + 2 * CONV_HALO - SUBLANES, ch), F32),
            pltpu.VMEM((tc, ch), F32),
        ],
        compiler_params=_params("parallel", "parallel"),
        name="conv_module",
    )(u, u, u, conv_w, conv_b, ln_g, ln_b)


def _resident(block, index_map):
    return pl.BlockSpec(block, index_map, pipeline_mode=pl.Buffered(1))


def _even_tail_kernel(h_ref, a_ref, c_ref, wa_ref, wc_ref, g_ref, wg_ref, wu_ref, wd_ref,
                      p_ref, pg_ref, wgate_ref, wproj_ref, o_ref):
    h = h_ref[...] + _dot(a_ref[...], wa_ref[...]) + _dot(c_ref[...], wc_ref[...])
    xn = _rms(h, g_ref[...]).astype(BF16)
    a = _dot(xn, wg_ref[...])
    b = _dot(xn, wu_ref[...])
    h = h + _dot((a * _sigmoid(a) * b).astype(BF16), wd_ref[...])
    o_ref[...] = _ple_update(h, p_ref, pg_ref, wgate_ref, wproj_ref)


def _even_tail(h, a, c, w_out, g, wg, wu, wd, p, pg, w_gate, w_proj, layer, ple_layer, tm):
    t, d = h.shape
    f_dim = wg.shape[-1]
    assert a.shape[1] == c.shape[1] and a.shape[1] + c.shape[1] == w_out.shape[1]
    return pl.pallas_call(
        _even_tail_kernel,
        grid=(t // tm,),
        in_specs=[
            pl.BlockSpec((tm, d), lambda i: (i, 0)),
            pl.BlockSpec((tm, a.shape[1]), lambda i: (i, 0)),
            pl.BlockSpec((tm, c.shape[1]), lambda i: (i, 0)),
            _resident((None, a.shape[1], d), lambda i: (layer, 0, 0)),
            _resident((None, c.shape[1], d), lambda i: (layer, 1, 0)),
            pl.BlockSpec((1, d), lambda i: (0, 0)),
            _resident((None, d, f_dim), lambda i: (layer, 0, 0)),
            _resident((None, d, f_dim), lambda i: (layer, 0, 0)),
            _resident((None, f_dim, d), lambda i: (layer, 0, 0)),
        ] + _ple_specs(p, w_gate, w_proj, ple_layer, tm, d, resident_weights=True),
        out_specs=pl.BlockSpec((tm, d), lambda i: (i, 0)),
        out_shape=jax.ShapeDtypeStruct((t, d), F32),
        compiler_params=_params("parallel"),
        name="out_proj_swiglu_ple",
    )(h, a, c, w_out, w_out, g, wg, wu, wd, p, pg, w_gate, w_proj)


def _pool_kernel(prev_ref, main_ref, next_ref, g_ref, w_ref, sc_ref, o_ref, *, tp, seq):
    i = pl.program_id(1)
    g = g_ref[...]
    h_main = main_ref[...]
    hn_main = _rms(h_main, g)
    ext = jnp.concatenate([_rms(prev_ref[...], g), hn_main, _rms(next_ref[...], g)], axis=0).astype(BF16)
    rows = tp + 2 * POOL_HALO
    t_pos = i * tp + lax.broadcasted_iota(jnp.int32, (tp, rows), 0)
    j_pos = i * tp - POOL_HALO + lax.broadcasted_iota(jnp.int32, (tp, rows), 1)
    in_seq = (j_pos >= 0) & (j_pos < seq)
    t_col = i * tp + lax.broadcasted_iota(jnp.int32, (tp, 1), 0)
    gc = w_ref.shape[1]
    for gi, win in enumerate(POOL_WINDOWS):
        left = win // 2
        right = win - 1 - left
        band = (in_seq & (j_pos >= t_pos - left) & (j_pos <= t_pos + right)).astype(BF16)
        cnt = jnp.minimum(t_col + right, seq - 1) - jnp.maximum(t_col - left, 0) + 1
        cols = slice(gi * gc, (gi + 1) * gc)
        win_sum = _dot(band, ext[:, cols])
        y = (win_sum / cnt.astype(F32) - hn_main[:, cols]).astype(BF16)
        o_ref[:, cols] = h_main[:, cols] + _dot(y, w_ref[gi]) * sc_ref[:, cols]


def _pool_mixer(h, g, pool_w, layer, pool_scale, batch, seq, tp):
    t, d = h.shape
    n_blocks = seq // tp
    hb = tp // POOL_HALO
    n_halo = seq // POOL_HALO
    kern = functools.partial(_pool_kernel, tp=tp, seq=seq)
    return pl.pallas_call(
        kern,
        grid=(batch, n_blocks),
        in_specs=[
            pl.BlockSpec((POOL_HALO, d), lambda b, i: (b * n_halo + jnp.maximum(i * hb - 1, 0), 0)),
            pl.BlockSpec((tp, d), lambda b, i: (b * n_blocks + i, 0)),
            pl.BlockSpec((POOL_HALO, d), lambda b, i: (b * n_halo + jnp.minimum((i + 1) * hb, n_halo - 1), 0)),
            pl.BlockSpec((1, d), lambda b, i: (0, 0)),
            pl.BlockSpec((None,) + pool_w.shape[1:], lambda b, i: (layer, 0, 0, 0)),
            pl.BlockSpec((1, d), lambda b, i: (0, 0)),
        ],
        out_specs=pl.BlockSpec((tp, d), lambda b, i: (b * n_blocks + i, 0)),
        out_shape=jax.ShapeDtypeStruct((t, d), F32),
        compiler_params=_params("parallel", "parallel"),
        name="pool_mixer",
    )(h, h, h, g, pool_w, pool_scale)


META_E1, META_E2, META_RANK1, META_RANK2, META_P1, META_P2 = range(6)


def _to_token_tiles(ref, x):
    tm, d = x.shape
    for a in range(d // LANES):
        ref[pl.ds(a, tm, stride=SUBLANES), :] = x[:, a * LANES:(a + 1) * LANES]


def _from_token_tiles(ref, tm, d):
    return [ref[pl.ds(a, tm, stride=SUBLANES), :] for a in range(d // LANES)]


def _router_kernel(h_ref, g_ref, wr_ref, xn_ref, meta_ref, counts_ref):
    @pl.when(pl.program_id(0) == 0)
    def _():
        counts_ref[...] = jnp.zeros(counts_ref.shape, F32)

    xn = _rms(h_ref[...], g_ref[...])
    _to_token_tiles(xn_ref, xn)
    logits = jnp.dot(xn, wr_ref[...], preferred_element_type=F32, precision=lax.Precision.HIGHEST)
    tm = logits.shape[0]
    lane = lax.broadcasted_iota(jnp.int32, logits.shape, 1)
    logits = jnp.where(lane < N_EXPERTS, logits, -jnp.inf)
    v1 = jnp.max(logits, axis=-1, keepdims=True)
    i1 = jnp.min(jnp.where(logits == v1, lane, LANES), axis=-1, keepdims=True)
    rest = jnp.where(lane == i1, -jnp.inf, logits)
    v2 = jnp.max(rest, axis=-1, keepdims=True)
    i2 = jnp.min(jnp.where(rest == v2, lane, LANES), axis=-1, keepdims=True)
    e2 = jnp.exp(v2 - v1)
    p1 = 1.0 / (1.0 + e2)
    p2 = e2 * p1
    chosen = (lane == i1) | (lane == i2)
    earlier = (lax.broadcasted_iota(jnp.int32, (tm, tm), 0) > lax.broadcasted_iota(jnp.int32, (tm, tm), 1))
    before = _dot(earlier.astype(BF16), chosen.astype(BF16)) + counts_ref[...]
    rank1 = jnp.sum(jnp.where(lane == i1, before, 0.0), axis=-1, keepdims=True)
    rank2 = jnp.sum(jnp.where(lane == i2, before, 0.0), axis=-1, keepdims=True)
    counts_ref[...] += jnp.sum(chosen.astype(F32), axis=0, keepdims=True)
    meta = jnp.zeros(logits.shape, F32)
    for col, val in ((META_E1, i1.astype(F32)), (META_E2, i2.astype(F32)), (META_RANK1, rank1),
                     (META_RANK2, rank2), (META_P1, p1), (META_P2, p2)):
        meta = jnp.where(lane == col, val, meta)
    meta_ref[...] = meta


def _router(h, g, wr_pad, tm):
    t, d = h.shape
    return pl.pallas_call(
        _router_kernel,
        grid=(t // tm,),
        in_specs=[
            pl.BlockSpec((tm, d), lambda i: (i, 0)),
            pl.BlockSpec((1, d), lambda i: (0, 0)),
            pl.BlockSpec(wr_pad.shape, lambda i: (0, 0)),
        ],
        out_specs=[
            pl.BlockSpec((tm * SUBLANES, LANES), lambda i: (i, 0)),
            pl.BlockSpec((tm, LANES), lambda i: (i, 0)),
            pl.BlockSpec((1, LANES), lambda i: (0, 0)),
        ],
        out_shape=[
            jax.ShapeDtypeStruct((t * SUBLANES, LANES), F32),
            jax.ShapeDtypeStruct((t, LANES), F32),
            jax.ShapeDtypeStruct((1, LANES), F32),
        ],
        compiler_params=_params("arbitrary"),
        name="router",
    )(h, g, wr_pad)


def _routing_tables(meta, counts, tile_rows, n_tiles):
    cnt = counts[0, :N_EXPERTS].astype(jnp.int32)
    padded = (cnt + tile_rows - 1) // tile_rows * tile_rows
    ends = jnp.cumsum(padded)
    starts = ends - padded
    pos1 = starts[meta[:, META_E1].astype(jnp.int32)] + meta[:, META_RANK1].astype(jnp.int32)
    pos2 = starts[meta[:, META_E2].astype(jnp.int32)] + meta[:, META_RANK2].astype(jnp.int32)
    n_used = ends[-1] // tile_rows
    tile_start = jnp.minimum(jnp.arange(n_tiles, dtype=jnp.int32), n_used - 1) * tile_rows
    tile_expert = jnp.sum((tile_start[:, None] >= ends[None, :]).astype(jnp.int32), axis=-1)
    return pos1, pos2, tile_expert, n_used.reshape(1)


def _token_rows(ref, i, n=1):
    return ref.at[pl.ds(pl.multiple_of(i * SUBLANES, SUBLANES), n * SUBLANES)]


def _dispatch_kernel(pos1_ref, pos2_ref, xn_ref, zeros_hbm, xs_hbm, sem, *, chunk):
    del zeros_hbm
    base = pl.program_id(0) * chunk

    def issue(i, carry):
        src = _token_rows(xn_ref, i)
        pltpu.make_async_copy(src, _token_rows(xs_hbm, pos1_ref[base + i]), sem).start()
        pltpu.make_async_copy(src, _token_rows(xs_hbm, pos2_ref[base + i]), sem).start()
        return carry

    lax.fori_loop(0, chunk, issue, 0, unroll=8)
    for _ in range(TOP_K):
        pltpu.make_async_copy(xn_ref, _token_rows(xs_hbm, 0, chunk), sem).wait()


def _dispatch(pos1, pos2, xn_tiles, n_rows, chunk):
    t = pos1.shape[0]
    zeros = jnp.zeros((n_rows * SUBLANES, LANES), F32)
    return pl.pallas_call(
        functools.partial(_dispatch_kernel, chunk=chunk),
        grid_spec=pltpu.PrefetchScalarGridSpec(
            num_scalar_prefetch=2,
            grid=(t // chunk,),
            in_specs=[pl.BlockSpec((chunk * SUBLANES, LANES), lambda i, p1, p2: (i, 0)),
                      pl.BlockSpec(memory_space=pl.ANY)],
            out_specs=pl.BlockSpec(memory_space=pl.ANY),
            scratch_shapes=[pltpu.SemaphoreType.DMA],
        ),
        out_shape=jax.ShapeDtypeStruct(zeros.shape, F32),
        input_output_aliases={3: 0},
        compiler_params=_params("arbitrary"),
        name="moe_dispatch",
    )(pos1, pos2, xn_tiles, zeros)


def _expert_kernel(tile_expert_ref, n_used_ref, xs_ref, wg_ref, wu_ref, wd_ref, ys_ref, x_sc, acc_sc,
                   *, tm, d):
    del tile_expert_ref
    r = pl.program_id(0)
    f = pl.program_id(1)
    last = pl.num_programs(1) - 1
    used = r < n_used_ref[0]

    @pl.when(used & (f == 0))
    def _():
        x_sc[...] = jnp.concatenate(_from_token_tiles(xs_ref, tm, d), axis=-1).astype(BF16)

    @pl.when(used)
    def _():
        x = x_sc[...]
        a = _dot(x, wg_ref[...])
        b = _dot(x, wu_ref[...])
        y = _dot((a * _sigmoid(a) * b).astype(BF16), wd_ref[...])

        @pl.when(f == 0)
        def _():
            acc_sc[...] = y

        @pl.when((f > 0) & (f < last))
        def _():
            acc_sc[...] += y

        @pl.when(f == last)
        def _():
            _to_token_tiles(ys_ref, acc_sc[...] + y)

    @pl.when(jnp.logical_not(used) & (f == last))
    def _():
        ys_ref[...] = jnp.zeros(ys_ref.shape, F32)


def _experts(tile_expert, n_used, xs, wg, wu, wd, layer, tm, tf):
    _, n_e, d, f_dim = wg.shape
    n_tiles = xs.shape[0] // (tm * SUBLANES)
    assert f_dim // tf >= 2

    def row_tile(r, f, te, nu):
        return (jnp.minimum(r, nu[0] - 1), 0)

    return pl.pallas_call(
        functools.partial(_expert_kernel, tm=tm, d=d),
        grid_spec=pltpu.PrefetchScalarGridSpec(
            num_scalar_prefetch=2,
            grid=(n_tiles, f_dim // tf),
            in_specs=[
                pl.BlockSpec((tm * SUBLANES, LANES), row_tile),
                pl.BlockSpec((None, None, d, tf), lambda r, f, te, nu: (layer, te[r], 0, f)),
                pl.BlockSpec((None, None, d, tf), lambda r, f, te, nu: (layer, te[r], 0, f)),
                pl.BlockSpec((None, None, tf, d), lambda r, f, te, nu: (layer, te[r], f, 0)),
            ],
            out_specs=pl.BlockSpec((tm * SUBLANES, LANES), lambda r, f, te, nu: (r, 0)),
            scratch_shapes=[pltpu.VMEM((tm, d), BF16), pltpu.VMEM((tm, d), F32)],
        ),
        out_shape=jax.ShapeDtypeStruct(xs.shape, F32),
        compiler_params=_params("arbitrary", "arbitrary"),
        name="moe_experts",
    )(tile_expert, n_used, xs, wg, wu, wd)


def _combine_kernel(pos1_ref, pos2_ref, h_ref, meta_ref, ys_hbm, p_ref, g_ref, wgate_ref, wproj_ref,
                    o_ref, y_sc, sems, *, tm, d):
    i = pl.program_id(0)
    slot = i % 2

    def gather(tile, slot):
        base = tile * tm

        def issue(r, carry):
            for k, pos_ref in enumerate((pos1_ref, pos2_ref)):
                pltpu.make_async_copy(_token_rows(ys_hbm, pos_ref[base + r]),
                                      _token_rows(y_sc.at[TOP_K * slot + k], r), sems.at[slot]).start()
            return carry

        lax.fori_loop(0, tm, issue, 0, unroll=8)

    @pl.when(i == 0)
    def _():
        gather(0, 0)

    @pl.when(i + 1 < pl.num_programs(0))
    def _():
        gather(i + 1, 1 - slot)

    for k in range(TOP_K):
        pltpu.make_async_copy(_token_rows(ys_hbm, 0, tm), y_sc.at[TOP_K * slot + k], sems.at[slot]).wait()
    meta = meta_ref[...]
    p1 = meta[:, META_P1:META_P1 + 1]
    p2 = meta[:, META_P2:META_P2 + 1]
    y1 = _from_token_tiles(y_sc.at[TOP_K * slot], tm, d)
    y2 = _from_token_tiles(y_sc.at[TOP_K * slot + 1], tm, d)
    for a in range(d // LANES):
        cols = slice(a * LANES, (a + 1) * LANES)
        o_ref[:, cols] = h_ref[:, cols] + p1 * y1[a] + p2 * y2[a]
    o_ref[...] = _ple_update(o_ref[...], p_ref, g_ref, wgate_ref, wproj_ref)


def _combine(pos1, pos2, h, meta, ys, p, g, w_gate, w_proj, layer, tm):
    t, d = h.shape
    return pl.pallas_call(
        functools.partial(_combine_kernel, tm=tm, d=d),
        grid_spec=pltpu.PrefetchScalarGridSpec(
            num_scalar_prefetch=2,
            grid=(t // tm,),
            in_specs=[
                pl.BlockSpec((tm, d), lambda i, p1, p2: (i, 0)),
                pl.BlockSpec((tm, LANES), lambda i, p1, p2: (i, 0)),
                pl.BlockSpec(memory_space=pl.ANY),
            ] + _ple_specs(p, w_gate, w_proj, layer, tm, d),
            out_specs=pl.BlockSpec((tm, d), lambda i, p1, p2: (i, 0)),
            scratch_shapes=[pltpu.VMEM((2 * TOP_K, tm * SUBLANES, LANES), F32),
                            pltpu.SemaphoreType.DMA((2,))],
        ),
        out_shape=jax.ShapeDtypeStruct((t, d), F32),
        compiler_params=_params("arbitrary"),
        name="moe_combine",
    )(pos1, pos2, h, meta, ys, p, g, w_gate, w_proj)


def _ple_specs(p, w_gate, w_proj, layer, tm, d, resident_weights=False):
    def spec(block, index, make=pl.BlockSpec):
        return make(block, lambda i, *prefetch: index(i))
    weight = functools.partial(spec, make=_resident) if resident_weights else spec
    return [
        spec((None, tm, p.shape[-1]), lambda i: (layer, i, 0)),
        spec((1, d), lambda i: (0, 0)),
        weight((None,) + w_gate.shape[1:], lambda i: (layer, 0, 0)),
        weight((None,) + w_proj.shape[1:], lambda i: (layer, 0, 0)),
    ]


def _rope_tables(seq):
    rows = seq // GRID_W
    r = jnp.broadcast_to(jnp.arange(rows, dtype=F32)[:, None], (rows, GRID_W)).reshape(seq)
    c = jnp.broadcast_to(jnp.arange(GRID_W, dtype=F32)[None, :], (rows, GRID_W)).reshape(seq)
    inv = ROPE_THETA ** (-jnp.arange(0, AXIS_DIM, 2, dtype=F32) / AXIS_DIM)
    ang = jnp.concatenate([r[:, None] * inv, c[:, None] * inv], axis=-1)
    cos, sin = jnp.cos(ang), jnp.sin(ang)
    reps = LANES // HEAD_DIM
    return (jnp.tile(jnp.concatenate([cos, cos], axis=-1), (1, reps)),
            jnp.tile(jnp.concatenate([-sin, sin], axis=-1), (1, reps)))


def _tile(n, want):
    t = min(n, want)
    assert n % t == 0, (n, t)
    return t


def kernel(x, p, norm_mix, norm_ffn, w_in, q_norm, k_norm, conv_w, conv_b, conv_ln_g, conv_ln_b, w_out,
           ffn_wg, ffn_wu, ffn_wd, pool_w, pool_scale, router_w, moe_wg, moe_wu, moe_wd, ple_norm,
           ple_gate_w, ple_proj):
    batch, seq, d = x.shape
    depth = p.shape[0]
    t = batch * seq
    q_dim = N_HEADS * HEAD_DIM
    assert seq % GRID_W == 0 and d % LANES == 0

    tm = _tile(seq, 512)
    tq = _tile(seq, 256)
    tk = _tile(seq, 8192)
    tk_online = _tile(seq, 512)
    tc = _tile(seq, 256)
    tp = _tile(seq, 256)
    tf_moe = moe_wg.shape[3] // 2
    tm_moe = _tile(t, 1024)
    n_moe_tiles = TOP_K * t // tm_moe + N_EXPERTS
    dispatch_chunk = _tile(t, 2048)

    cos_t, sin_t = _rope_tables(seq)
    row = lambda v: v.reshape(1, -1)
    tile_heads = lambda v: jnp.tile(v, LANES // HEAD_DIM).reshape(1, LANES)

    bf = lambda w: w.astype(BF16)
    w_in_b, w_out_b = bf(w_in), bf(w_out)
    ffn_wg_b, ffn_wu_b, ffn_wd_b = bf(ffn_wg), bf(ffn_wu), bf(ffn_wd)
    pool_w_b = bf(pool_w)
    moe_wg_b, moe_wu_b, moe_wd_b = bf(moe_wg), bf(moe_wu), bf(moe_wd)
    ple_gate_b, ple_proj_b = bf(ple_gate_w), bf(ple_proj)
    p_rows = p.reshape(depth, t, -1)

    h = x.reshape(t, d)
    for i in range(depth):
        j = i // 2
        ple_args = (p_rows, row(ple_norm[i]), ple_gate_b, ple_proj_b, i)
        if i % 2 == 0:
            q, k, vt, u = _in_proj(h, row(norm_mix[i]), w_in_b, j, tile_heads(q_norm[j]),
                                   tile_heads(k_norm[j]), cos_t, sin_t, seq, tm)
            logit_bound = (HEAD_DIM ** 0.5 * LOG2E) * jnp.max(jnp.abs(q_norm[j])) * jnp.max(jnp.abs(k_norm[j]))
            a = _attention(q, k, vt, logit_bound, batch, seq, tq, tk, tk_online)
            c = _conv_module(u, conv_w[j], row(conv_b[j]), row(conv_ln_g[j]), row(conv_ln_b[j]),
                             batch, seq, tc)
            h = _even_tail(h, a, c, w_out_b, row(norm_ffn[i]), ffn_wg_b, ffn_wu_b, ffn_wd_b,
                           p_rows, row(ple_norm[i]), ple_gate_b, ple_proj_b, j, i, tm)
        else:
            h = _pool_mixer(h, row(norm_mix[i]), pool_w_b, j, row(pool_scale[j]), batch, seq, tp)
            wr_pad = jnp.pad(router_w[j], ((0, 0), (0, LANES - N_EXPERTS)))
            xn_tiles, meta, counts = _router(h, row(norm_ffn[i]), wr_pad, tm)
            pos1, pos2, tile_expert, n_used = _routing_tables(meta, counts, tm_moe, n_moe_tiles)
            xs = _dispatch(pos1, pos2, xn_tiles, n_moe_tiles * tm_moe, dispatch_chunk)
            ys = _experts(tile_expert, n_used, xs, moe_wg_b, moe_wu_b, moe_wd_b, j, tm_moe, tf_moe)
            h = _combine(pos1, pos2, h, meta, ys, *ple_args, tm)
    return h.reshape(batch, seq, d)
```

```python
import functools

import jax
import jax.numpy as jnp
from jax import lax
from jax.experimental import pallas as pl
from jax.experimental.pallas import tpu as pltpu

F32 = jnp.float32
BF16 = jnp.bfloat16

GRID_W = 64
N_HEADS = 8
KV_HEADS = 2
HEAD_DIM = 64
Q_PER_KV = N_HEADS // KV_HEADS
AXIS_DIM = HEAD_DIM // 2
ROPE_THETA = 10000.0
CONV_WIDTH = 31
POOL_WINDOWS = (2, 4, 8, 16)
N_EXPERTS = 8
TOP_K = 2
EPS = 1e-6

LANES = 128
SUBLANES = 8
VMEM_LIMIT = 56 * 1024 * 1024
BF16_SUBLANES = 16
VT_ROWS = HEAD_DIM + BF16_SUBLANES
LOG2E = 1.4426950408889634
Q_SCALE = HEAD_DIM ** -0.5 * LOG2E
MAX_UNSHIFTED_LOGIT = 80.0
CONV_HALO = 16
CONV_ROWS = 64
POOL_HALO = 8


def _params(*sem):
    return pltpu.CompilerParams(dimension_semantics=sem, vmem_limit_bytes=VMEM_LIMIT)


def _rms(x, g):
    return x * lax.rsqrt(jnp.mean(x * x, axis=-1, keepdims=True) + EPS) * g


def _sigmoid(x):
    return 1.0 / (1.0 + jnp.exp(-x))


def _dot(a, b):
    return jnp.dot(a, b, preferred_element_type=F32)


def _ple_update(h, p_ref, g_ref, wgate_ref, wproj_ref):
    gate = _sigmoid(_dot(_rms(h, g_ref[...]).astype(BF16), wgate_ref[...]))
    return h + gate * _dot(p_ref[...].astype(BF16), wproj_ref[...])


def _inproj_kernel(h_ref, g_ref, w_ref, qg_ref, kg_ref, cos_ref, sin_ref,
                   q_ref, k_ref, vt_ref, u_ref, *, q_dim, kv_dim, conv_ch):
    xn = _rms(h_ref[...], g_ref[...]).astype(BF16)
    proj = _dot(xn, w_ref[...])
    tm = proj.shape[0]
    cos = cos_ref[...]
    sin = sin_ref[...]
    lane = lax.broadcasted_iota(jnp.int32, (tm, LANES), 1)
    head0 = lane < HEAD_DIM
    first_half = (lane % HEAD_DIM) < (HEAD_DIM // 2)

    def norm_rope(x, g, scale):
        sq = x * x
        s0 = jnp.sum(jnp.where(head0, sq, 0.0), axis=-1, keepdims=True)
        s1 = jnp.sum(jnp.where(head0, 0.0, sq), axis=-1, keepdims=True)
        ms = jnp.where(head0, s0, s1) * (1.0 / HEAD_DIM)
        y = x * lax.rsqrt(ms + EPS) * g
        partner = jnp.where(first_half,
                            pltpu.roll(y, LANES - HEAD_DIM // 2, 1),
                            pltpu.roll(y, HEAD_DIM // 2, 1))
        return (y * cos + partner * sin) * scale

    for c in range(q_dim // LANES):
        x = proj[:, c * LANES:(c + 1) * LANES]
        q_ref[:, c * LANES:(c + 1) * LANES] = norm_rope(x, qg_ref[...], Q_SCALE).astype(BF16)
    sub = lax.broadcasted_iota(jnp.int32, (VT_ROWS - HEAD_DIM, tm), 0)
    ones_rows = jnp.where(sub == 0, 1.0, 0.0).astype(BF16)
    for c in range(kv_dim // LANES):
        x = proj[:, q_dim + c * LANES:q_dim + (c + 1) * LANES]
        kk = norm_rope(x, kg_ref[...], 1.0).astype(BF16)
        vv_t = proj[:, q_dim + kv_dim + c * LANES:q_dim + kv_dim + (c + 1) * LANES].T
        for j in range(LANES // HEAD_DIM):
            head = c * (LANES // HEAD_DIM) + j
            k_ref[head] = kk[:, j * HEAD_DIM:(j + 1) * HEAD_DIM]
            vt_ref[head, 0:HEAD_DIM, :] = vv_t[j * HEAD_DIM:(j + 1) * HEAD_DIM, :].astype(BF16)
            vt_ref[head, HEAD_DIM:VT_ROWS, :] = ones_rows
    u0 = q_dim + 2 * kv_dim
    u_ref[...] = (proj[:, u0:u0 + conv_ch] * _sigmoid(proj[:, u0 + conv_ch:u0 + 2 * conv_ch])).astype(BF16)


def _in_proj(h, g, w_in, layer, qg, kg, cos_t, sin_t, seq, tm):
    t, d = h.shape
    w_in_dim = w_in.shape[-1]
    q_dim = N_HEADS * HEAD_DIM
    kv_dim = KV_HEADS * HEAD_DIM
    conv_ch = (w_in_dim - q_dim - 2 * kv_dim) // 2
    n_seq_blocks = seq // tm
    kern = functools.partial(_inproj_kernel, q_dim=q_dim, kv_dim=kv_dim, conv_ch=conv_ch)
    return pl.pallas_call(
        kern,
        grid=(t // tm,),
        in_specs=[
            pl.BlockSpec((tm, d), lambda i: (i, 0)),
            pl.BlockSpec((1, d), lambda i: (0, 0)),
            pl.BlockSpec((None, d, w_in_dim), lambda i: (layer, 0, 0)),
            pl.BlockSpec((1, LANES), lambda i: (0, 0)),
            pl.BlockSpec((1, LANES), lambda i: (0, 0)),
            pl.BlockSpec((tm, LANES), lambda i: (i % n_seq_blocks, 0)),
            pl.BlockSpec((tm, LANES), lambda i: (i % n_seq_blocks, 0)),
        ],
        out_specs=[
            pl.BlockSpec((tm, q_dim), lambda i: (i, 0)),
            pl.BlockSpec((KV_HEADS, tm, HEAD_DIM), lambda i: (0, i, 0)),
            pl.BlockSpec((KV_HEADS, VT_ROWS, tm), lambda i: (0, 0, i)),
            pl.BlockSpec((tm, conv_ch), lambda i: (i, 0)),
        ],
        out_shape=[
            jax.ShapeDtypeStruct((t, q_dim), BF16),
            jax.ShapeDtypeStruct((KV_HEADS, t, HEAD_DIM), BF16),
            jax.ShapeDtypeStruct((KV_HEADS, VT_ROWS, t), BF16),
            jax.ShapeDtypeStruct((t, conv_ch), BF16),
        ],
        compiler_params=_params("parallel"),
        name="in_proj",
    )(h, g, w_in, qg, kg, cos_t, sin_t)


def _stack_query_heads(q_ref, q_sc, tq):
    for g in range(Q_PER_KV):
        q_sc[g * tq:(g + 1) * tq, :] = q_ref[:, g * HEAD_DIM:(g + 1) * HEAD_DIM]


def _unstack_query_heads(out, tq):
    return jnp.concatenate([out[g * tq:(g + 1) * tq, :] for g in range(Q_PER_KV)], axis=-1).astype(BF16)


_NT = (((1,), (1,)), ((), ()))


def _attn_unshifted_kernel(q_ref, k_ref, vt_ref, o_ref, q_sc, acc_sc, *, tq, tk, n_kv):
    _stack_query_heads(q_ref, q_sc, tq)
    acc_sc[...] = jnp.zeros(acc_sc.shape, F32)

    def body(j, carry):
        kv0 = pl.multiple_of(j * tk, tk)
        s_t = lax.dot_general(k_ref[0, pl.ds(kv0, tk), :], q_sc[...], _NT,
                              preferred_element_type=F32)
        p_t = jnp.exp2(s_t).astype(BF16)
        acc_sc[...] += _dot(vt_ref[0, :, pl.ds(kv0, tk)], p_t)
        return carry

    lax.fori_loop(0, n_kv, body, 0)
    acc = acc_sc[...]
    out_t = acc[:HEAD_DIM, :] / acc[HEAD_DIM:HEAD_DIM + 1, :]
    o_ref[...] = _unstack_query_heads(out_t.T, tq)


def _attn_online_kernel(q_ref, k_ref, vt_ref, o_ref, q_sc, m_sc, l_sc, acc_sc, *, tq, tk, n_kv):
    _stack_query_heads(q_ref, q_sc, tq)
    m_sc[...] = jnp.full(m_sc.shape, -jnp.inf, F32)
    l_sc[...] = jnp.zeros(l_sc.shape, F32)
    acc_sc[...] = jnp.zeros(acc_sc.shape, F32)

    def body(j, carry):
        kv0 = pl.multiple_of(j * tk, tk)
        s = lax.dot_general(q_sc[...], k_ref[0, pl.ds(kv0, tk), :], _NT, preferred_element_type=F32)
        m_prev = m_sc[...]
        m_new = jnp.maximum(m_prev, jnp.max(s, axis=-1, keepdims=True))
        alpha = jnp.exp2(m_prev - m_new)
        p = jnp.exp2(s - m_new)
        l_sc[...] = alpha * l_sc[...] + jnp.sum(p, axis=-1, keepdims=True)
        v_t = vt_ref[0, 0:HEAD_DIM, pl.ds(kv0, tk)]
        acc_sc[...] = alpha * acc_sc[...] + lax.dot_general(p.astype(BF16), v_t, _NT,
                                                            preferred_element_type=F32)
        m_sc[...] = m_new
        return carry

    lax.fori_loop(0, n_kv, body, 0)
    o_ref[...] = _unstack_query_heads(acc_sc[...] / l_sc[...], tq)


def _attention_call(kern, scratch, name, q, k, vt, batch, seq, tq, tk):
    t = q.shape[0]
    n_q = seq // tq
    gw = Q_PER_KV * HEAD_DIM
    return pl.pallas_call(
        functools.partial(kern, tq=tq, tk=tk, n_kv=seq // tk),
        grid=(batch, KV_HEADS, n_q),
        in_specs=[
            pl.BlockSpec((tq, gw), lambda b, h, i: (b * n_q + i, h)),
            pl.BlockSpec((1, seq, HEAD_DIM), lambda b, h, i: (h, b, 0)),
            pl.BlockSpec((1, VT_ROWS, seq), lambda b, h, i: (h, 0, b)),
        ],
        out_specs=pl.BlockSpec((tq, gw), lambda b, h, i: (b * n_q + i, h)),
        out_shape=jax.ShapeDtypeStruct((t, N_HEADS * HEAD_DIM), BF16),
        scratch_shapes=[pltpu.VMEM((Q_PER_KV * tq, HEAD_DIM), BF16)] + scratch,
        compiler_params=_params("parallel", "parallel", "parallel"),
        name=name,
    )(q, k, vt)


def _attention(q, k, vt, logit_bound, batch, seq, tq, tk, tk_online):
    m = Q_PER_KV * tq

    def unshifted(q, k, vt):
        return _attention_call(_attn_unshifted_kernel, [pltpu.VMEM((VT_ROWS, m), F32)],
                               "attention", q, k, vt, batch, seq, tq, tk)

    def online(q, k, vt):
        scratch = [pltpu.VMEM((m, 1), F32), pltpu.VMEM((m, 1), F32), pltpu.VMEM((m, HEAD_DIM), F32)]
        return _attention_call(_attn_online_kernel, scratch, "attention_online",
                               q, k, vt, batch, seq, tq, tk_online)

    return lax.cond(logit_bound <= MAX_UNSHIFTED_LOGIT, unshifted, online, q, k, vt)


def _conv_kernel(prev_ref, main_ref, next_ref, w_ref, b_ref, g_ref, beta_ref, o_ref, ext_sc, shift_sc, y_sc,
                 *, tc, n_blocks):
    i = pl.program_id(1)
    ch = main_ref.shape[1]
    prev = prev_ref[...].astype(F32)
    nxt = next_ref[...].astype(F32)
    ext_sc[0:CONV_HALO, :] = jnp.where(i == 0, 0.0, prev)
    ext_sc[CONV_HALO:CONV_HALO + tc, :] = main_ref[...].astype(F32)
    ext_sc[CONV_HALO + tc:, :] = jnp.where(i == n_blocks - 1, 0.0, nxt)
    rows = shift_sc.shape[1]
    for b in range(SUBLANES):
        shift_sc[b] = ext_sc[b:b + rows, :]
    base = CONV_HALO - CONV_WIDTH // 2
    for r0 in range(0, tc, CONV_ROWS):
        for c in range(ch // LANES):
            cols = slice(c * LANES, (c + 1) * LANES)
            acc = jnp.zeros((CONV_ROWS, LANES), F32) + b_ref[:, cols]
            for kk in range(CONV_WIDTH):
                a, b = divmod(base + kk, SUBLANES)
                row0 = a * SUBLANES + r0
                acc = acc + shift_sc[b, row0:row0 + CONV_ROWS, cols] * w_ref[kk:kk + 1, cols]
            y_sc[r0:r0 + CONV_ROWS, cols] = acc
    y = y_sc[...]
    mu = jnp.mean(y, axis=-1, keepdims=True)
    yc = y - mu
    var = jnp.mean(yc * yc, axis=-1, keepdims=True)
    z = yc * lax.rsqrt(var + EPS) * g_ref[...] + beta_ref[...]
    o_ref[...] = (z * _sigmoid(z)).astype(BF16)


def _conv_module(u, conv_w, conv_b, ln_g, ln_b, batch, seq, tc):
    t, ch = u.shape
    n_blocks = seq // tc
    hb = tc // CONV_HALO
    n_halo = seq // CONV_HALO
    kern = functools.partial(_conv_kernel, tc=tc, n_blocks=n_blocks)
    return pl.pallas_call(
        kern,
        grid=(batch, n_blocks),
        in_specs=[
            pl.BlockSpec((CONV_HALO, ch), lambda b, i: (b * n_halo + jnp.maximum(i * hb - 1, 0), 0)),
            pl.BlockSpec((tc, ch), lambda b, i: (b * n_blocks + i, 0)),
            pl.BlockSpec((CONV_HALO, ch), lambda b, i: (b * n_halo + jnp.minimum((i + 1) * hb, n_halo - 1), 0)),
            pl.BlockSpec((CONV_WIDTH, ch), lambda b, i: (0, 0)),
            pl.BlockSpec((1, ch), lambda b, i: (0, 0)),
            pl.BlockSpec((1, ch), lambda b, i: (0, 0)),
            pl.BlockSpec((1, ch), lambda b, i: (0, 0)),
        ],
        out_specs=pl.BlockSpec((tc, ch), lambda b, i: (b * n_blocks + i, 0)),
        out_shape=jax.ShapeDtypeStruct((t, ch), BF16),
        scratch_shapes=[
            pltpu.VMEM((tc + 2 * CONV_HALO, ch), F32),
            pltpu.VMEM((SUBLANES, tc + 2 * CONV_HALO - SUBLANES, ch), F32),
            pltpu.VMEM((tc, ch), F32),
        ],
        compiler_params=_params("parallel", "parallel"),
        name="conv_module",
    )(u, u, u, conv_w, conv_b, ln_g, ln_b)


def _resident(block, index_map):
    return pl.BlockSpec(block, index_map, pipeline_mode=pl.Buffered(1))


def _even_tail_kernel(h_ref, a_ref, c_ref, wa_ref, wc_ref, g_ref, wg_ref, wu_ref, wd_ref,
                      p_ref, pg_ref, wgate_ref, wproj_ref, o_ref):
    h = h_ref[...] + _dot(a_ref[...], wa_ref[...]) + _dot(c_ref[...], wc_ref[...])
    xn = _rms(h, g_ref[...]).astype(BF16)
    a = _dot(xn, wg_ref[...])
    b = _dot(xn, wu_ref[...])
    h = h + _dot((a * _sigmoid(a) * b).astype(BF16), wd_ref[...])
    o_ref[...] = _ple_update(h, p_ref, pg_ref, wgate_ref, wproj_ref)


def _even_tail(h, a, c, w_out, g, wg, wu, wd, p, pg, w_gate, w_proj, layer, ple_layer, tm):
    t, d = h.shape
    f_dim = wg.shape[-1]
    assert a.shape[1] == c.shape[1] and a.shape[1] + c.shape[1] == w_out.shape[1]
    return pl.pallas_call(
        _even_tail_kernel,
        grid=(t // tm,),
        in_specs=[
            pl.BlockSpec((tm, d), lambda i: (i, 0)),
            pl.BlockSpec((tm, a.shape[1]), lambda i: (i, 0)),
            pl.BlockSpec((tm, c.shape[1]), lambda i: (i, 0)),
            _resident((None, a.shape[1], d), lambda i: (layer, 0, 0)),
            _resident((None, c.shape[1], d), lambda i: (layer, 1, 0)),
            pl.BlockSpec((1, d), lambda i: (0, 0)),
            _resident((None, d, f_dim), lambda i: (layer, 0, 0)),
            _resident((None, d, f_dim), lambda i: (layer, 0, 0)),
            _resident((None, f_dim, d), lambda i: (layer, 0, 0)),
        ] + _ple_specs(p, w_gate, w_proj, ple_layer, tm, d, resident_weights=True),
        out_specs=pl.BlockSpec((tm, d), lambda i: (i, 0)),
        out_shape=jax.ShapeDtypeStruct((t, d), F32),
        compiler_params=_params("parallel"),
        name="out_proj_swiglu_ple",
    )(h, a, c, w_out, w_out, g, wg, wu, wd, p, pg, w_gate, w_proj)


def _pool_kernel(prev_ref, main_ref, next_ref, g_ref, w_ref, sc_ref, o_ref, *, tp, seq):
    i = pl.program_id(1)
    g = g_ref[...]
    h_main = main_ref[...]
    hn_main = _rms(h_main, g)
    ext = jnp.concatenate([_rms(prev_ref[...], g), hn_main, _rms(next_ref[...], g)], axis=0).astype(BF16)
    rows = tp + 2 * POOL_HALO
    t_pos = i * tp + lax.broadcasted_iota(jnp.int32, (tp, rows), 0)
    j_pos = i * tp - POOL_HALO + lax.broadcasted_iota(jnp.int32, (tp, rows), 1)
    in_seq = (j_pos >= 0) & (j_pos < seq)
    t_col = i * tp + lax.broadcasted_iota(jnp.int32, (tp, 1), 0)
    gc = w_ref.shape[1]
    for gi, win in enumerate(POOL_WINDOWS):
        left = win // 2
        right = win - 1 - left
        band = (in_seq & (j_pos >= t_pos - left) & (j_pos <= t_pos + right)).astype(BF16)
        cnt = jnp.minimum(t_col + right, seq - 1) - jnp.maximum(t_col - left, 0) + 1
        cols = slice(gi * gc, (gi + 1) * gc)
        win_sum = _dot(band, ext[:, cols])
        y = (win_sum / cnt.astype(F32) - hn_main[:, cols]).astype(BF16)
        o_ref[:, cols] = h_main[:, cols] + _dot(y, w_ref[gi]) * sc_ref[:, cols]


def _pool_mixer(h, g, pool_w, layer, pool_scale, batch, seq, tp):
    t, d = h.shape
    n_blocks = seq // tp
    hb = tp // POOL_HALO
    n_halo = seq // POOL_HALO
    kern = functools.partial(_pool_kernel, tp=tp, seq=seq)
    return pl.pallas_call(
        kern,
        grid=(batch, n_blocks),
        in_specs=[
            pl.BlockSpec((POOL_HALO, d), lambda b, i: (b * n_halo + jnp.maximum(i * hb - 1, 0), 0)),
            pl.BlockSpec((tp, d), lambda b, i: (b * n_blocks + i, 0)),
            pl.BlockSpec((POOL_HALO, d), lambda b, i: (b * n_halo + jnp.minimum((i + 1) * hb, n_halo - 1), 0)),
            pl.BlockSpec((1, d), lambda b, i: (0, 0)),
            pl.BlockSpec((None,) + pool_w.shape[1:], lambda b, i: (layer, 0, 0, 0)),
            pl.BlockSpec((1, d), lambda b, i: (0, 0)),
        ],
        out_specs=pl.BlockSpec((tp, d), lambda b, i: (b * n_blocks + i, 0)),
        out_shape=jax.ShapeDtypeStruct((t, d), F32),
        compiler_params=_params("parallel", "parallel"),
        name="pool_mixer",
    )(h, h, h, g, pool_w, pool_scale)


META_E1, META_E2, META_RANK1, META_RANK2, META_P1, META_P2 = range(6)


def _to_token_tiles(ref, x):
    tm, d = x.shape
    for a in range(d // LANES):
        ref[pl.ds(a, tm, stride=SUBLANES), :] = x[:, a * LANES:(a + 1) * LANES]


def _from_token_tiles(ref, tm, d):
    return [ref[pl.ds(a, tm, stride=SUBLANES), :] for a in range(d // LANES)]


def _router_kernel(h_ref, g_ref, wr_ref, xn_ref, meta_ref, counts_ref):
    @pl.when(pl.program_id(0) == 0)
    def _():
        counts_ref[...] = jnp.zeros(counts_ref.shape, F32)

    xn = _rms(h_ref[...], g_ref[...])
    _to_token_tiles(xn_ref, xn)
    logits = jnp.dot(xn, wr_ref[...], preferred_element_type=F32, precision=lax.Precision.HIGHEST)
    tm = logits.shape[0]
    lane = lax.broadcasted_iota(jnp.int32, logits.shape, 1)
    logits = jnp.where(lane < N_EXPERTS, logits, -jnp.inf)
    v1 = jnp.max(logits, axis=-1, keepdims=True)
    i1 = jnp.min(jnp.where(logits == v1, lane, LANES), axis=-1, keepdims=True)
    rest = jnp.where(lane == i1, -jnp.inf, logits)
    v2 = jnp.max(rest, axis=-1, keepdims=True)
    i2 = jnp.min(jnp.where(rest == v2, lane, LANES), axis=-1, keepdims=True)
    e2 = jnp.exp(v2 - v1)
    p1 = 1.0 / (1.0 + e2)
    p2 = e2 * p1
    chosen = (lane == i1) | (lane == i2)
    earlier = (lax.broadcasted_iota(jnp.int32, (tm, tm), 0) > lax.broadcasted_iota(jnp.int32, (tm, tm), 1))
    before = _dot(earlier.astype(BF16), chosen.astype(BF16)) + counts_ref[...]
    rank1 = jnp.sum(jnp.where(lane == i1, before, 0.0), axis=-1, keepdims=True)
    rank2 = jnp.sum(jnp.where(lane == i2, before, 0.0), axis=-1, keepdims=True)
    counts_ref[...] += jnp.sum(chosen.astype(F32), axis=0, keepdims=True)
    meta = jnp.zeros(logits.shape, F32)
    for col, val in ((META_E1, i1.astype(F32)), (META_E2, i2.astype(F32)), (META_RANK1, rank1),
                     (META_RANK2, rank2), (META_P1, p1), (META_P2, p2)):
        meta = jnp.where(lane == col, val, meta)
    meta_ref[...] = meta


def _router(h, g, wr_pad, tm):
    t, d = h.shape
    return pl.pallas_call(
        _router_kernel,
        grid=(t // tm,),
        in_specs=[
            pl.BlockSpec((tm, d), lambda i: (i, 0)),
            pl.BlockSpec((1, d), lambda i: (0, 0)),
            pl.BlockSpec(wr_pad.shape, lambda i: (0, 0)),
        ],
        out_specs=[
            pl.BlockSpec((tm * SUBLANES, LANES), lambda i: (i, 0)),
            pl.BlockSpec((tm, LANES), lambda i: (i, 0)),
            pl.BlockSpec((1, LANES), lambda i: (0, 0)),
        ],
        out_shape=[
            jax.ShapeDtypeStruct((t * SUBLANES, LANES), F32),
            jax.ShapeDtypeStruct((t, LANES), F32),
            jax.ShapeDtypeStruct((1, LANES), F32),
        ],
        compiler_params=_params("arbitrary"),
        name="router",
    )(h, g, wr_pad)


def _routing_tables(meta, counts, tile_rows, n_tiles):
    cnt = counts[0, :N_EXPERTS].astype(jnp.int32)
    padded = (cnt + tile_rows - 1) // tile_rows * tile_rows
    ends = jnp.cumsum(padded)
    starts = ends - padded
    pos1 = starts[meta[:, META_E1].astype(jnp.int32)] + meta[:, META_RANK1].astype(jnp.int32)
    pos2 = starts[meta[:, META_E2].astype(jnp.int32)] + meta[:, META_RANK2].astype(jnp.int32)
    n_used = ends[-1] // tile_rows
    tile_start = jnp.minimum(jnp.arange(n_tiles, dtype=jnp.int32), n_used - 1) * tile_rows
    tile_expert = jnp.sum((tile_start[:, None] >= ends[None, :]).astype(jnp.int32), axis=-1)
    n_free = n_tiles * tile_rows - TOP_K * meta.shape[0]
    pad_ends = jnp.cumsum(padded - cnt)
    slot = jnp.arange(n_free, dtype=jnp.int32)
    owner = jnp.minimum(jnp.sum((slot[:, None] >= pad_ends[None, :]).astype(jnp.int32), axis=-1), N_EXPERTS - 1)
    free_pos = jnp.where(slot < pad_ends[-1], ends[owner] - (pad_ends[owner] - slot),
                         ends[-1] + (slot - pad_ends[-1]))
    return pos1, pos2, tile_expert, n_used.reshape(1), free_pos


def _token_rows(ref, i, n=1):
    return ref.at[pl.ds(pl.multiple_of(i * SUBLANES, SUBLANES), n * SUBLANES)]


def _dispatch_kernel(pos1_ref, pos2_ref, free_pos_ref, xn_ref, xs_hbm, zero_sc, sem, zero_sem,
                     *, chunk, free_chunk):
    step = pl.program_id(0)
    base = step * chunk
    zero_sc[...] = jnp.zeros(zero_sc.shape, F32)

    def zero_row(i, carry):
        dst = _token_rows(xs_hbm, free_pos_ref[step * free_chunk + i])
        pltpu.make_async_copy(_token_rows(zero_sc, i), dst, zero_sem).start()
        return carry

    lax.fori_loop(0, free_chunk, zero_row, 0, unroll=8)

    def issue(i, carry):
        src = _token_rows(xn_ref, i)
        pltpu.make_async_copy(src, _token_rows(xs_hbm, pos1_ref[base + i]), sem).start()
        pltpu.make_async_copy(src, _token_rows(xs_hbm, pos2_ref[base + i]), sem).start()
        return carry

    lax.fori_loop(0, chunk, issue, 0, unroll=8)
    for _ in range(TOP_K):
        pltpu.make_async_copy(xn_ref, _token_rows(xs_hbm, 0, chunk), sem).wait()
    pltpu.make_async_copy(zero_sc, _token_rows(xs_hbm, 0, free_chunk), zero_sem).wait()


def _dispatch(pos1, pos2, free_pos, xn_tiles, n_rows, chunk):
    t = pos1.shape[0]
    n_steps = t // chunk
    assert free_pos.shape[0] == n_rows - TOP_K * t and free_pos.shape[0] % n_steps == 0
    free_chunk = free_pos.shape[0] // n_steps
    return pl.pallas_call(
        functools.partial(_dispatch_kernel, chunk=chunk, free_chunk=free_chunk),
        grid_spec=pltpu.PrefetchScalarGridSpec(
            num_scalar_prefetch=3,
            grid=(n_steps,),
            in_specs=[pl.BlockSpec((chunk * SUBLANES, LANES), lambda i, *prefetch: (i, 0))],
            out_specs=pl.BlockSpec(memory_space=pl.ANY),
            scratch_shapes=[pltpu.VMEM((free_chunk * SUBLANES, LANES), F32), pltpu.SemaphoreType.DMA,
                            pltpu.SemaphoreType.DMA],
        ),
        out_shape=jax.ShapeDtypeStruct((n_rows * SUBLANES, LANES), F32),
        compiler_params=_params("arbitrary"),
        name="moe_dispatch",
    )(pos1, pos2, free_pos, xn_tiles)


def _expert_kernel(tile_expert_ref, n_used_ref, xs_ref, wg_ref, wu_ref, wd_ref, ys_ref, x_sc, acc_sc,
                   *, tm, d):
    del tile_expert_ref
    r = pl.program_id(0)
    f = pl.program_id(1)
    last = pl.num_programs(1) - 1
    used = r < n_used_ref[0]

    @pl.when(used & (f == 0))
    def _():
        x_sc[...] = jnp.concatenate(_from_token_tiles(xs_ref, tm, d), axis=-1).astype(BF16)

    @pl.when(used)
    def _():
        x = x_sc[...]
        a = _dot(x, wg_ref[...])
        b = _dot(x, wu_ref[...])
        y = _dot((a * _sigmoid(a) * b).astype(BF16), wd_ref[...])

        @pl.when(f == 0)
        def _():
            acc_sc[...] = y

        @pl.when((f > 0) & (f < last))
        def _():
            acc_sc[...] += y

        @pl.when(f == last)
        def _():
            _to_token_tiles(ys_ref, acc_sc[...] + y)

    @pl.when(jnp.logical_not(used) & (f == last))
    def _():
        ys_ref[...] = jnp.zeros(ys_ref.shape, F32)


def _experts(tile_expert, n_used, xs, wg, wu, wd, layer, tm, tf):
    _, n_e, d, f_dim = wg.shape
    n_tiles = xs.shape[0] // (tm * SUBLANES)
    assert f_dim // tf >= 2

    def row_tile(r, f, te, nu):
        return (jnp.minimum(r, nu[0] - 1), 0)

    return pl.pallas_call(
        functools.partial(_expert_kernel, tm=tm, d=d),
        grid_spec=pltpu.PrefetchScalarGridSpec(
            num_scalar_prefetch=2,
            grid=(n_tiles, f_dim // tf),
            in_specs=[
                pl.BlockSpec((tm * SUBLANES, LANES), row_tile),
                pl.BlockSpec((None, None, d, tf), lambda r, f, te, nu: (layer, te[r], 0, f)),
                pl.BlockSpec((None, None, d, tf), lambda r, f, te, nu: (layer, te[r], 0, f)),
                pl.BlockSpec((None, None, tf, d), lambda r, f, te, nu: (layer, te[r], f, 0)),
            ],
            out_specs=pl.BlockSpec((tm * SUBLANES, LANES), lambda r, f, te, nu: (r, 0)),
            scratch_shapes=[pltpu.VMEM((tm, d), BF16), pltpu.VMEM((tm, d), F32)],
        ),
        out_shape=jax.ShapeDtypeStruct(xs.shape, F32),
        compiler_params=_params("arbitrary", "arbitrary"),
        name="moe_experts",
    )(tile_expert, n_used, xs, wg, wu, wd)


def _combine_kernel(pos1_ref, pos2_ref, h_ref, meta_ref, ys_hbm, p_ref, g_ref, wgate_ref, wproj_ref,
                    o_ref, y_sc, sems, *, tm, d):
    i = pl.program_id(0)
    slot = i % 2

    def gather(tile, slot):
        base = tile * tm

        def issue(r, carry):
            for k, pos_ref in enumerate((pos1_ref, pos2_ref)):
                pltpu.make_async_copy(_token_rows(ys_hbm, pos_ref[base + r]),
                                      _token_rows(y_sc.at[TOP_K * slot + k], r), sems.at[slot]).start()
            return carry

        lax.fori_loop(0, tm, issue, 0, unroll=8)

    @pl.when(i == 0)
    def _():
        gather(0, 0)

    @pl.when(i + 1 < pl.num_programs(0))
    def _():
        gather(i + 1, 1 - slot)

    for k in range(TOP_K):
        pltpu.make_async_copy(_token_rows(ys_hbm, 0, tm), y_sc.at[TOP_K * slot + k], sems.at[slot]).wait()
    meta = meta_ref[...]
    p1 = meta[:, META_P1:META_P1 + 1]
    p2 = meta[:, META_P2:META_P2 + 1]
    y1 = _from_token_tiles(y_sc.at[TOP_K * slot], tm, d)
    y2 = _from_token_tiles(y_sc.at[TOP_K * slot + 1], tm, d)
    for a in range(d // LANES):
        cols = slice(a * LANES, (a + 1) * LANES)
        o_ref[:, cols] = h_ref[:, cols] + p1 * y1[a] + p2 * y2[a]
    o_ref[...] = _ple_update(o_ref[...], p_ref, g_ref, wgate_ref, wproj_ref)


def _combine(pos1, pos2, h, meta, ys, p, g, w_gate, w_proj, layer, tm):
    t, d = h.shape
    return pl.pallas_call(
        functools.partial(_combine_kernel, tm=tm, d=d),
        grid_spec=pltpu.PrefetchScalarGridSpec(
            num_scalar_prefetch=2,
            grid=(t // tm,),
            in_specs=[
                pl.BlockSpec((tm, d), lambda i, p1, p2: (i, 0)),
                pl.BlockSpec((tm, LANES), lambda i, p1, p2: (i, 0)),
                pl.BlockSpec(memory_space=pl.ANY),
            ] + _ple_specs(p, w_gate, w_proj, layer, tm, d),
            out_specs=pl.BlockSpec((tm, d), lambda i, p1, p2: (i, 0)),
            scratch_shapes=[pltpu.VMEM((2 * TOP_K, tm * SUBLANES, LANES), F32),
                            pltpu.SemaphoreType.DMA((2,))],
        ),
        out_shape=jax.ShapeDtypeStruct((t, d), F32),
        compiler_params=_params("arbitrary"),
        name="moe_combine",
    )(pos1, pos2, h, meta, ys, p, g, w_gate, w_proj)


def _ple_specs(p, w_gate, w_proj, layer, tm, d, resident_weights=False):
    def spec(block, index, make=pl.BlockSpec):
        return make(block, lambda i, *prefetch: index(i))
    weight = functools.partial(spec, make=_resident) if resident_weights else spec
    return [
        spec((None, tm, p.shape[-1]), lambda i: (layer, i, 0)),
        spec((1, d), lambda i: (0, 0)),
        weight((None,) + w_gate.shape[1:], lambda i: (layer, 0, 0)),
        weight((None,) + w_proj.shape[1:], lambda i: (layer, 0, 0)),
    ]


def _rope_tables(seq):
    rows = seq // GRID_W
    r = jnp.broadcast_to(jnp.arange(rows, dtype=F32)[:, None], (rows, GRID_W)).reshape(seq)
    c = jnp.broadcast_to(jnp.arange(GRID_W, dtype=F32)[None, :], (rows, GRID_W)).reshape(seq)
    inv = ROPE_THETA ** (-jnp.arange(0, AXIS_DIM, 2, dtype=F32) / AXIS_DIM)
    ang = jnp.concatenate([r[:, None] * inv, c[:, None] * inv], axis=-1)
    cos, sin = jnp.cos(ang), jnp.sin(ang)
    reps = LANES // HEAD_DIM
    return (jnp.tile(jnp.concatenate([cos, cos], axis=-1), (1, reps)),
            jnp.tile(jnp.concatenate([-sin, sin], axis=-1), (1, reps)))


def _tile(n, want):
    t = min(n, want)
    assert n % t == 0, (n, t)
    return t


def kernel(x, p, norm_mix, norm_ffn, w_in, q_norm, k_norm, conv_w, conv_b, conv_ln_g, conv_ln_b, w_out,
           ffn_wg, ffn_wu, ffn_wd, pool_w, pool_scale, router_w, moe_wg, moe_wu, moe_wd, ple_norm,
           ple_gate_w, ple_proj):
    batch, seq, d = x.shape
    depth = p.shape[0]
    t = batch * seq
    q_dim = N_HEADS * HEAD_DIM
    assert seq % GRID_W == 0 and d % LANES == 0

    tm = _tile(seq, 512)
    tq = _tile(seq, 256)
    tk = _tile(seq, 8192)
    tk_online = _tile(seq, 512)
    tc = _tile(seq, 256)
    tp = _tile(seq, 256)
    tf_moe = moe_wg.shape[3] // 2
    tm_moe = _tile(t, 1024)
    n_moe_tiles = TOP_K * t // tm_moe + N_EXPERTS
    dispatch_chunk = _tile(t, 2048)

    cos_t, sin_t = _rope_tables(seq)
    row = lambda v: v.reshape(1, -1)
    tile_heads = lambda v: jnp.tile(v, LANES // HEAD_DIM).reshape(1, LANES)

    bf = lambda w: w.astype(BF16)
    w_in_b, w_out_b = bf(w_in), bf(w_out)
    ffn_wg_b, ffn_wu_b, ffn_wd_b = bf(ffn_wg), bf(ffn_wu), bf(ffn_wd)
    pool_w_b = bf(pool_w)
    moe_wg_b, moe_wu_b, moe_wd_b = bf(moe_wg), bf(moe_wu), bf(moe_wd)
    ple_gate_b, ple_proj_b = bf(ple_gate_w), bf(ple_proj)
    p_rows = p.reshape(depth, t, -1)

    h = x.reshape(t, d)
    for i in range(depth):
        j = i // 2
        ple_args = (p_rows, row(ple_norm[i]), ple_gate_b, ple_proj_b, i)
        if i % 2 == 0:
            q, k, vt, u = _in_proj(h, row(norm_mix[i]), w_in_b, j, tile_heads(q_norm[j]),
                                   tile_heads(k_norm[j]), cos_t, sin_t, seq, tm)
            logit_bound = (HEAD_DIM ** 0.5 * LOG2E) * jnp.max(jnp.abs(q_norm[j])) * jnp.max(jnp.abs(k_norm[j]))
            a = _attention(q, k, vt, logit_bound, batch, seq, tq, tk, tk_online)
            c = _conv_module(u, conv_w[j], row(conv_b[j]), row(conv_ln_g[j]), row(conv_ln_b[j]),
                             batch, seq, tc)
            h = _even_tail(h, a, c, w_out_b, row(norm_ffn[i]), ffn_wg_b, ffn_wu_b, ffn_wd_b,
                           p_rows, row(ple_norm[i]), ple_gate_b, ple_proj_b, j, i, tm)
        else:
            h = _pool_mixer(h, row(norm_mix[i]), pool_w_b, j, row(pool_scale[j]), batch, seq, tp)
            wr_pad = jnp.pad(router_w[j], ((0, 0), (0, LANES - N_EXPERTS)))
            xn_tiles, meta, counts = _router(h, row(norm_ffn[i]), wr_pad, tm)
            pos1, pos2, tile_expert, n_used, free_pos = _routing_tables(meta, counts, tm_moe, n_moe_tiles)
            xs = _dispatch(pos1, pos2, free_pos, xn_tiles, n_moe_tiles * tm_moe, dispatch_chunk)
            ys = _experts(tile_expert, n_used, xs, moe_wg_b, moe_wu_b, moe_wd_b, j, tm_moe, tf_moe)
            h = _combine(pos1, pos2, h, meta, ys, *ple_args, tm)
    return h.reshape(batch, seq, d)
```

```python
import functools

import jax
import jax.numpy as jnp
from jax import lax
from jax.experimental import pallas as pl
from jax.experimental.pallas import tpu as pltpu

F32 = jnp.float32
BF16 = jnp.bfloat16

GRID_W = 64
N_HEADS = 8
KV_HEADS = 2
HEAD_DIM = 64
Q_PER_KV = N_HEADS // KV_HEADS
AXIS_DIM = HEAD_DIM // 2
ROPE_THETA = 10000.0
CONV_WIDTH = 31
POOL_WINDOWS = (2, 4, 8, 16)
N_EXPERTS = 8
TOP_K = 2
EPS = 1e-6

LANES = 128
SUBLANES = 8
VMEM_LIMIT = 56 * 1024 * 1024
BF16_SUBLANES = 16
VT_ROWS = HEAD_DIM + BF16_SUBLANES
LOG2E = 1.4426950408889634
Q_SCALE = HEAD_DIM ** -0.5 * LOG2E
MAX_UNSHIFTED_LOGIT = 80.0
CONV_HALO = 16
CONV_ROWS = 64
POOL_HALO = 8


def _params(*sem):
    return pltpu.CompilerParams(dimension_semantics=sem, vmem_limit_bytes=VMEM_LIMIT)


def _rms(x, g):
    return x * lax.rsqrt(jnp.mean(x * x, axis=-1, keepdims=True) + EPS) * g


def _sigmoid(x):
    return 1.0 / (1.0 + jnp.exp(-x))


def _dot(a, b):
    return jnp.dot(a, b, preferred_element_type=F32)


def _ple_update(h, p_ref, g_ref, wgate_ref, wproj_ref):
    gate = _sigmoid(_dot(_rms(h, g_ref[...]).astype(BF16), wgate_ref[...]))
    return h + gate * _dot(p_ref[...].astype(BF16), wproj_ref[...])


def _inproj_kernel(h_ref, g_ref, w_ref, qg_ref, kg_ref, cos_ref, sin_ref,
                   q_ref, k_ref, vt_ref, u_ref, *, q_dim, kv_dim, conv_ch):
    xn = _rms(h_ref[...], g_ref[...]).astype(BF16)
    proj = _dot(xn, w_ref[...])
    tm = proj.shape[0]
    cos = cos_ref[...]
    sin = sin_ref[...]
    lane = lax.broadcasted_iota(jnp.int32, (tm, LANES), 1)
    head0 = lane < HEAD_DIM
    first_half = (lane % HEAD_DIM) < (HEAD_DIM // 2)

    def norm_rope(x, g, scale):
        sq = x * x
        s0 = jnp.sum(jnp.where(head0, sq, 0.0), axis=-1, keepdims=True)
        s1 = jnp.sum(jnp.where(head0, 0.0, sq), axis=-1, keepdims=True)
        ms = jnp.where(head0, s0, s1) * (1.0 / HEAD_DIM)
        y = x * lax.rsqrt(ms + EPS) * g
        partner = jnp.where(first_half,
                            pltpu.roll(y, LANES - HEAD_DIM // 2, 1),
                            pltpu.roll(y, HEAD_DIM // 2, 1))
        return (y * cos + partner * sin) * scale

    for c in range(q_dim // LANES):
        x = proj[:, c * LANES:(c + 1) * LANES]
        q_ref[:, c * LANES:(c + 1) * LANES] = norm_rope(x, qg_ref[...], Q_SCALE).astype(BF16)
    sub = lax.broadcasted_iota(jnp.int32, (VT_ROWS - HEAD_DIM, tm), 0)
    ones_rows = jnp.where(sub == 0, 1.0, 0.0).astype(BF16)
    for c in range(kv_dim // LANES):
        x = proj[:, q_dim + c * LANES:q_dim + (c + 1) * LANES]
        kk = norm_rope(x, kg_ref[...], 1.0).astype(BF16)
        vv_t = proj[:, q_dim + kv_dim + c * LANES:q_dim + kv_dim + (c + 1) * LANES].T
        for j in range(LANES // HEAD_DIM):
            head = c * (LANES // HEAD_DIM) + j
            k_ref[head] = kk[:, j * HEAD_DIM:(j + 1) * HEAD_DIM]
            vt_ref[head, 0:HEAD_DIM, :] = vv_t[j * HEAD_DIM:(j + 1) * HEAD_DIM, :].astype(BF16)
            vt_ref[head, HEAD_DIM:VT_ROWS, :] = ones_rows
    u0 = q_dim + 2 * kv_dim
    u_ref[...] = (proj[:, u0:u0 + conv_ch] * _sigmoid(proj[:, u0 + conv_ch:u0 + 2 * conv_ch])).astype(BF16)


def _in_proj(h, g, w_in, layer, qg, kg, cos_t, sin_t, seq, tm):
    t, d = h.shape
    w_in_dim = w_in.shape[-1]
    q_dim = N_HEADS * HEAD_DIM
    kv_dim = KV_HEADS * HEAD_DIM
    conv_ch = (w_in_dim - q_dim - 2 * kv_dim) // 2
    n_seq_blocks = seq // tm
    kern = functools.partial(_inproj_kernel, q_dim=q_dim, kv_dim=kv_dim, conv_ch=conv_ch)
    return pl.pallas_call(
        kern,
        grid=(t // tm,),
        in_specs=[
            pl.BlockSpec((tm, d), lambda i: (i, 0)),
            pl.BlockSpec((1, d), lambda i: (0, 0)),
            pl.BlockSpec((None, d, w_in_dim), lambda i: (layer, 0, 0)),
            pl.BlockSpec((1, LANES), lambda i: (0, 0)),
            pl.BlockSpec((1, LANES), lambda i: (0, 0)),
            pl.BlockSpec((tm, LANES), lambda i: (i % n_seq_blocks, 0)),
            pl.BlockSpec((tm, LANES), lambda i: (i % n_seq_blocks, 0)),
        ],
        out_specs=[
            pl.BlockSpec((tm, q_dim), lambda i: (i, 0)),
            pl.BlockSpec((KV_HEADS, tm, HEAD_DIM), lambda i: (0, i, 0)),
            pl.BlockSpec((KV_HEADS, VT_ROWS, tm), lambda i: (0, 0, i)),
            pl.BlockSpec((tm, conv_ch), lambda i: (i, 0)),
        ],
        out_shape=[
            jax.ShapeDtypeStruct((t, q_dim), BF16),
            jax.ShapeDtypeStruct((KV_HEADS, t, HEAD_DIM), BF16),
            jax.ShapeDtypeStruct((KV_HEADS, VT_ROWS, t), BF16),
            jax.ShapeDtypeStruct((t, conv_ch), BF16),
        ],
        compiler_params=_params("parallel"),
        name="in_proj",
    )(h, g, w_in, qg, kg, cos_t, sin_t)


def _stack_query_heads(q_ref, q_sc, tq):
    for g in range(Q_PER_KV):
        q_sc[g * tq:(g + 1) * tq, :] = q_ref[:, g * HEAD_DIM:(g + 1) * HEAD_DIM]


def _unstack_query_heads(out, tq):
    return jnp.concatenate([out[g * tq:(g + 1) * tq, :] for g in range(Q_PER_KV)], axis=-1).astype(BF16)


_NT = (((1,), (1,)), ((), ()))


def _attn_unshifted_kernel(q_ref, k_ref, vt_ref, o_ref, q_sc, acc_sc, *, tq, tk, n_kv):
    _stack_query_heads(q_ref, q_sc, tq)
    acc_sc[...] = jnp.zeros(acc_sc.shape, F32)

    def body(j, carry):
        kv0 = pl.multiple_of(j * tk, tk)
        s_t = lax.dot_general(k_ref[0, pl.ds(kv0, tk), :], q_sc[...], _NT,
                              preferred_element_type=F32)
        p_t = jnp.exp2(s_t).astype(BF16)
        acc_sc[...] += _dot(vt_ref[0, :, pl.ds(kv0, tk)], p_t)
        return carry

    lax.fori_loop(0, n_kv, body, 0)
    acc = acc_sc[...]
    out_t = acc[:HEAD_DIM, :] / acc[HEAD_DIM:HEAD_DIM + 1, :]
    o_ref[...] = _unstack_query_heads(out_t.T, tq)


def _attn_online_kernel(q_ref, k_ref, vt_ref, o_ref, q_sc, m_sc, l_sc, acc_sc, *, tq, tk, n_kv):
    _stack_query_heads(q_ref, q_sc, tq)
    m_sc[...] = jnp.full(m_sc.shape, -jnp.inf, F32)
    l_sc[...] = jnp.zeros(l_sc.shape, F32)
    acc_sc[...] = jnp.zeros(acc_sc.shape, F32)

    def body(j, carry):
        kv0 = pl.multiple_of(j * tk, tk)
        s = lax.dot_general(q_sc[...], k_ref[0, pl.ds(kv0, tk), :], _NT, preferred_element_type=F32)
        m_prev = m_sc[...]
        m_new = jnp.maximum(m_prev, jnp.max(s, axis=-1, keepdims=True))
        alpha = jnp.exp2(m_prev - m_new)
        p = jnp.exp2(s - m_new)
        l_sc[...] = alpha * l_sc[...] + jnp.sum(p, axis=-1, keepdims=True)
        v_t = vt_ref[0, 0:HEAD_DIM, pl.ds(kv0, tk)]
        acc_sc[...] = alpha * acc_sc[...] + lax.dot_general(p.astype(BF16), v_t, _NT,
                                                            preferred_element_type=F32)
        m_sc[...] = m_new
        return carry

    lax.fori_loop(0, n_kv, body, 0)
    o_ref[...] = _unstack_query_heads(acc_sc[...] / l_sc[...], tq)


def _attention_call(kern, scratch, name, q, k, vt, batch, seq, tq, tk):
    t = q.shape[0]
    n_q = seq // tq
    gw = Q_PER_KV * HEAD_DIM
    return pl.pallas_call(
        functools.partial(kern, tq=tq, tk=tk, n_kv=seq // tk),
        grid=(batch, KV_HEADS, n_q),
        in_specs=[
            pl.BlockSpec((tq, gw), lambda b, h, i: (b * n_q + i, h)),
            pl.BlockSpec((1, seq, HEAD_DIM), lambda b, h, i: (h, b, 0)),
            pl.BlockSpec((1, VT_ROWS, seq), lambda b, h, i: (h, 0, b)),
        ],
        out_specs=pl.BlockSpec((tq, gw), lambda b, h, i: (b * n_q + i, h)),
        out_shape=jax.ShapeDtypeStruct((t, N_HEADS * HEAD_DIM), BF16),
        scratch_shapes=[pltpu.VMEM((Q_PER_KV * tq, HEAD_DIM), BF16)] + scratch,
        compiler_params=_params("parallel", "parallel", "parallel"),
        name=name,
    )(q, k, vt)


def _attention(q, k, vt, logit_bound, batch, seq, tq, tk, tk_online):
    m = Q_PER_KV * tq

    def unshifted(q, k, vt):
        return _attention_call(_attn_unshifted_kernel, [pltpu.VMEM((VT_ROWS, m), F32)],
                               "attention", q, k, vt, batch, seq, tq, tk)

    def online(q, k, vt):
        scratch = [pltpu.VMEM((m, 1), F32), pltpu.VMEM((m, 1), F32), pltpu.VMEM((m, HEAD_DIM), F32)]
        return _attention_call(_attn_online_kernel, scratch, "attention_online",
                               q, k, vt, batch, seq, tq, tk_online)

    return lax.cond(logit_bound <= MAX_UNSHIFTED_LOGIT, unshifted, online, q, k, vt)


def _conv_kernel(prev_ref, main_ref, next_ref, w_ref, b_ref, g_ref, beta_ref, o_ref, ext_sc, shift_sc, y_sc,
                 *, tc, n_blocks):
    i = pl.program_id(1)
    ch = main_ref.shape[1]
    prev = prev_ref[...].astype(F32)
    nxt = next_ref[...].astype(F32)
    ext_sc[0:CONV_HALO, :] = jnp.where(i == 0, 0.0, prev)
    ext_sc[CONV_HALO:CONV_HALO + tc, :] = main_ref[...].astype(F32)
    ext_sc[CONV_HALO + tc:, :] = jnp.where(i == n_blocks - 1, 0.0, nxt)
    rows = shift_sc.shape[1]
    for b in range(SUBLANES):
        shift_sc[b] = ext_sc[b:b + rows, :]
    base = CONV_HALO - CONV_WIDTH // 2
    for r0 in range(0, tc, CONV_ROWS):
        for c in range(ch // LANES):
            cols = slice(c * LANES, (c + 1) * LANES)
            acc = jnp.zeros((CONV_ROWS, LANES), F32) + b_ref[:, cols]
            for kk in range(CONV_WIDTH):
                a, b = divmod(base + kk, SUBLANES)
                row0 = a * SUBLANES + r0
                acc = acc + shift_sc[b, row0:row0 + CONV_ROWS, cols] * w_ref[kk:kk + 1, cols]
            y_sc[r0:r0 + CONV_ROWS, cols] = acc
    y = y_sc[...]
    mu = jnp.mean(y, axis=-1, keepdims=True)
    yc = y - mu
    var = jnp.mean(yc * yc, axis=-1, keepdims=True)
    z = yc * lax.rsqrt(var + EPS) * g_ref[...] + beta_ref[...]
    o_ref[...] = (z * _sigmoid(z)).astype(BF16)


def _conv_module(u, conv_w, conv_b, ln_g, ln_b, batch, seq, tc):
    t, ch = u.shape
    n_blocks = seq // tc
    hb = tc // CONV_HALO
    n_halo = seq // CONV_HALO
    kern = functools.partial(_conv_kernel, tc=tc, n_blocks=n_blocks)
    return pl.pallas_call(
        kern,
        grid=(batch, n_blocks),
        in_specs=[
            pl.BlockSpec((CONV_HALO, ch), lambda b, i: (b * n_halo + jnp.maximum(i * hb - 1, 0), 0)),
            pl.BlockSpec((tc, ch), lambda b, i: (b * n_blocks + i, 0)),
            pl.BlockSpec((CONV_HALO, ch), lambda b, i: (b * n_halo + jnp.minimum((i + 1) * hb, n_halo - 1), 0)),
            pl.BlockSpec((CONV_WIDTH, ch), lambda b, i: (0, 0)),
            pl.BlockSpec((1, ch), lambda b, i: (0, 0)),
            pl.BlockSpec((1, ch), lambda b, i: (0, 0)),
            pl.BlockSpec((1, ch), lambda b, i: (0, 0)),
        ],
        out_specs=pl.BlockSpec((tc, ch), lambda b, i: (b * n_blocks + i, 0)),
        out_shape=jax.ShapeDtypeStruct((t, ch), BF16),
        scratch_shapes=[
            pltpu.VMEM((tc + 2 * CONV_HALO, ch), F32),
            pltpu.VMEM((SUBLANES, tc + 2 * CONV_HALO - SUBLANES, ch), F32),
            pltpu.VMEM((tc, ch), F32),
        ],
        compiler_params=_params("parallel", "parallel"),
        name="conv_module",
    )(u, u, u, conv_w, conv_b, ln_g, ln_b)


def _resident(block, index_map):
    return pl.BlockSpec(block, index_map, pipeline_mode=pl.Buffered(1))


def _even_tail_kernel(h_ref, a_ref, c_ref, wa_ref, wc_ref, g_ref, wg_ref, wu_ref, wd_ref,
                      p_ref, pg_ref, wgate_ref, wproj_ref, o_ref):
    h = h_ref[...] + _dot(a_ref[...], wa_ref[...]) + _dot(c_ref[...], wc_ref[...])
    xn = _rms(h, g_ref[...]).astype(BF16)
    a = _dot(xn, wg_ref[...])
    b = _dot(xn, wu_ref[...])
    h = h + _dot((a * _sigmoid(a) * b).astype(BF16), wd_ref[...])
    o_ref[...] = _ple_update(h, p_ref, pg_ref, wgate_ref, wproj_ref)


def _even_tail(h, a, c, w_out, g, wg, wu, wd, p, pg, w_gate, w_proj, layer, ple_layer, tm):
    t, d = h.shape
    f_dim = wg.shape[-1]
    assert a.shape[1] == c.shape[1] and a.shape[1] + c.shape[1] == w_out.shape[1]
    return pl.pallas_call(
        _even_tail_kernel,
        grid=(t // tm,),
        in_specs=[
            pl.BlockSpec((tm, d), lambda i: (i, 0)),
            pl.BlockSpec((tm, a.shape[1]), lambda i: (i, 0)),
            pl.BlockSpec((tm, c.shape[1]), lambda i: (i, 0)),
            _resident((None, a.shape[1], d), lambda i: (layer, 0, 0)),
            _resident((None, c.shape[1], d), lambda i: (layer, 1, 0)),
            pl.BlockSpec((1, d), lambda i: (0, 0)),
            _resident((None, d, f_dim), lambda i: (layer, 0, 0)),
            _resident((None, d, f_dim), lambda i: (layer, 0, 0)),
            _resident((None, f_dim, d), lambda i: (layer, 0, 0)),
        ] + _ple_specs(p, w_gate, w_proj, ple_layer, tm, d, resident_weights=True),
        out_specs=pl.BlockSpec((tm, d), lambda i: (i, 0)),
        out_shape=jax.ShapeDtypeStruct((t, d), F32),
        compiler_params=_params("parallel"),
        name="out_proj_swiglu_ple",
    )(h, a, c, w_out, w_out, g, wg, wu, wd, p, pg, w_gate, w_proj)


def _pool_kernel(prev_ref, main_ref, next_ref, g_ref, w_ref, sc_ref, o_ref, *, tp, seq):
    i = pl.program_id(1)
    g = g_ref[...]
    h_main = main_ref[...]
    hn_main = _rms(h_main, g)
    ext = jnp.concatenate([_rms(prev_ref[...], g), hn_main, _rms(next_ref[...], g)], axis=0).astype(BF16)
    rows = tp + 2 * POOL_HALO
    t_pos = i * tp + lax.broadcasted_iota(jnp.int32, (tp, rows), 0)
    j_pos = i * tp - POOL_HALO + lax.broadcasted_iota(jnp.int32, (tp, rows), 1)
    in_seq = (j_pos >= 0) & (j_pos < seq)
    t_col = i * tp + lax.broadcasted_iota(jnp.int32, (tp, 1), 0)
    gc = w_ref.shape[1]
    for gi, win in enumerate(POOL_WINDOWS):
        left = win // 2
        right = win - 1 - left
        band = (in_seq & (j_pos >= t_pos - left) & (j_pos <= t_pos + right)).astype(BF16)
        cnt = jnp.minimum(t_col + right, seq - 1) - jnp.maximum(t_col - left, 0) + 1
        cols = slice(gi * gc, (gi + 1) * gc)
        win_sum = _dot(band, ext[:, cols])
        y = (win_sum / cnt.astype(F32) - hn_main[:, cols]).astype(BF16)
        o_ref[:, cols] = h_main[:, cols] + _dot(y, w_ref[gi]) * sc_ref[:, cols]


def _pool_mixer(h, g, pool_w, layer, pool_scale, batch, seq, tp):
    t, d = h.shape
    n_blocks = seq // tp
    hb = tp // POOL_HALO
    n_halo = seq // POOL_HALO
    kern = functools.partial(_pool_kernel, tp=tp, seq=seq)
    return pl.pallas_call(
        kern,
        grid=(batch, n_blocks),
        in_specs=[
            pl.BlockSpec((POOL_HALO, d), lambda b, i: (b * n_halo + jnp.maximum(i * hb - 1, 0), 0)),
            pl.BlockSpec((tp, d), lambda b, i: (b * n_blocks + i, 0)),
            pl.BlockSpec((POOL_HALO, d), lambda b, i: (b * n_halo + jnp.minimum((i + 1) * hb, n_halo - 1), 0)),
            pl.BlockSpec((1, d), lambda b, i: (0, 0)),
            pl.BlockSpec((None,) + pool_w.shape[1:], lambda b, i: (layer, 0, 0, 0)),
            pl.BlockSpec((1, d), lambda b, i: (0, 0)),
        ],
        out_specs=pl.BlockSpec((tp, d), lambda b, i: (b * n_blocks + i, 0)),
        out_shape=jax.ShapeDtypeStruct((t, d), F32),
        compiler_params=_params("parallel", "parallel"),
        name="pool_mixer",
    )(h, h, h, g, pool_w, pool_scale)


META_E1, META_E2, META_RANK1, META_RANK2, META_P1, META_P2 = range(6)


def _to_token_tiles(ref, x):
    tm, d = x.shape
    for a in range(d // LANES):
        ref[pl.ds(a, tm, stride=SUBLANES), :] = x[:, a * LANES:(a + 1) * LANES]


def _from_token_tiles(ref, tm, d):
    return [ref[pl.ds(a, tm, stride=SUBLANES), :] for a in range(d // LANES)]


def _router_kernel(h_ref, g_ref, wr_ref, xn_ref, meta_ref, counts_ref):
    @pl.when(pl.program_id(0) == 0)
    def _():
        counts_ref[...] = jnp.zeros(counts_ref.shape, F32)

    xn = _rms(h_ref[...], g_ref[...])
    _to_token_tiles(xn_ref, xn)
    logits = jnp.dot(xn, wr_ref[...], preferred_element_type=F32, precision=lax.Precision.HIGHEST)
    tm = logits.shape[0]
    lane = lax.broadcasted_iota(jnp.int32, logits.shape, 1)
    logits = jnp.where(lane < N_EXPERTS, logits, -jnp.inf)
    v1 = jnp.max(logits, axis=-1, keepdims=True)
    i1 = jnp.min(jnp.where(logits == v1, lane, LANES), axis=-1, keepdims=True)
    rest = jnp.where(lane == i1, -jnp.inf, logits)
    v2 = jnp.max(rest, axis=-1, keepdims=True)
    i2 = jnp.min(jnp.where(rest == v2, lane, LANES), axis=-1, keepdims=True)
    e2 = jnp.exp(v2 - v1)
    p1 = 1.0 / (1.0 + e2)
    p2 = e2 * p1
    chosen = (lane == i1) | (lane == i2)
    earlier = (lax.broadcasted_iota(jnp.int32, (tm, tm), 0) > lax.broadcasted_iota(jnp.int32, (tm, tm), 1))
    before = _dot(earlier.astype(BF16), chosen.astype(BF16)) + counts_ref[...]
    rank1 = jnp.sum(jnp.where(lane == i1, before, 0.0), axis=-1, keepdims=True)
    rank2 = jnp.sum(jnp.where(lane == i2, before, 0.0), axis=-1, keepdims=True)
    counts_ref[...] += jnp.sum(chosen.astype(F32), axis=0, keepdims=True)
    meta = jnp.zeros(logits.shape, F32)
    for col, val in ((META_E1, i1.astype(F32)), (META_E2, i2.astype(F32)), (META_RANK1, rank1),
                     (META_RANK2, rank2), (META_P1, p1), (META_P2, p2)):
        meta = jnp.where(lane == col, val, meta)
    meta_ref[...] = meta


def _router(h, g, wr_pad, tm):
    t, d = h.shape
    return pl.pallas_call(
        _router_kernel,
        grid=(t // tm,),
        in_specs=[
            pl.BlockSpec((tm, d), lambda i: (i, 0)),
            pl.BlockSpec((1, d), lambda i: (0, 0)),
            pl.BlockSpec(wr_pad.shape, lambda i: (0, 0)),
        ],
        out_specs=[
            pl.BlockSpec((tm * SUBLANES, LANES), lambda i: (i, 0)),
            pl.BlockSpec((tm, LANES), lambda i: (i, 0)),
            pl.BlockSpec((1, LANES), lambda i: (0, 0)),
        ],
        out_shape=[
            jax.ShapeDtypeStruct((t * SUBLANES, LANES), F32),
            jax.ShapeDtypeStruct((t, LANES), F32),
            jax.ShapeDtypeStruct((1, LANES), F32),
        ],
        compiler_params=_params("arbitrary"),
        name="router",
    )(h, g, wr_pad)


def _routing_tables(meta, counts, tile_rows, n_tiles):
    cnt = counts[0, :N_EXPERTS].astype(jnp.int32)
    padded = (cnt + tile_rows - 1) // tile_rows * tile_rows
    ends = jnp.cumsum(padded)
    starts = ends - padded
    pos1 = starts[meta[:, META_E1].astype(jnp.int32)] + meta[:, META_RANK1].astype(jnp.int32)
    pos2 = starts[meta[:, META_E2].astype(jnp.int32)] + meta[:, META_RANK2].astype(jnp.int32)
    n_used = ends[-1] // tile_rows
    tile_start = jnp.minimum(jnp.arange(n_tiles, dtype=jnp.int32), n_used - 1) * tile_rows
    tile_expert = jnp.sum((tile_start[:, None] >= ends[None, :]).astype(jnp.int32), axis=-1)
    n_free = n_tiles * tile_rows - TOP_K * meta.shape[0]
    pad_ends = jnp.cumsum(padded - cnt)
    slot = jnp.arange(n_free, dtype=jnp.int32)
    owner = jnp.minimum(jnp.sum((slot[:, None] >= pad_ends[None, :]).astype(jnp.int32), axis=-1), N_EXPERTS - 1)
    free_pos = jnp.where(slot < pad_ends[-1], ends[owner] - (pad_ends[owner] - slot),
                         ends[-1] + (slot - pad_ends[-1]))
    return pos1, pos2, tile_expert, n_used.reshape(1), free_pos


def _token_rows(ref, i, n=1):
    return ref.at[pl.ds(pl.multiple_of(i * SUBLANES, SUBLANES), n * SUBLANES)]


def _dispatch_kernel(pos1_ref, pos2_ref, free_pos_ref, xn_ref, xs_hbm, zero_sc, sem, zero_sem,
                     *, chunk, free_chunk):
    step = pl.program_id(0)
    base = step * chunk
    zero_sc[...] = jnp.zeros(zero_sc.shape, F32)

    def zero_row(i, carry):
        dst = _token_rows(xs_hbm, free_pos_ref[step * free_chunk + i])
        pltpu.make_async_copy(_token_rows(zero_sc, i), dst, zero_sem).start()
        return carry

    lax.fori_loop(0, free_chunk, zero_row, 0, unroll=8)

    def issue(i, carry):
        src = _token_rows(xn_ref, i)
        pltpu.make_async_copy(src, _token_rows(xs_hbm, pos1_ref[base + i]), sem).start()
        pltpu.make_async_copy(src, _token_rows(xs_hbm, pos2_ref[base + i]), sem).start()
        return carry

    lax.fori_loop(0, chunk, issue, 0, unroll=8)
    for _ in range(TOP_K):
        pltpu.make_async_copy(xn_ref, _token_rows(xs_hbm, 0, chunk), sem).wait()
    pltpu.make_async_copy(zero_sc, _token_rows(xs_hbm, 0, free_chunk), zero_sem).wait()


def _dispatch(pos1, pos2, free_pos, xn_tiles, n_rows, chunk):
    t = pos1.shape[0]
    n_steps = t // chunk
    assert free_pos.shape[0] == n_rows - TOP_K * t and free_pos.shape[0] % n_steps == 0
    free_chunk = free_pos.shape[0] // n_steps
    return pl.pallas_call(
        functools.partial(_dispatch_kernel, chunk=chunk, free_chunk=free_chunk),
        grid_spec=pltpu.PrefetchScalarGridSpec(
            num_scalar_prefetch=3,
            grid=(n_steps,),
            in_specs=[pl.BlockSpec((chunk * SUBLANES, LANES), lambda i, *prefetch: (i, 0))],
            out_specs=pl.BlockSpec(memory_space=pl.ANY),
            scratch_shapes=[pltpu.VMEM((free_chunk * SUBLANES, LANES), F32), pltpu.SemaphoreType.DMA,
                            pltpu.SemaphoreType.DMA],
        ),
        out_shape=jax.ShapeDtypeStruct((n_rows * SUBLANES, LANES), F32),
        compiler_params=_params("arbitrary"),
        name="moe_dispatch",
    )(pos1, pos2, free_pos, xn_tiles)


def _expert_kernel(tile_expert_ref, n_used_ref, xs_ref, wg_ref, wu_ref, wd_ref, ys_ref, acc_sc, *, tm, d):
    del tile_expert_ref
    r = pl.program_id(0)
    f = pl.program_id(1)
    last = pl.num_programs(1) - 1
    used = r < n_used_ref[0]

    @pl.when(used)
    def _():
        x = jnp.concatenate(_from_token_tiles(xs_ref, tm, d), axis=-1).astype(BF16)
        a = _dot(x, wg_ref[...])
        b = _dot(x, wu_ref[...])
        y = _dot((a * _sigmoid(a) * b).astype(BF16), wd_ref[...])

        @pl.when(f == 0)
        def _():
            acc_sc[...] = y

        @pl.when((f > 0) & (f < last))
        def _():
            acc_sc[...] += y

        @pl.when(f == last)
        def _():
            _to_token_tiles(ys_ref, acc_sc[...] + y)

    @pl.when(jnp.logical_not(used) & (f == last))
    def _():
        ys_ref[...] = jnp.zeros(ys_ref.shape, F32)


def _experts(tile_expert, n_used, xs, wg, wu, wd, layer, tm, tf):
    _, n_e, d, f_dim = wg.shape
    n_tiles = xs.shape[0] // (tm * SUBLANES)
    assert f_dim // tf >= 2

    def row_tile(r, f, te, nu):
        return (jnp.minimum(r, nu[0] - 1), 0)

    return pl.pallas_call(
        functools.partial(_expert_kernel, tm=tm, d=d),
        grid_spec=pltpu.PrefetchScalarGridSpec(
            num_scalar_prefetch=2,
            grid=(n_tiles, f_dim // tf),
            in_specs=[
                pl.BlockSpec((tm * SUBLANES, LANES), row_tile),
                pl.BlockSpec((None, None, d, tf), lambda r, f, te, nu: (layer, te[r], 0, f)),
                pl.BlockSpec((None, None, d, tf), lambda r, f, te, nu: (layer, te[r], 0, f)),
                pl.BlockSpec((None, None, tf, d), lambda r, f, te, nu: (layer, te[r], f, 0)),
            ],
            out_specs=pl.BlockSpec((tm * SUBLANES, LANES), lambda r, f, te, nu: (r, 0)),
            scratch_shapes=[pltpu.VMEM((tm, d), F32)],
        ),
        out_shape=jax.ShapeDtypeStruct(xs.shape, F32),
        compiler_params=_params("arbitrary", "arbitrary"),
        name="moe_experts",
    )(tile_expert, n_used, xs, wg, wu, wd)


def _combine_kernel(pos1_ref, pos2_ref, h_ref, meta_ref, ys_hbm, p_ref, g_ref, wgate_ref, wproj_ref,
                    o_ref, y_sc, sems, *, tm, d):
    i = pl.program_id(0)
    slot = i % 2

    def gather(tile, slot):
        base = tile * tm

        def issue(r, carry):
            for k, pos_ref in enumerate((pos1_ref, pos2_ref)):
                pltpu.make_async_copy(_token_rows(ys_hbm, pos_ref[base + r]),
                                      _token_rows(y_sc.at[TOP_K * slot + k], r), sems.at[slot]).start()
            return carry

        lax.fori_loop(0, tm, issue, 0, unroll=8)

    @pl.when(i == 0)
    def _():
        gather(0, 0)

    @pl.when(i + 1 < pl.num_programs(0))
    def _():
        gather(i + 1, 1 - slot)

    for k in range(TOP_K):
        pltpu.make_async_copy(_token_rows(ys_hbm, 0, tm), y_sc.at[TOP_K * slot + k], sems.at[slot]).wait()
    meta = meta_ref[...]
    p1 = meta[:, META_P1:META_P1 + 1]
    p2 = meta[:, META_P2:META_P2 + 1]
    y1 = _from_token_tiles(y_sc.at[TOP_K * slot], tm, d)
    y2 = _from_token_tiles(y_sc.at[TOP_K * slot + 1], tm, d)
    for a in range(d // LANES):
        cols = slice(a * LANES, (a + 1) * LANES)
        o_ref[:, cols] = h_ref[:, cols] + p1 * y1[a] + p2 * y2[a]
    o_ref[...] = _ple_update(o_ref[...], p_ref, g_ref, wgate_ref, wproj_ref)


def _combine(pos1, pos2, h, meta, ys, p, g, w_gate, w_proj, layer, tm):
    t, d = h.shape
    return pl.pallas_call(
        functools.partial(_combine_kernel, tm=tm, d=d),
        grid_spec=pltpu.PrefetchScalarGridSpec(
            num_scalar_prefetch=2,
            grid=(t // tm,),
            in_specs=[
                pl.BlockSpec((tm, d), lambda i, p1, p2: (i, 0)),
                pl.BlockSpec((tm, LANES), lambda i, p1, p2: (i, 0)),
                pl.BlockSpec(memory_space=pl.ANY),
            ] + _ple_specs(p, w_gate, w_proj, layer, tm, d),
            out_specs=pl.BlockSpec((tm, d), lambda i, p1, p2: (i, 0)),
            scratch_shapes=[pltpu.VMEM((2 * TOP_K, tm * SUBLANES, LANES), F32),
                            pltpu.SemaphoreType.DMA((2,))],
        ),
        out_shape=jax.ShapeDtypeStruct((t, d), F32),
        compiler_params=_params("arbitrary"),
        name="moe_combine",
    )(pos1, pos2, h, meta, ys, p, g, w_gate, w_proj)


def _ple_specs(p, w_gate, w_proj, layer, tm, d, resident_weights=False):
    def spec(block, index, make=pl.BlockSpec):
        return make(block, lambda i, *prefetch: index(i))
    weight = functools.partial(spec, make=_resident) if resident_weights else spec
    return [
        spec((None, tm, p.shape[-1]), lambda i: (layer, i, 0)),
        spec((1, d), lambda i: (0, 0)),
        weight((None,) + w_gate.shape[1:], lambda i: (layer, 0, 0)),
        weight((None,) + w_proj.shape[1:], lambda i: (layer, 0, 0)),
    ]


def _rope_tables(seq):
    rows = seq // GRID_W
    r = jnp.broadcast_to(jnp.arange(rows, dtype=F32)[:, None], (rows, GRID_W)).reshape(seq)
    c = jnp.broadcast_to(jnp.arange(GRID_W, dtype=F32)[None, :], (rows, GRID_W)).reshape(seq)
    inv = ROPE_THETA ** (-jnp.arange(0, AXIS_DIM, 2, dtype=F32) / AXIS_DIM)
    ang = jnp.concatenate([r[:, None] * inv, c[:, None] * inv], axis=-1)
    cos, sin = jnp.cos(ang), jnp.sin(ang)
    reps = LANES // HEAD_DIM
    return (jnp.tile(jnp.concatenate([cos, cos], axis=-1), (1, reps)),
            jnp.tile(jnp.concatenate([-sin, sin], axis=-1), (1, reps)))


def _tile(n, want):
    t = min(n, want)
    assert n % t == 0, (n, t)
    return t


def kernel(x, p, norm_mix, norm_ffn, w_in, q_norm, k_norm, conv_w, conv_b, conv_ln_g, conv_ln_b, w_out,
           ffn_wg, ffn_wu, ffn_wd, pool_w, pool_scale, router_w, moe_wg, moe_wu, moe_wd, ple_norm,
           ple_gate_w, ple_proj):
    batch, seq, d = x.shape
    depth = p.shape[0]
    t = batch * seq
    q_dim = N_HEADS * HEAD_DIM
    assert seq % GRID_W == 0 and d % LANES == 0

    tm = _tile(seq, 512)
    tq = _tile(seq, 256)
    tk = _tile(seq, 8192)
    tk_online = _tile(seq, 512)
    tc = _tile(seq, 256)
    tp = _tile(seq, 256)
    tf_moe = moe_wg.shape[3] // 2
    tm_moe = _tile(t, 1024)
    n_moe_tiles = TOP_K * t // tm_moe + N_EXPERTS
    dispatch_chunk = _tile(t, 2048)

    cos_t, sin_t = _rope_tables(seq)
    row = lambda v: v.reshape(1, -1)
    tile_heads = lambda v: jnp.tile(v, LANES // HEAD_DIM).reshape(1, LANES)

    bf = lambda w: w.astype(BF16)
    w_in_b, w_out_b = bf(w_in), bf(w_out)
    ffn_wg_b, ffn_wu_b, ffn_wd_b = bf(ffn_wg), bf(ffn_wu), bf(ffn_wd)
    pool_w_b = bf(pool_w)
    moe_wg_b, moe_wu_b, moe_wd_b = bf(moe_wg), bf(moe_wu), bf(moe_wd)
    ple_gate_b, ple_proj_b = bf(ple_gate_w), bf(ple_proj)
    p_rows = p.reshape(depth, t, -1)

    h = x.reshape(t, d)
    for i in range(depth):
        j = i // 2
        ple_args = (p_rows, row(ple_norm[i]), ple_gate_b, ple_proj_b, i)
        if i % 2 == 0:
            q, k, vt, u = _in_proj(h, row(norm_mix[i]), w_in_b, j, tile_heads(q_norm[j]),
                                   tile_heads(k_norm[j]), cos_t, sin_t, seq, tm)
            logit_bound = (HEAD_DIM ** 0.5 * LOG2E) * jnp.max(jnp.abs(q_norm[j])) * jnp.max(jnp.abs(k_norm[j]))
            a = _attention(q, k, vt, logit_bound, batch, seq, tq, tk, tk_online)
            c = _conv_module(u, conv_w[j], row(conv_b[j]), row(conv_ln_g[j]), row(conv_ln_b[j]),
                             batch, seq, tc)
            h = _even_tail(h, a, c, w_out_b, row(norm_ffn[i]), ffn_wg_b, ffn_wu_b, ffn_wd_b,
                           p_rows, row(ple_norm[i]), ple_gate_b, ple_proj_b, j, i, tm)
        else:
            h = _pool_mixer(h, row(norm_mix[i]), pool_w_b, j, row(pool_scale[j]), batch, seq, tp)
            wr_pad = jnp.pad(router_w[j], ((0, 0), (0, LANES - N_EXPERTS)))
            xn_tiles, meta, counts = _router(h, row(norm_ffn[i]), wr_pad, tm)
            pos1, pos2, tile_expert, n_used, free_pos = _routing_tables(meta, counts, tm_moe, n_moe_tiles)
            xs = _dispatch(pos1, pos2, free_pos, xn_tiles, n_moe_tiles * tm_moe, dispatch_chunk)
            ys = _experts(tile_expert, n_used, xs, moe_wg_b, moe_wu_b, moe_wd_b, j, tm_moe, tf_moe)
            h = _combine(pos1, pos2, h, meta, ys, *ple_args, tm)
    return h.reshape(batch, seq, d)
```

```python
import functools

import jax
import jax.numpy as jnp
from jax import lax
from jax.experimental import pallas as pl
from jax.experimental.pallas import tpu as pltpu

F32 = jnp.float32
BF16 = jnp.bfloat16

GRID_W = 64
N_HEADS = 8
KV_HEADS = 2
HEAD_DIM = 64
Q_PER_KV = N_HEADS // KV_HEADS
AXIS_DIM = HEAD_DIM // 2
ROPE_THETA = 10000.0
CONV_WIDTH = 31
POOL_WINDOWS = (2, 4, 8, 16)
N_EXPERTS = 8
TOP_K = 2
EPS = 1e-6

LANES = 128
SUBLANES = 8
VMEM_LIMIT = 56 * 1024 * 1024
BF16_SUBLANES = 16
VT_ROWS = HEAD_DIM + BF16_SUBLANES
LOG2E = 1.4426950408889634
Q_SCALE = HEAD_DIM ** -0.5 * LOG2E
MAX_UNSHIFTED_LOGIT = 80.0
CONV_HALO = 16
CONV_ROWS = 64
POOL_HALO = 8


def _params(*sem):
    return pltpu.CompilerParams(dimension_semantics=sem, vmem_limit_bytes=VMEM_LIMIT)


def _rms(x, g):
    return x * lax.rsqrt(jnp.mean(x * x, axis=-1, keepdims=True) + EPS) * g


def _sigmoid(x):
    return 1.0 / (1.0 + jnp.exp(-x))


def _dot(a, b):
    return jnp.dot(a, b, preferred_element_type=F32)


def _ple_update(h, p_ref, g_ref, wgate_ref, wproj_ref):
    gate = _sigmoid(_dot(_rms(h, g_ref[...]).astype(BF16), wgate_ref[...]))
    return h + gate * _dot(p_ref[...].astype(BF16), wproj_ref[...])


def _inproj_kernel(h_ref, g_ref, w_ref, qg_ref, kg_ref, cos_ref, sin_ref,
                   q_ref, k_ref, vt_ref, u_ref, *, q_dim, kv_dim, conv_ch):
    xn = _rms(h_ref[...], g_ref[...]).astype(BF16)
    proj = _dot(xn, w_ref[...])
    tm = proj.shape[0]
    cos = cos_ref[...]
    sin = sin_ref[...]
    lane = lax.broadcasted_iota(jnp.int32, (tm, LANES), 1)
    head0 = lane < HEAD_DIM
    first_half = (lane % HEAD_DIM) < (HEAD_DIM // 2)

    def norm_rope(x, g, scale):
        sq = x * x
        s0 = jnp.sum(jnp.where(head0, sq, 0.0), axis=-1, keepdims=True)
        s1 = jnp.sum(jnp.where(head0, 0.0, sq), axis=-1, keepdims=True)
        ms = jnp.where(head0, s0, s1) * (1.0 / HEAD_DIM)
        y = x * lax.rsqrt(ms + EPS) * g
        partner = jnp.where(first_half,
                            pltpu.roll(y, LANES - HEAD_DIM // 2, 1),
                            pltpu.roll(y, HEAD_DIM // 2, 1))
        return (y * cos + partner * sin) * scale

    for c in range(q_dim // LANES):
        x = proj[:, c * LANES:(c + 1) * LANES]
        q_ref[:, c * LANES:(c + 1) * LANES] = norm_rope(x, qg_ref[...], Q_SCALE).astype(BF16)
    sub = lax.broadcasted_iota(jnp.int32, (VT_ROWS - HEAD_DIM, tm), 0)
    ones_rows = jnp.where(sub == 0, 1.0, 0.0).astype(BF16)
    for c in range(kv_dim // LANES):
        x = proj[:, q_dim + c * LANES:q_dim + (c + 1) * LANES]
        kk = norm_rope(x, kg_ref[...], 1.0).astype(BF16)
        vv_t = proj[:, q_dim + kv_dim + c * LANES:q_dim + kv_dim + (c + 1) * LANES].T
        for j in range(LANES // HEAD_DIM):
            head = c * (LANES // HEAD_DIM) + j
            k_ref[head] = kk[:, j * HEAD_DIM:(j + 1) * HEAD_DIM]
            vt_ref[head, 0:HEAD_DIM, :] = vv_t[j * HEAD_DIM:(j + 1) * HEAD_DIM, :].astype(BF16)
            vt_ref[head, HEAD_DIM:VT_ROWS, :] = ones_rows
    u0 = q_dim + 2 * kv_dim
    u_ref[...] = (proj[:, u0:u0 + conv_ch] * _sigmoid(proj[:, u0 + conv_ch:u0 + 2 * conv_ch])).astype(BF16)


def _in_proj(h, g, w_in, layer, qg, kg, cos_t, sin_t, seq, tm):
    t, d = h.shape
    w_in_dim = w_in.shape[-1]
    q_dim = N_HEADS * HEAD_DIM
    kv_dim = KV_HEADS * HEAD_DIM
    conv_ch = (w_in_dim - q_dim - 2 * kv_dim) // 2
    n_seq_blocks = seq // tm
    kern = functools.partial(_inproj_kernel, q_dim=q_dim, kv_dim=kv_dim, conv_ch=conv_ch)
    return pl.pallas_call(
        kern,
        grid=(t // tm,),
        in_specs=[
            pl.BlockSpec((tm, d), lambda i: (i, 0)),
            pl.BlockSpec((1, d), lambda i: (0, 0)),
            pl.BlockSpec((None, d, w_in_dim), lambda i: (layer, 0, 0)),
            pl.BlockSpec((1, LANES), lambda i: (0, 0)),
            pl.BlockSpec((1, LANES), lambda i: (0, 0)),
            pl.BlockSpec((tm, LANES), lambda i: (i % n_seq_blocks, 0)),
            pl.BlockSpec((tm, LANES), lambda i: (i % n_seq_blocks, 0)),
        ],
        out_specs=[
            pl.BlockSpec((tm, q_dim), lambda i: (i, 0)),
            pl.BlockSpec((KV_HEADS, tm, HEAD_DIM), lambda i: (0, i, 0)),
            pl.BlockSpec((KV_HEADS, VT_ROWS, tm), lambda i: (0, 0, i)),
            pl.BlockSpec((tm, conv_ch), lambda i: (i, 0)),
        ],
        out_shape=[
            jax.ShapeDtypeStruct((t, q_dim), BF16),
            jax.ShapeDtypeStruct((KV_HEADS, t, HEAD_DIM), BF16),
            jax.ShapeDtypeStruct((KV_HEADS, VT_ROWS, t), BF16),
            jax.ShapeDtypeStruct((t, conv_ch), BF16),
        ],
        compiler_params=_params("parallel"),
        name="in_proj",
    )(h, g, w_in, qg, kg, cos_t, sin_t)


def _stack_query_heads(q_ref, q_sc, tq):
    for g in range(Q_PER_KV):
        q_sc[g * tq:(g + 1) * tq, :] = q_ref[:, g * HEAD_DIM:(g + 1) * HEAD_DIM]


def _unstack_query_heads(out, tq):
    return jnp.concatenate([out[g * tq:(g + 1) * tq, :] for g in range(Q_PER_KV)], axis=-1).astype(BF16)


_NT = (((1,), (1,)), ((), ()))


def _attn_unshifted_kernel(q_ref, k_ref, vt_ref, o_ref, q_sc, acc_sc, *, tq, tk, n_kv):
    _stack_query_heads(q_ref, q_sc, tq)
    acc_sc[...] = jnp.zeros(acc_sc.shape, F32)

    def body(j, carry):
        kv0 = pl.multiple_of(j * tk, tk)
        s_t = lax.dot_general(k_ref[0, pl.ds(kv0, tk), :], q_sc[...], _NT,
                              preferred_element_type=F32)
        p_t = jnp.exp2(s_t).astype(BF16)
        acc_sc[...] += _dot(vt_ref[0, :, pl.ds(kv0, tk)], p_t)
        return carry

    lax.fori_loop(0, n_kv, body, 0)
    acc = acc_sc[...]
    out_t = acc[:HEAD_DIM, :] / acc[HEAD_DIM:HEAD_DIM + 1, :]
    o_ref[...] = _unstack_query_heads(out_t.T, tq)


def _attn_online_kernel(q_ref, k_ref, vt_ref, o_ref, q_sc, m_sc, l_sc, acc_sc, *, tq, tk, n_kv):
    _stack_query_heads(q_ref, q_sc, tq)
    m_sc[...] = jnp.full(m_sc.shape, -jnp.inf, F32)
    l_sc[...] = jnp.zeros(l_sc.shape, F32)
    acc_sc[...] = jnp.zeros(acc_sc.shape, F32)

    def body(j, carry):
        kv0 = pl.multiple_of(j * tk, tk)
        s = lax.dot_general(q_sc[...], k_ref[0, pl.ds(kv0, tk), :], _NT, preferred_element_type=F32)
        m_prev = m_sc[...]
        m_new = jnp.maximum(m_prev, jnp.max(s, axis=-1, keepdims=True))
        alpha = jnp.exp2(m_prev - m_new)
        p = jnp.exp2(s - m_new)
        l_sc[...] = alpha * l_sc[...] + jnp.sum(p, axis=-1, keepdims=True)
        v_t = vt_ref[0, 0:HEAD_DIM, pl.ds(kv0, tk)]
        acc_sc[...] = alpha * acc_sc[...] + lax.dot_general(p.astype(BF16), v_t, _NT,
                                                            preferred_element_type=F32)
        m_sc[...] = m_new
        return carry

    lax.fori_loop(0, n_kv, body, 0)
    o_ref[...] = _unstack_query_heads(acc_sc[...] / l_sc[...], tq)


def _attention_call(kern, scratch, name, q, k, vt, batch, seq, tq, tk):
    t = q.shape[0]
    n_q = seq // tq
    gw = Q_PER_KV * HEAD_DIM
    return pl.pallas_call(
        functools.partial(kern, tq=tq, tk=tk, n_kv=seq // tk),
        grid=(batch, KV_HEADS, n_q),
        in_specs=[
            pl.BlockSpec((tq, gw), lambda b, h, i: (b * n_q + i, h)),
            pl.BlockSpec((1, seq, HEAD_DIM), lambda b, h, i: (h, b, 0)),
            pl.BlockSpec((1, VT_ROWS, seq), lambda b, h, i: (h, 0, b)),
        ],
        out_specs=pl.BlockSpec((tq, gw), lambda b, h, i: (b * n_q + i, h)),
        out_shape=jax.ShapeDtypeStruct((t, N_HEADS * HEAD_DIM), BF16),
        scratch_shapes=[pltpu.VMEM((Q_PER_KV * tq, HEAD_DIM), BF16)] + scratch,
        compiler_params=_params("parallel", "parallel", "parallel"),
        name=name,
    )(q, k, vt)


def _attention(q, k, vt, logit_bound, batch, seq, tq, tk, tk_online):
    m = Q_PER_KV * tq

    def unshifted(q, k, vt):
        return _attention_call(_attn_unshifted_kernel, [pltpu.VMEM((VT_ROWS, m), F32)],
                               "attention", q, k, vt, batch, seq, tq, tk)

    def online(q, k, vt):
        scratch = [pltpu.VMEM((m, 1), F32), pltpu.VMEM((m, 1), F32), pltpu.VMEM((m, HEAD_DIM), F32)]
        return _attention_call(_attn_online_kernel, scratch, "attention_online",
                               q, k, vt, batch, seq, tq, tk_online)

    return lax.cond(logit_bound <= MAX_UNSHIFTED_LOGIT, unshifted, online, q, k, vt)


def _conv_kernel(prev_ref, main_ref, next_ref, w_ref, b_ref, g_ref, beta_ref, o_ref, ext_sc, shift_sc, y_sc,
                 *, tc, n_blocks):
    i = pl.program_id(1)
    ch = main_ref.shape[1]
    prev = prev_ref[...].astype(F32)
    nxt = next_ref[...].astype(F32)
    ext_sc[0:CONV_HALO, :] = jnp.where(i == 0, 0.0, prev)
    ext_sc[CONV_HALO:CONV_HALO + tc, :] = main_ref[...].astype(F32)
    ext_sc[CONV_HALO + tc:, :] = jnp.where(i == n_blocks - 1, 0.0, nxt)
    rows = shift_sc.shape[1]
    for b in range(SUBLANES):
        shift_sc[b] = ext_sc[b:b + rows, :]
    base = CONV_HALO - CONV_WIDTH // 2
    for r0 in range(0, tc, CONV_ROWS):
        for c in range(ch // LANES):
            cols = slice(c * LANES, (c + 1) * LANES)
            acc = jnp.zeros((CONV_ROWS, LANES), F32) + b_ref[:, cols]
            for kk in range(CONV_WIDTH):
                a, b = divmod(base + kk, SUBLANES)
                row0 = a * SUBLANES + r0
                acc = acc + shift_sc[b, row0:row0 + CONV_ROWS, cols] * w_ref[kk:kk + 1, cols]
            y_sc[r0:r0 + CONV_ROWS, cols] = acc
    y = y_sc[...]
    mu = jnp.mean(y, axis=-1, keepdims=True)
    yc = y - mu
    var = jnp.mean(yc * yc, axis=-1, keepdims=True)
    z = yc * lax.rsqrt(var + EPS) * g_ref[...] + beta_ref[...]
    o_ref[...] = (z * _sigmoid(z)).astype(BF16)


def _conv_module(u, conv_w, conv_b, ln_g, ln_b, batch, seq, tc):
    t, ch = u.shape
    n_blocks = seq // tc
    hb = tc // CONV_HALO
    n_halo = seq // CONV_HALO
    kern = functools.partial(_conv_kernel, tc=tc, n_blocks=n_blocks)
    return pl.pallas_call(
        kern,
        grid=(batch, n_blocks),
        in_specs=[
            pl.BlockSpec((CONV_HALO, ch), lambda b, i: (b * n_halo + jnp.maximum(i * hb - 1, 0), 0)),
            pl.BlockSpec((tc, ch), lambda b, i: (b * n_blocks + i, 0)),
            pl.BlockSpec((CONV_HALO, ch), lambda b, i: (b * n_halo + jnp.minimum((i + 1) * hb, n_halo - 1), 0)),
            pl.BlockSpec((CONV_WIDTH, ch), lambda b, i: (0, 0)),
            pl.BlockSpec((1, ch), lambda b, i: (0, 0)),
            pl.BlockSpec((1, ch), lambda b, i: (0, 0)),
            pl.BlockSpec((1, ch), lambda b, i: (0, 0)),
        ],
        out_specs=pl.BlockSpec((tc, ch), lambda b, i: (b * n_blocks + i, 0)),
        out_shape=jax.ShapeDtypeStruct((t, ch), BF16),
        scratch_shapes=[
            pltpu.VMEM((tc + 2 * CONV_HALO, ch), F32),
            pltpu.VMEM((SUBLANES, tc + 2 * CONV_HALO - SUBLANES, ch), F32),
            pltpu.VMEM((tc, ch), F32),
        ],
        compiler_params=_params("parallel", "parallel"),
        name="conv_module",
    )(u, u, u, conv_w, conv_b, ln_g, ln_b)


def _resident(block, index_map):
    return pl.BlockSpec(block, index_map, pipeline_mode=pl.Buffered(1))


def _even_tail_kernel(h_ref, a_ref, c_ref, wa_ref, wc_ref, g_ref, wg_ref, wu_ref, wd_ref,
                      p_ref, pg_ref, wgate_ref, wproj_ref, o_ref):
    h = h_ref[...] + _dot(a_ref[...], wa_ref[...]) + _dot(c_ref[...], wc_ref[...])
    xn = _rms(h, g_ref[...]).astype(BF16)
    a = _dot(xn, wg_ref[...])
    b = _dot(xn, wu_ref[...])
    h = h + _dot((a * _sigmoid(a) * b).astype(BF16), wd_ref[...])
    o_ref[...] = _ple_update(h, p_ref, pg_ref, wgate_ref, wproj_ref)


def _even_tail(h, a, c, w_out, g, wg, wu, wd, p, pg, w_gate, w_proj, layer, ple_layer, tm):
    t, d = h.shape
    f_dim = wg.shape[-1]
    assert a.shape[1] == c.shape[1] and a.shape[1] + c.shape[1] == w_out.shape[1]
    return pl.pallas_call(
        _even_tail_kernel,
        grid=(t // tm,),
        in_specs=[
            pl.BlockSpec((tm, d), lambda i: (i, 0)),
            pl.BlockSpec((tm, a.shape[1]), lambda i: (i, 0)),
            pl.BlockSpec((tm, c.shape[1]), lambda i: (i, 0)),
            _resident((None, a.shape[1], d), lambda i: (layer, 0, 0)),
            _resident((None, c.shape[1], d), lambda i: (layer, 1, 0)),
            pl.BlockSpec((1, d), lambda i: (0, 0)),
            _resident((None, d, f_dim), lambda i: (layer, 0, 0)),
            _resident((None, d, f_dim), lambda i: (layer, 0, 0)),
            _resident((None, f_dim, d), lambda i: (layer, 0, 0)),
        ] + _ple_specs(p, w_gate, w_proj, ple_layer, tm, d, resident_weights=True),
        out_specs=pl.BlockSpec((tm, d), lambda i: (i, 0)),
        out_shape=jax.ShapeDtypeStruct((t, d), F32),
        compiler_params=_params("parallel"),
        name="out_proj_swiglu_ple",
    )(h, a, c, w_out, w_out, g, wg, wu, wd, p, pg, w_gate, w_proj)


def _pool_kernel(prev_ref, main_ref, next_ref, g_ref, w_ref, sc_ref, o_ref, *, tp, seq):
    i = pl.program_id(1)
    g = g_ref[...]
    h_main = main_ref[...]
    hn_main = _rms(h_main, g)
    ext = jnp.concatenate([_rms(prev_ref[...], g), hn_main, _rms(next_ref[...], g)], axis=0).astype(BF16)
    rows = tp + 2 * POOL_HALO
    t_pos = i * tp + lax.broadcasted_iota(jnp.int32, (tp, rows), 0)
    j_pos = i * tp - POOL_HALO + lax.broadcasted_iota(jnp.int32, (tp, rows), 1)
    in_seq = (j_pos >= 0) & (j_pos < seq)
    t_col = i * tp + lax.broadcasted_iota(jnp.int32, (tp, 1), 0)
    gc = w_ref.shape[1]
    for gi, win in enumerate(POOL_WINDOWS):
        left = win // 2
        right = win - 1 - left
        band = (in_seq & (j_pos >= t_pos - left) & (j_pos <= t_pos + right)).astype(BF16)
        cnt = jnp.minimum(t_col + right, seq - 1) - jnp.maximum(t_col - left, 0) + 1
        cols = slice(gi * gc, (gi + 1) * gc)
        win_sum = _dot(band, ext[:, cols])
        y = (win_sum / cnt.astype(F32) - hn_main[:, cols]).astype(BF16)
        o_ref[:, cols] = h_main[:, cols] + _dot(y, w_ref[gi]) * sc_ref[:, cols]


def _pool_mixer(h, g, pool_w, layer, pool_scale, batch, seq, tp):
    t, d = h.shape
    n_blocks = seq // tp
    hb = tp // POOL_HALO
    n_halo = seq // POOL_HALO
    kern = functools.partial(_pool_kernel, tp=tp, seq=seq)
    return pl.pallas_call(
        kern,
        grid=(batch, n_blocks),
        in_specs=[
            pl.BlockSpec((POOL_HALO, d), lambda b, i: (b * n_halo + jnp.maximum(i * hb - 1, 0), 0)),
            pl.BlockSpec((tp, d), lambda b, i: (b * n_blocks + i, 0)),
            pl.BlockSpec((POOL_HALO, d), lambda b, i: (b * n_halo + jnp.minimum((i + 1) * hb, n_halo - 1), 0)),
            pl.BlockSpec((1, d), lambda b, i: (0, 0)),
            pl.BlockSpec((None,) + pool_w.shape[1:], lambda b, i: (layer, 0, 0, 0)),
            pl.BlockSpec((1, d), lambda b, i: (0, 0)),
        ],
        out_specs=pl.BlockSpec((tp, d), lambda b, i: (b * n_blocks + i, 0)),
        out_shape=jax.ShapeDtypeStruct((t, d), F32),
        compiler_params=_params("parallel", "parallel"),
        name="pool_mixer",
    )(h, h, h, g, pool_w, pool_scale)


META_E1, META_E2, META_RANK1, META_RANK2, META_P1, META_P2 = range(6)


def _to_token_tiles(ref, x):
    tm, d = x.shape
    for a in range(d // LANES):
        ref[pl.ds(a, tm, stride=SUBLANES), :] = x[:, a * LANES:(a + 1) * LANES]


def _from_token_tiles(ref, tm, d):
    return [ref[pl.ds(a, tm, stride=SUBLANES), :] for a in range(d // LANES)]


def _router_kernel(h_ref, g_ref, wr_ref, xn_ref, meta_ref, counts_ref):
    @pl.when(pl.program_id(0) == 0)
    def _():
        counts_ref[...] = jnp.zeros(counts_ref.shape, F32)

    xn = _rms(h_ref[...], g_ref[...])
    _to_token_tiles(xn_ref, xn)
    logits = jnp.dot(xn, wr_ref[...], preferred_element_type=F32, precision=lax.Precision.HIGHEST)
    tm = logits.shape[0]
    lane = lax.broadcasted_iota(jnp.int32, logits.shape, 1)
    logits = jnp.where(lane < N_EXPERTS, logits, -jnp.inf)
    v1 = jnp.max(logits, axis=-1, keepdims=True)
    i1 = jnp.min(jnp.where(logits == v1, lane, LANES), axis=-1, keepdims=True)
    rest = jnp.where(lane == i1, -jnp.inf, logits)
    v2 = jnp.max(rest, axis=-1, keepdims=True)
    i2 = jnp.min(jnp.where(rest == v2, lane, LANES), axis=-1, keepdims=True)
    e2 = jnp.exp(v2 - v1)
    p1 = 1.0 / (1.0 + e2)
    p2 = e2 * p1
    chosen = (lane == i1) | (lane == i2)
    earlier = (lax.broadcasted_iota(jnp.int32, (tm, tm), 0) > lax.broadcasted_iota(jnp.int32, (tm, tm), 1))
    before = _dot(earlier.astype(BF16), chosen.astype(BF16)) + counts_ref[...]
    rank1 = jnp.sum(jnp.where(lane == i1, before, 0.0), axis=-1, keepdims=True)
    rank2 = jnp.sum(jnp.where(lane == i2, before, 0.0), axis=-1, keepdims=True)
    counts_ref[...] += jnp.sum(chosen.astype(F32), axis=0, keepdims=True)
    meta = jnp.zeros(logits.shape, F32)
    for col, val in ((META_E1, i1.astype(F32)), (META_E2, i2.astype(F32)), (META_RANK1, rank1),
                     (META_RANK2, rank2), (META_P1, p1), (META_P2, p2)):
        meta = jnp.where(lane == col, val, meta)
    meta_ref[...] = meta


def _router(h, g, wr_pad, tm):
    t, d = h.shape
    return pl.pallas_call(
        _router_kernel,
        grid=(t // tm,),
        in_specs=[
            pl.BlockSpec((tm, d), lambda i: (i, 0)),
            pl.BlockSpec((1, d), lambda i: (0, 0)),
            pl.BlockSpec(wr_pad.shape, lambda i: (0, 0)),
        ],
        out_specs=[
            pl.BlockSpec((tm * SUBLANES, LANES), lambda i: (i, 0)),
            pl.BlockSpec((tm, LANES), lambda i: (i, 0)),
            pl.BlockSpec((1, LANES), lambda i: (0, 0)),
        ],
        out_shape=[
            jax.ShapeDtypeStruct((t * SUBLANES, LANES), F32),
            jax.ShapeDtypeStruct((t, LANES), F32),
            jax.ShapeDtypeStruct((1, LANES), F32),
        ],
        compiler_params=_params("arbitrary"),
        name="router",
    )(h, g, wr_pad)


def _routing_tables(meta, counts, tile_rows, n_tiles):
    cnt = counts[0, :N_EXPERTS].astype(jnp.int32)
    padded = (cnt + tile_rows - 1) // tile_rows * tile_rows
    ends = jnp.cumsum(padded)
    starts = ends - padded
    pos1 = starts[meta[:, META_E1].astype(jnp.int32)] + meta[:, META_RANK1].astype(jnp.int32)
    pos2 = starts[meta[:, META_E2].astype(jnp.int32)] + meta[:, META_RANK2].astype(jnp.int32)
    n_used = ends[-1] // tile_rows
    tile_start = jnp.minimum(jnp.arange(n_tiles, dtype=jnp.int32), n_used - 1) * tile_rows
    tile_expert = jnp.sum((tile_start[:, None] >= ends[None, :]).astype(jnp.int32), axis=-1)
    n_free = n_tiles * tile_rows - TOP_K * meta.shape[0]
    pad_ends = jnp.cumsum(padded - cnt)
    slot = jnp.arange(n_free, dtype=jnp.int32)
    owner = jnp.minimum(jnp.sum((slot[:, None] >= pad_ends[None, :]).astype(jnp.int32), axis=-1), N_EXPERTS - 1)
    free_pos = jnp.where(slot < pad_ends[-1], ends[owner] - (pad_ends[owner] - slot),
                         ends[-1] + (slot - pad_ends[-1]))
    return pos1, pos2, tile_expert, n_used.reshape(1), free_pos


def _token_rows(ref, i, n=1):
    return ref.at[pl.ds(pl.multiple_of(i * SUBLANES, SUBLANES), n * SUBLANES)]


def _dispatch_kernel(pos1_ref, pos2_ref, free_pos_ref, xn_ref, xs_hbm, zero_sc, sem, zero_sem,
                     *, chunk, free_chunk):
    step = pl.program_id(0)
    base = step * chunk
    zero_sc[...] = jnp.zeros(zero_sc.shape, F32)

    def zero_row(i, carry):
        dst = _token_rows(xs_hbm, free_pos_ref[step * free_chunk + i])
        pltpu.make_async_copy(_token_rows(zero_sc, i), dst, zero_sem).start()
        return carry

    lax.fori_loop(0, free_chunk, zero_row, 0, unroll=8)

    def issue(i, carry):
        src = _token_rows(xn_ref, i)
        pltpu.make_async_copy(src, _token_rows(xs_hbm, pos1_ref[base + i]), sem).start()
        pltpu.make_async_copy(src, _token_rows(xs_hbm, pos2_ref[base + i]), sem).start()
        return carry

    lax.fori_loop(0, chunk, issue, 0, unroll=8)
    for _ in range(TOP_K):
        pltpu.make_async_copy(xn_ref, _token_rows(xs_hbm, 0, chunk), sem).wait()
    pltpu.make_async_copy(zero_sc, _token_rows(xs_hbm, 0, free_chunk), zero_sem).wait()


def _dispatch(pos1, pos2, free_pos, xn_tiles, n_rows, chunk):
    t = pos1.shape[0]
    n_steps = t // chunk
    assert free_pos.shape[0] == n_rows - TOP_K * t and free_pos.shape[0] % n_steps == 0
    free_chunk = free_pos.shape[0] // n_steps
    return pl.pallas_call(
        functools.partial(_dispatch_kernel, chunk=chunk, free_chunk=free_chunk),
        grid_spec=pltpu.PrefetchScalarGridSpec(
            num_scalar_prefetch=3,
            grid=(n_steps,),
            in_specs=[pl.BlockSpec((chunk * SUBLANES, LANES), lambda i, *prefetch: (i, 0))],
            out_specs=pl.BlockSpec(memory_space=pl.ANY),
            scratch_shapes=[pltpu.VMEM((free_chunk * SUBLANES, LANES), F32), pltpu.SemaphoreType.DMA,
                            pltpu.SemaphoreType.DMA],
        ),
        out_shape=jax.ShapeDtypeStruct((n_rows * SUBLANES, LANES), F32),
        compiler_params=_params("arbitrary"),
        name="moe_dispatch",
    )(pos1, pos2, free_pos, xn_tiles)


def _expert_kernel(tile_expert_ref, n_used_ref, xs_ref, wg_ref, wu_ref, wd_ref, ys_ref, acc_sc, *, tm, d):
    del tile_expert_ref
    r = pl.program_id(0)
    f = pl.program_id(1)
    last = pl.num_programs(1) - 1
    used = r < n_used_ref[0]

    @pl.when((r == 0) & (f == 0))
    def _():
        acc_sc[...] = jnp.zeros(acc_sc.shape, F32)

    @pl.when(used)
    def _():
        x = jnp.concatenate(_from_token_tiles(xs_ref, tm, d), axis=-1).astype(BF16)
        a = _dot(x, wg_ref[...])
        b = _dot(x, wu_ref[...])
        y = _dot((a * _sigmoid(a) * b).astype(BF16), wd_ref[...])
        total = jnp.where(f == 0, 0.0, acc_sc[...]) + y
        acc_sc[...] = total
        _to_token_tiles(ys_ref, total)

    @pl.when(jnp.logical_not(used) & (f == last))
    def _():
        ys_ref[...] = jnp.zeros(ys_ref.shape, F32)


def _experts(tile_expert, n_used, xs, wg, wu, wd, layer, tm, tf):
    _, n_e, d, f_dim = wg.shape
    n_tiles = xs.shape[0] // (tm * SUBLANES)
    assert f_dim // tf >= 2

    def row_tile(r, f, te, nu):
        return (jnp.minimum(r, nu[0] - 1), 0)

    return pl.pallas_call(
        functools.partial(_expert_kernel, tm=tm, d=d),
        grid_spec=pltpu.PrefetchScalarGridSpec(
            num_scalar_prefetch=2,
            grid=(n_tiles, f_dim // tf),
            in_specs=[
                pl.BlockSpec((tm * SUBLANES, LANES), row_tile),
                pl.BlockSpec((None, None, d, tf), lambda r, f, te, nu: (layer, te[r], 0, f)),
                pl.BlockSpec((None, None, d, tf), lambda r, f, te, nu: (layer, te[r], 0, f)),
                pl.BlockSpec((None, None, tf, d), lambda r, f, te, nu: (layer, te[r], f, 0)),
            ],
            out_specs=pl.BlockSpec((tm * SUBLANES, LANES), lambda r, f, te, nu: (r, 0)),
            scratch_shapes=[pltpu.VMEM((tm, d), F32)],
        ),
        out_shape=jax.ShapeDtypeStruct(xs.shape, F32),
        compiler_params=_params("arbitrary", "arbitrary"),
        name="moe_experts",
    )(tile_expert, n_used, xs, wg, wu, wd)


def _combine_kernel(pos1_ref, pos2_ref, h_ref, meta_ref, ys_hbm, p_ref, g_ref, wgate_ref, wproj_ref,
                    o_ref, y_sc, sems, *, tm, d):
    i = pl.program_id(0)
    slot = i % 2

    def gather(tile, slot):
        base = tile * tm

        def issue(r, carry):
            for k, pos_ref in enumerate((pos1_ref, pos2_ref)):
                pltpu.make_async_copy(_token_rows(ys_hbm, pos_ref[base + r]),
                                      _token_rows(y_sc.at[TOP_K * slot + k], r), sems.at[slot]).start()
            return carry

        lax.fori_loop(0, tm, issue, 0, unroll=8)

    @pl.when(i == 0)
    def _():
        gather(0, 0)

    @pl.when(i + 1 < pl.num_programs(0))
    def _():
        gather(i + 1, 1 - slot)

    for k in range(TOP_K):
        pltpu.make_async_copy(_token_rows(ys_hbm, 0, tm), y_sc.at[TOP_K * slot + k], sems.at[slot]).wait()
    meta = meta_ref[...]
    p1 = meta[:, META_P1:META_P1 + 1]
    p2 = meta[:, META_P2:META_P2 + 1]
    y1 = _from_token_tiles(y_sc.at[TOP_K * slot], tm, d)
    y2 = _from_token_tiles(y_sc.at[TOP_K * slot + 1], tm, d)
    for a in range(d // LANES):
        cols = slice(a * LANES, (a + 1) * LANES)
        o_ref[:, cols] = h_ref[:, cols] + p1 * y1[a] + p2 * y2[a]
    o_ref[...] = _ple_update(o_ref[...], p_ref, g_ref, wgate_ref, wproj_ref)


def _combine(pos1, pos2, h, meta, ys, p, g, w_gate, w_proj, layer, tm):
    t, d = h.shape
    return pl.pallas_call(
        functools.partial(_combine_kernel, tm=tm, d=d),
        grid_spec=pltpu.PrefetchScalarGridSpec(
            num_scalar_prefetch=2,
            grid=(t // tm,),
            in_specs=[
                pl.BlockSpec((tm, d), lambda i, p1, p2: (i, 0)),
                pl.BlockSpec((tm, LANES), lambda i, p1, p2: (i, 0)),
                pl.BlockSpec(memory_space=pl.ANY),
            ] + _ple_specs(p, w_gate, w_proj, layer, tm, d),
            out_specs=pl.BlockSpec((tm, d), lambda i, p1, p2: (i, 0)),
            scratch_shapes=[pltpu.VMEM((2 * TOP_K, tm * SUBLANES, LANES), F32),
                            pltpu.SemaphoreType.DMA((2,))],
        ),
        out_shape=jax.ShapeDtypeStruct((t, d), F32),
        compiler_params=_params("arbitrary"),
        name="moe_combine",
    )(pos1, pos2, h, meta, ys, p, g, w_gate, w_proj)


def _ple_specs(p, w_gate, w_proj, layer, tm, d, resident_weights=False):
    def spec(block, index, make=pl.BlockSpec):
        return make(block, lambda i, *prefetch: index(i))
    weight = functools.partial(spec, make=_resident) if resident_weights else spec
    return [
        spec((None, tm, p.shape[-1]), lambda i: (layer, i, 0)),
        spec((1, d), lambda i: (0, 0)),
        weight((None,) + w_gate.shape[1:], lambda i: (layer, 0, 0)),
        weight((None,) + w_proj.shape[1:], lambda i: (layer, 0, 0)),
    ]


def _rope_tables(seq):
    rows = seq // GRID_W
    r = jnp.broadcast_to(jnp.arange(rows, dtype=F32)[:, None], (rows, GRID_W)).reshape(seq)
    c = jnp.broadcast_to(jnp.arange(GRID_W, dtype=F32)[None, :], (rows, GRID_W)).reshape(seq)
    inv = ROPE_THETA ** (-jnp.arange(0, AXIS_DIM, 2, dtype=F32) / AXIS_DIM)
    ang = jnp.concatenate([r[:, None] * inv, c[:, None] * inv], axis=-1)
    cos, sin = jnp.cos(ang), jnp.sin(ang)
    reps = LANES // HEAD_DIM
    return (jnp.tile(jnp.concatenate([cos, cos], axis=-1), (1, reps)),
            jnp.tile(jnp.concatenate([-sin, sin], axis=-1), (1, reps)))


def _tile(n, want):
    t = min(n, want)
    assert n % t == 0, (n, t)
    return t


def kernel(x, p, norm_mix, norm_ffn, w_in, q_norm, k_norm, conv_w, conv_b, conv_ln_g, conv_ln_b, w_out,
           ffn_wg, ffn_wu, ffn_wd, pool_w, pool_scale, router_w, moe_wg, moe_wu, moe_wd, ple_norm,
           ple_gate_w, ple_proj):
    batch, seq, d = x.shape
    depth = p.shape[0]
    t = batch * seq
    q_dim = N_HEADS * HEAD_DIM
    assert seq % GRID_W == 0 and d % LANES == 0

    tm = _tile(seq, 512)
    tq = _tile(seq, 256)
    tk = _tile(seq, 8192)
    tk_online = _tile(seq, 512)
    tc = _tile(seq, 256)
    tp = _tile(seq, 256)
    tf_moe = moe_wg.shape[3] // 2
    tm_moe = _tile(t, 1024)
    n_moe_tiles = TOP_K * t // tm_moe + N_EXPERTS
    dispatch_chunk = _tile(t, 2048)

    cos_t, sin_t = _rope_tables(seq)
    row = lambda v: v.reshape(1, -1)
    tile_heads = lambda v: jnp.tile(v, LANES // HEAD_DIM).reshape(1, LANES)

    bf = lambda w: w.astype(BF16)
    w_in_b, w_out_b = bf(w_in), bf(w_out)
    ffn_wg_b, ffn_wu_b, ffn_wd_b = bf(ffn_wg), bf(ffn_wu), bf(ffn_wd)
    pool_w_b = bf(pool_w)
    moe_wg_b, moe_wu_b, moe_wd_b = bf(moe_wg), bf(moe_wu), bf(moe_wd)
    ple_gate_b, ple_proj_b = bf(ple_gate_w), bf(ple_proj)
    p_rows = p.reshape(depth, t, -1)

    h = x.reshape(t, d)
    for i in range(depth):
        j = i // 2
        ple_args = (p_rows, row(ple_norm[i]), ple_gate_b, ple_proj_b, i)
        if i % 2 == 0:
            q, k, vt, u = _in_proj(h, row(norm_mix[i]), w_in_b, j, tile_heads(q_norm[j]),
                                   tile_heads(k_norm[j]), cos_t, sin_t, seq, tm)
            logit_bound = (HEAD_DIM ** 0.5 * LOG2E) * jnp.max(jnp.abs(q_norm[j])) * jnp.max(jnp.abs(k_norm[j]))
            a = _attention(q, k, vt, logit_bound, batch, seq, tq, tk, tk_online)
            c = _conv_module(u, conv_w[j], row(conv_b[j]), row(conv_ln_g[j]), row(conv_ln_b[j]),
                             batch, seq, tc)
            h = _even_tail(h, a, c, w_out_b, row(norm_ffn[i]), ffn_wg_b, ffn_wu_b, ffn_wd_b,
                           p_rows, row(ple_norm[i]), ple_gate_b, ple_proj_b, j, i, tm)
        else:
            h = _pool_mixer(h, row(norm_mix[i]), pool_w_b, j, row(pool_scale[j]), batch, seq, tp)
            wr_pad = jnp.pad(router_w[j], ((0, 0), (0, LANES - N_EXPERTS)))
            xn_tiles, meta, counts = _router(h, row(norm_ffn[i]), wr_pad, tm)
            pos1, pos2, tile_expert, n_used, free_pos = _routing_tables(meta, counts, tm_moe, n_moe_tiles)
            xs = _dispatch(pos1, pos2, free_pos, xn_tiles, n_moe_tiles * tm_moe, dispatch_chunk)
            ys = _experts(tile_expert, n_used, xs, moe_wg_b, moe_wu_b, moe_wd_b, j, tm_moe, tf_moe)
            h = _combine(pos1, pos2, h, meta, ys, *ple_args, tm)
    return h.reshape(batch, seq, d)
```

```python
import functools

import jax
import jax.numpy as jnp
from jax import lax
from jax.experimental import pallas as pl
from jax.experimental.pallas import tpu as pltpu

F32 = jnp.float32
BF16 = jnp.bfloat16

GRID_W = 64
N_HEADS = 8
KV_HEADS = 2
HEAD_DIM = 64
Q_PER_KV = N_HEADS // KV_HEADS
AXIS_DIM = HEAD_DIM // 2
ROPE_THETA = 10000.0
CONV_WIDTH = 31
POOL_WINDOWS = (2, 4, 8, 16)
N_EXPERTS = 8
TOP_K = 2
EPS = 1e-6

LANES = 128
SUBLANES = 8
VMEM_LIMIT = 56 * 1024 * 1024
BF16_SUBLANES = 16
VT_ROWS = HEAD_DIM + BF16_SUBLANES
LOG2E = 1.4426950408889634
Q_SCALE = HEAD_DIM ** -0.5 * LOG2E
MAX_UNSHIFTED_LOGIT = 80.0
CONV_HALO = 16
CONV_ROWS = 64
POOL_HALO = 8


def _params(*sem):
    return pltpu.CompilerParams(dimension_semantics=sem, vmem_limit_bytes=VMEM_LIMIT)


def _rms(x, g):
    return x * lax.rsqrt(jnp.mean(x * x, axis=-1, keepdims=True) + EPS) * g


def _sigmoid(x):
    return 1.0 / (1.0 + jnp.exp(-x))


def _dot(a, b):
    return jnp.dot(a, b, preferred_element_type=F32)


def _ple_update(h, p_ref, g_ref, wgate_ref, wproj_ref):
    gate = _sigmoid(_dot(_rms(h, g_ref[...]).astype(BF16), wgate_ref[...]))
    return h + gate * _dot(p_ref[...].astype(BF16), wproj_ref[...])


def _inproj_kernel(h_ref, g_ref, w_ref, qg_ref, kg_ref, cos_ref, sin_ref,
                   q_ref, k_ref, vt_ref, u_ref, *, q_dim, kv_dim, conv_ch):
    xn = _rms(h_ref[...], g_ref[...]).astype(BF16)
    proj = _dot(xn, w_ref[...])
    tm = proj.shape[0]
    cos = cos_ref[...]
    sin = sin_ref[...]
    lane = lax.broadcasted_iota(jnp.int32, (tm, LANES), 1)
    head0 = lane < HEAD_DIM
    first_half = (lane % HEAD_DIM) < (HEAD_DIM // 2)

    def norm_rope(x, g, scale):
        sq = x * x
        s0 = jnp.sum(jnp.where(head0, sq, 0.0), axis=-1, keepdims=True)
        s1 = jnp.sum(jnp.where(head0, 0.0, sq), axis=-1, keepdims=True)
        ms = jnp.where(head0, s0, s1) * (1.0 / HEAD_DIM)
        y = x * lax.rsqrt(ms + EPS) * g
        partner = jnp.where(first_half,
                            pltpu.roll(y, LANES - HEAD_DIM // 2, 1),
                            pltpu.roll(y, HEAD_DIM // 2, 1))
        return (y * cos + partner * sin) * scale

    for c in range(q_dim // LANES):
        x = proj[:, c * LANES:(c + 1) * LANES]
        q_ref[:, c * LANES:(c + 1) * LANES] = norm_rope(x, qg_ref[...], Q_SCALE).astype(BF16)
    sub = lax.broadcasted_iota(jnp.int32, (VT_ROWS - HEAD_DIM, tm), 0)
    ones_rows = jnp.where(sub == 0, 1.0, 0.0).astype(BF16)
    for c in range(kv_dim // LANES):
        x = proj[:, q_dim + c * LANES:q_dim + (c + 1) * LANES]
        kk = norm_rope(x, kg_ref[...], 1.0).astype(BF16)
        vv_t = proj[:, q_dim + kv_dim + c * LANES:q_dim + kv_dim + (c + 1) * LANES].T
        for j in range(LANES // HEAD_DIM):
            head = c * (LANES // HEAD_DIM) + j
            k_ref[head] = kk[:, j * HEAD_DIM:(j + 1) * HEAD_DIM]
            vt_ref[head, 0:HEAD_DIM, :] = vv_t[j * HEAD_DIM:(j + 1) * HEAD_DIM, :].astype(BF16)
            vt_ref[head, HEAD_DIM:VT_ROWS, :] = ones_rows
    u0 = q_dim + 2 * kv_dim
    u_ref[...] = (proj[:, u0:u0 + conv_ch] * _sigmoid(proj[:, u0 + conv_ch:u0 + 2 * conv_ch])).astype(BF16)


def _in_proj(h, g, w_in, layer, qg, kg, cos_t, sin_t, seq, tm):
    t, d = h.shape
    w_in_dim = w_in.shape[-1]
    q_dim = N_HEADS * HEAD_DIM
    kv_dim = KV_HEADS * HEAD_DIM
    conv_ch = (w_in_dim - q_dim - 2 * kv_dim) // 2
    n_seq_blocks = seq // tm
    kern = functools.partial(_inproj_kernel, q_dim=q_dim, kv_dim=kv_dim, conv_ch=conv_ch)
    return pl.pallas_call(
        kern,
        grid=(t // tm,),
        in_specs=[
            pl.BlockSpec((tm, d), lambda i: (i, 0)),
            pl.BlockSpec((1, d), lambda i: (0, 0)),
            pl.BlockSpec((None, d, w_in_dim), lambda i: (layer, 0, 0)),
            pl.BlockSpec((1, LANES), lambda i: (0, 0)),
            pl.BlockSpec((1, LANES), lambda i: (0, 0)),
            pl.BlockSpec((tm, LANES), lambda i: (i % n_seq_blocks, 0)),
            pl.BlockSpec((tm, LANES), lambda i: (i % n_seq_blocks, 0)),
        ],
        out_specs=[
            pl.BlockSpec((tm, q_dim), lambda i: (i, 0)),
            pl.BlockSpec((KV_HEADS, tm, HEAD_DIM), lambda i: (0, i, 0)),
            pl.BlockSpec((KV_HEADS, VT_ROWS, tm), lambda i: (0, 0, i)),
            pl.BlockSpec((tm, conv_ch), lambda i: (i, 0)),
        ],
        out_shape=[
            jax.ShapeDtypeStruct((t, q_dim), BF16),
            jax.ShapeDtypeStruct((KV_HEADS, t, HEAD_DIM), BF16),
            jax.ShapeDtypeStruct((KV_HEADS, VT_ROWS, t), BF16),
            jax.ShapeDtypeStruct((t, conv_ch), BF16),
        ],
        compiler_params=_params("parallel"),
        name="in_proj",
    )(h, g, w_in, qg, kg, cos_t, sin_t)


def _stack_query_heads(q_ref, q_sc, tq):
    for g in range(Q_PER_KV):
        q_sc[g * tq:(g + 1) * tq, :] = q_ref[:, g * HEAD_DIM:(g + 1) * HEAD_DIM]


def _unstack_query_heads(out, tq):
    return jnp.concatenate([out[g * tq:(g + 1) * tq, :] for g in range(Q_PER_KV)], axis=-1).astype(BF16)


_NT = (((1,), (1,)), ((), ()))


def _attn_unshifted_kernel(q_ref, k_ref, vt_ref, o_ref, q_sc, acc_sc, *, tq, tk, n_kv):
    _stack_query_heads(q_ref, q_sc, tq)
    acc_sc[...] = jnp.zeros(acc_sc.shape, F32)

    def body(j, carry):
        kv0 = pl.multiple_of(j * tk, tk)
        s_t = lax.dot_general(k_ref[0, pl.ds(kv0, tk), :], q_sc[...], _NT,
                              preferred_element_type=F32)
        p_t = jnp.exp2(s_t).astype(BF16)
        acc_sc[...] += _dot(vt_ref[0, :, pl.ds(kv0, tk)], p_t)
        return carry

    lax.fori_loop(0, n_kv, body, 0)
    acc = acc_sc[...]
    out_t = acc[:HEAD_DIM, :] / acc[HEAD_DIM:HEAD_DIM + 1, :]
    o_ref[...] = _unstack_query_heads(out_t.T, tq)


def _attn_online_kernel(q_ref, k_ref, vt_ref, o_ref, q_sc, m_sc, l_sc, acc_sc, *, tq, tk, n_kv):
    _stack_query_heads(q_ref, q_sc, tq)
    m_sc[...] = jnp.full(m_sc.shape, -jnp.inf, F32)
    l_sc[...] = jnp.zeros(l_sc.shape, F32)
    acc_sc[...] = jnp.zeros(acc_sc.shape, F32)

    def body(j, carry):
        kv0 = pl.multiple_of(j * tk, tk)
        s = lax.dot_general(q_sc[...], k_ref[0, pl.ds(kv0, tk), :], _NT, preferred_element_type=F32)
        m_prev = m_sc[...]
        m_new = jnp.maximum(m_prev, jnp.max(s, axis=-1, keepdims=True))
        alpha = jnp.exp2(m_prev - m_new)
        p = jnp.exp2(s - m_new)
        l_sc[...] = alpha * l_sc[...] + jnp.sum(p, axis=-1, keepdims=True)
        v_t = vt_ref[0, 0:HEAD_DIM, pl.ds(kv0, tk)]
        acc_sc[...] = alpha * acc_sc[...] + lax.dot_general(p.astype(BF16), v_t, _NT,
                                                            preferred_element_type=F32)
        m_sc[...] = m_new
        return carry

    lax.fori_loop(0, n_kv, body, 0)
    o_ref[...] = _unstack_query_heads(acc_sc[...] / l_sc[...], tq)


def _attention_call(kern, scratch, name, q, k, vt, batch, seq, tq, tk):
    t = q.shape[0]
    n_q = seq // tq
    gw = Q_PER_KV * HEAD_DIM
    return pl.pallas_call(
        functools.partial(kern, tq=tq, tk=tk, n_kv=seq // tk),
        grid=(batch, KV_HEADS, n_q),
        in_specs=[
            pl.BlockSpec((tq, gw), lambda b, h, i: (b * n_q + i, h)),
            pl.BlockSpec((1, seq, HEAD_DIM), lambda b, h, i: (h, b, 0)),
            pl.BlockSpec((1, VT_ROWS, seq), lambda b, h, i: (h, 0, b)),
        ],
        out_specs=pl.BlockSpec((tq, gw), lambda b, h, i: (b * n_q + i, h)),
        out_shape=jax.ShapeDtypeStruct((t, N_HEADS * HEAD_DIM), BF16),
        scratch_shapes=[pltpu.VMEM((Q_PER_KV * tq, HEAD_DIM), BF16)] + scratch,
        compiler_params=_params("parallel", "parallel", "parallel"),
        name=name,
    )(q, k, vt)


def _attention(q, k, vt, logit_bound, batch, seq, tq, tk, tk_online):
    m = Q_PER_KV * tq

    def unshifted(q, k, vt):
        return _attention_call(_attn_unshifted_kernel, [pltpu.VMEM((VT_ROWS, m), F32)],
                               "attention", q, k, vt, batch, seq, tq, tk)

    def online(q, k, vt):
        scratch = [pltpu.VMEM((m, 1), F32), pltpu.VMEM((m, 1), F32), pltpu.VMEM((m, HEAD_DIM), F32)]
        return _attention_call(_attn_online_kernel, scratch, "attention_online",
                               q, k, vt, batch, seq, tq, tk_online)

    return lax.cond(logit_bound <= MAX_UNSHIFTED_LOGIT, unshifted, online, q, k, vt)


def _conv_kernel(prev_ref, main_ref, next_ref, w_ref, b_ref, g_ref, beta_ref, o_ref, ext_sc, shift_sc, y_sc,
                 *, tc, n_blocks):
    i = pl.program_id(1)
    ch = main_ref.shape[1]
    prev = prev_ref[...].astype(F32)
    nxt = next_ref[...].astype(F32)
    ext_sc[0:CONV_HALO, :] = jnp.where(i == 0, 0.0, prev)
    ext_sc[CONV_HALO:CONV_HALO + tc, :] = main_ref[...].astype(F32)
    ext_sc[CONV_HALO + tc:, :] = jnp.where(i == n_blocks - 1, 0.0, nxt)
    rows = shift_sc.shape[1]
    for b in range(SUBLANES):
        shift_sc[b] = ext_sc[b:b + rows, :]
    base = CONV_HALO - CONV_WIDTH // 2
    for r0 in range(0, tc, CONV_ROWS):
        for c in range(ch // LANES):
            cols = slice(c * LANES, (c + 1) * LANES)
            acc = jnp.zeros((CONV_ROWS, LANES), F32) + b_ref[:, cols]
            for kk in range(CONV_WIDTH):
                a, b = divmod(base + kk, SUBLANES)
                row0 = a * SUBLANES + r0
                acc = acc + shift_sc[b, row0:row0 + CONV_ROWS, cols] * w_ref[kk:kk + 1, cols]
            y_sc[r0:r0 + CONV_ROWS, cols] = acc
    y = y_sc[...]
    mu = jnp.mean(y, axis=-1, keepdims=True)
    yc = y - mu
    var = jnp.mean(yc * yc, axis=-1, keepdims=True)
    z = yc * lax.rsqrt(var + EPS) * g_ref[...] + beta_ref[...]
    o_ref[...] = (z * _sigmoid(z)).astype(BF16)


def _conv_module(u, conv_w, conv_b, ln_g, ln_b, batch, seq, tc):
    t, ch = u.shape
    n_blocks = seq // tc
    hb = tc // CONV_HALO
    n_halo = seq // CONV_HALO
    kern = functools.partial(_conv_kernel, tc=tc, n_blocks=n_blocks)
    return pl.pallas_call(
        kern,
        grid=(batch, n_blocks),
        in_specs=[
            pl.BlockSpec((CONV_HALO, ch), lambda b, i: (b * n_halo + jnp.maximum(i * hb - 1, 0), 0)),
            pl.BlockSpec((tc, ch), lambda b, i: (b * n_blocks + i, 0)),
            pl.BlockSpec((CONV_HALO, ch), lambda b, i: (b * n_halo + jnp.minimum((i + 1) * hb, n_halo - 1), 0)),
            pl.BlockSpec((CONV_WIDTH, ch), lambda b, i: (0, 0)),
            pl.BlockSpec((1, ch), lambda b, i: (0, 0)),
            pl.BlockSpec((1, ch), lambda b, i: (0, 0)),
            pl.BlockSpec((1, ch), lambda b, i: (0, 0)),
        ],
        out_specs=pl.BlockSpec((tc, ch), lambda b, i: (b * n_blocks + i, 0)),
        out_shape=jax.ShapeDtypeStruct((t, ch), BF16),
        scratch_shapes=[
            pltpu.VMEM((tc + 2 * CONV_HALO, ch), F32),
            pltpu.VMEM((SUBLANES, tc + 2 * CONV_HALO - SUBLANES, ch), F32),
            pltpu.VMEM((tc, ch), F32),
        ],
        compiler_params=_params("parallel", "parallel"),
        name="conv_module",
    )(u, u, u, conv_w, conv_b, ln_g, ln_b)


def _resident(block, index_map):
    return pl.BlockSpec(block, index_map, pipeline_mode=pl.Buffered(1))


def _even_tail_kernel(h_ref, a_ref, c_ref, wa_ref, wc_ref, g_ref, wg_ref, wu_ref, wd_ref,
                      p_ref, pg_ref, wgate_ref, wproj_ref, o_ref):
    h = h_ref[...] + _dot(a_ref[...], wa_ref[...]) + _dot(c_ref[...], wc_ref[...])
    xn = _rms(h, g_ref[...]).astype(BF16)
    a = _dot(xn, wg_ref[...])
    b = _dot(xn, wu_ref[...])
    h = h + _dot((a * _sigmoid(a) * b).astype(BF16), wd_ref[...])
    o_ref[...] = _ple_update(h, p_ref, pg_ref, wgate_ref, wproj_ref)


def _even_tail(h, a, c, w_out, g, wg, wu, wd, p, pg, w_gate, w_proj, layer, ple_layer, tm):
    t, d = h.shape
    f_dim = wg.shape[-1]
    assert a.shape[1] == c.shape[1] and a.shape[1] + c.shape[1] == w_out.shape[1]
    return pl.pallas_call(
        _even_tail_kernel,
        grid=(t // tm,),
        in_specs=[
            pl.BlockSpec((tm, d), lambda i: (i, 0)),
            pl.BlockSpec((tm, a.shape[1]), lambda i: (i, 0)),
            pl.BlockSpec((tm, c.shape[1]), lambda i: (i, 0)),
            _resident((None, a.shape[1], d), lambda i: (layer, 0, 0)),
            _resident((None, c.shape[1], d), lambda i: (layer, 1, 0)),
            pl.BlockSpec((1, d), lambda i: (0, 0)),
            _resident((None, d, f_dim), lambda i: (layer, 0, 0)),
            _resident((None, d, f_dim), lambda i: (layer, 0, 0)),
            _resident((None, f_dim, d), lambda i: (layer, 0, 0)),
        ] + _ple_specs(p, w_gate, w_proj, ple_layer, tm, d, resident_weights=True),
        out_specs=pl.BlockSpec((tm, d), lambda i: (i, 0)),
        out_shape=jax.ShapeDtypeStruct((t, d), F32),
        compiler_params=_params("parallel"),
        name="out_proj_swiglu_ple",
    )(h, a, c, w_out, w_out, g, wg, wu, wd, p, pg, w_gate, w_proj)


def _pool_kernel(prev_ref, main_ref, next_ref, g_ref, w_ref, sc_ref, o_ref, *, tp, seq):
    i = pl.program_id(1)
    g = g_ref[...]
    h_main = main_ref[...]
    hn_main = _rms(h_main, g)
    ext = jnp.concatenate([_rms(prev_ref[...], g), hn_main, _rms(next_ref[...], g)], axis=0).astype(BF16)
    rows = tp + 2 * POOL_HALO
    t_pos = i * tp + lax.broadcasted_iota(jnp.int32, (tp, rows), 0)
    j_pos = i * tp - POOL_HALO + lax.broadcasted_iota(jnp.int32, (tp, rows), 1)
    in_seq = (j_pos >= 0) & (j_pos < seq)
    t_col = i * tp + lax.broadcasted_iota(jnp.int32, (tp, 1), 0)
    gc = w_ref.shape[1]
    for gi, win in enumerate(POOL_WINDOWS):
        left = win // 2
        right = win - 1 - left
        band = (in_seq & (j_pos >= t_pos - left) & (j_pos <= t_pos + right)).astype(BF16)
        cnt = jnp.minimum(t_col + right, seq - 1) - jnp.maximum(t_col - left, 0) + 1
        cols = slice(gi * gc, (gi + 1) * gc)
        win_sum = _dot(band, ext[:, cols])
        y = (win_sum / cnt.astype(F32) - hn_main[:, cols]).astype(BF16)
        o_ref[:, cols] = h_main[:, cols] + _dot(y, w_ref[gi]) * sc_ref[:, cols]


def _pool_mixer(h, g, pool_w, layer, pool_scale, batch, seq, tp):
    t, d = h.shape
    n_blocks = seq // tp
    hb = tp // POOL_HALO
    n_halo = seq // POOL_HALO
    kern = functools.partial(_pool_kernel, tp=tp, seq=seq)
    return pl.pallas_call(
        kern,
        grid=(batch, n_blocks),
        in_specs=[
            pl.BlockSpec((POOL_HALO, d), lambda b, i: (b * n_halo + jnp.maximum(i * hb - 1, 0), 0)),
            pl.BlockSpec((tp, d), lambda b, i: (b * n_blocks + i, 0)),
            pl.BlockSpec((POOL_HALO, d), lambda b, i: (b * n_halo + jnp.minimum((i + 1) * hb, n_halo - 1), 0)),
            pl.BlockSpec((1, d), lambda b, i: (0, 0)),
            pl.BlockSpec((None,) + pool_w.shape[1:], lambda b, i: (layer, 0, 0, 0)),
            pl.BlockSpec((1, d), lambda b, i: (0, 0)),
        ],
        out_specs=pl.BlockSpec((tp, d), lambda b, i: (b * n_blocks + i, 0)),
        out_shape=jax.ShapeDtypeStruct((t, d), F32),
        compiler_params=_params("parallel", "parallel"),
        name="pool_mixer",
    )(h, h, h, g, pool_w, pool_scale)


META_E1, META_E2, META_RANK1, META_RANK2, META_P1, META_P2 = range(6)


def _to_token_tiles(ref, x):
    tm, d = x.shape
    for a in range(d // LANES):
        ref[pl.ds(a, tm, stride=SUBLANES), :] = x[:, a * LANES:(a + 1) * LANES]


def _from_token_tiles(ref, tm, d):
    return [ref[pl.ds(a, tm, stride=SUBLANES), :] for a in range(d // LANES)]


def _router_kernel(h_ref, g_ref, wr_ref, xn_ref, meta_ref, counts_ref):
    @pl.when(pl.program_id(0) == 0)
    def _():
        counts_ref[...] = jnp.zeros(counts_ref.shape, F32)

    xn = _rms(h_ref[...], g_ref[...])
    _to_token_tiles(xn_ref, xn)
    logits = jnp.dot(xn, wr_ref[...], preferred_element_type=F32, precision=lax.Precision.HIGHEST)
    tm = logits.shape[0]
    lane = lax.broadcasted_iota(jnp.int32, logits.shape, 1)
    logits = jnp.where(lane < N_EXPERTS, logits, -jnp.inf)
    v1 = jnp.max(logits, axis=-1, keepdims=True)
    i1 = jnp.min(jnp.where(logits == v1, lane, LANES), axis=-1, keepdims=True)
    rest = jnp.where(lane == i1, -jnp.inf, logits)
    v2 = jnp.max(rest, axis=-1, keepdims=True)
    i2 = jnp.min(jnp.where(rest == v2, lane, LANES), axis=-1, keepdims=True)
    e2 = jnp.exp(v2 - v1)
    p1 = 1.0 / (1.0 + e2)
    p2 = e2 * p1
    chosen = (lane == i1) | (lane == i2)
    earlier = (lax.broadcasted_iota(jnp.int32, (tm, tm), 0) > lax.broadcasted_iota(jnp.int32, (tm, tm), 1))
    before = _dot(earlier.astype(BF16), chosen.astype(BF16)) + counts_ref[...]
    rank1 = jnp.sum(jnp.where(lane == i1, before, 0.0), axis=-1, keepdims=True)
    rank2 = jnp.sum(jnp.where(lane == i2, before, 0.0), axis=-1, keepdims=True)
    counts_ref[...] += jnp.sum(chosen.astype(F32), axis=0, keepdims=True)
    meta = jnp.zeros(logits.shape, F32)
    for col, val in ((META_E1, i1.astype(F32)), (META_E2, i2.astype(F32)), (META_RANK1, rank1),
                     (META_RANK2, rank2), (META_P1, p1), (META_P2, p2)):
        meta = jnp.where(lane == col, val, meta)
    meta_ref[...] = meta


def _router(h, g, wr_pad, tm):
    t, d = h.shape
    return pl.pallas_call(
        _router_kernel,
        grid=(t // tm,),
        in_specs=[
            pl.BlockSpec((tm, d), lambda i: (i, 0)),
            pl.BlockSpec((1, d), lambda i: (0, 0)),
            pl.BlockSpec(wr_pad.shape, lambda i: (0, 0)),
        ],
        out_specs=[
            pl.BlockSpec((tm * SUBLANES, LANES), lambda i: (i, 0)),
            pl.BlockSpec((tm, LANES), lambda i: (i, 0)),
            pl.BlockSpec((1, LANES), lambda i: (0, 0)),
        ],
        out_shape=[
            jax.ShapeDtypeStruct((t * SUBLANES, LANES), F32),
            jax.ShapeDtypeStruct((t, LANES), F32),
            jax.ShapeDtypeStruct((1, LANES), F32),
        ],
        compiler_params=_params("arbitrary"),
        name="router",
    )(h, g, wr_pad)


def _routing_tables(meta, counts, tile_rows, n_tiles):
    cnt = counts[0, :N_EXPERTS].astype(jnp.int32)
    padded = (cnt + tile_rows - 1) // tile_rows * tile_rows
    ends = jnp.cumsum(padded)
    starts = ends - padded
    pos1 = starts[meta[:, META_E1].astype(jnp.int32)] + meta[:, META_RANK1].astype(jnp.int32)
    pos2 = starts[meta[:, META_E2].astype(jnp.int32)] + meta[:, META_RANK2].astype(jnp.int32)
    n_used = ends[-1] // tile_rows
    tile_start = jnp.minimum(jnp.arange(n_tiles, dtype=jnp.int32), n_used - 1) * tile_rows
    tile_expert = jnp.sum((tile_start[:, None] >= ends[None, :]).astype(jnp.int32), axis=-1)
    n_free = n_tiles * tile_rows - TOP_K * meta.shape[0]
    pad_ends = jnp.cumsum(padded - cnt)
    slot = jnp.arange(n_free, dtype=jnp.int32)
    owner = jnp.minimum(jnp.sum((slot[:, None] >= pad_ends[None, :]).astype(jnp.int32), axis=-1), N_EXPERTS - 1)
    free_pos = jnp.where(slot < pad_ends[-1], ends[owner] - (pad_ends[owner] - slot),
                         ends[-1] + (slot - pad_ends[-1]))
    return pos1, pos2, tile_expert, n_used.reshape(1), free_pos


def _token_rows(ref, i, n=1):
    return ref.at[pl.ds(pl.multiple_of(i * SUBLANES, SUBLANES), n * SUBLANES)]


def _dispatch_kernel(pos1_ref, pos2_ref, free_pos_ref, xn_ref, xs_hbm, zero_sc, sem, zero_sem,
                     *, chunk, free_chunk):
    step = pl.program_id(0)
    base = step * chunk
    zero_sc[...] = jnp.zeros(zero_sc.shape, F32)

    def zero_row(i, carry):
        dst = _token_rows(xs_hbm, free_pos_ref[step * free_chunk + i])
        pltpu.make_async_copy(_token_rows(zero_sc, i), dst, zero_sem).start()
        return carry

    lax.fori_loop(0, free_chunk, zero_row, 0, unroll=8)

    def issue(i, carry):
        src = _token_rows(xn_ref, i)
        pltpu.make_async_copy(src, _token_rows(xs_hbm, pos1_ref[base + i]), sem).start(priority=0)
        pltpu.make_async_copy(src, _token_rows(xs_hbm, pos2_ref[base + i]), sem).start(priority=1)
        return carry

    lax.fori_loop(0, chunk, issue, 0, unroll=8)
    for _ in range(TOP_K):
        pltpu.make_async_copy(xn_ref, _token_rows(xs_hbm, 0, chunk), sem).wait()
    pltpu.make_async_copy(zero_sc, _token_rows(xs_hbm, 0, free_chunk), zero_sem).wait()


def _dispatch(pos1, pos2, free_pos, xn_tiles, n_rows, chunk):
    t = pos1.shape[0]
    n_steps = t // chunk
    assert free_pos.shape[0] == n_rows - TOP_K * t and free_pos.shape[0] % n_steps == 0
    free_chunk = free_pos.shape[0] // n_steps
    return pl.pallas_call(
        functools.partial(_dispatch_kernel, chunk=chunk, free_chunk=free_chunk),
        grid_spec=pltpu.PrefetchScalarGridSpec(
            num_scalar_prefetch=3,
            grid=(n_steps,),
            in_specs=[pl.BlockSpec((chunk * SUBLANES, LANES), lambda i, *prefetch: (i, 0))],
            out_specs=pl.BlockSpec(memory_space=pl.ANY),
            scratch_shapes=[pltpu.VMEM((free_chunk * SUBLANES, LANES), F32), pltpu.SemaphoreType.DMA,
                            pltpu.SemaphoreType.DMA],
        ),
        out_shape=jax.ShapeDtypeStruct((n_rows * SUBLANES, LANES), F32),
        compiler_params=_params("arbitrary"),
        name="moe_dispatch",
    )(pos1, pos2, free_pos, xn_tiles)


def _expert_kernel(tile_expert_ref, n_used_ref, xs_ref, wg_ref, wu_ref, wd_ref, ys_ref, acc_sc, *, tm, d):
    del tile_expert_ref
    r = pl.program_id(0)
    f = pl.program_id(1)
    last = pl.num_programs(1) - 1
    used = r < n_used_ref[0]

    @pl.when((r == 0) & (f == 0))
    def _():
        acc_sc[...] = jnp.zeros(acc_sc.shape, F32)

    @pl.when(used)
    def _():
        x = jnp.concatenate(_from_token_tiles(xs_ref, tm, d), axis=-1).astype(BF16)
        a = _dot(x, wg_ref[...])
        b = _dot(x, wu_ref[...])
        y = _dot((a * _sigmoid(a) * b).astype(BF16), wd_ref[...])
        total = jnp.where(f == 0, 0.0, acc_sc[...]) + y
        acc_sc[...] = total
        _to_token_tiles(ys_ref, total)

    @pl.when(jnp.logical_not(used) & (f == last))
    def _():
        ys_ref[...] = jnp.zeros(ys_ref.shape, F32)


def _experts(tile_expert, n_used, xs, wg, wu, wd, layer, tm, tf):
    _, n_e, d, f_dim = wg.shape
    n_tiles = xs.shape[0] // (tm * SUBLANES)
    assert f_dim // tf >= 2

    def row_tile(r, f, te, nu):
        return (jnp.minimum(r, nu[0] - 1), 0)

    return pl.pallas_call(
        functools.partial(_expert_kernel, tm=tm, d=d),
        grid_spec=pltpu.PrefetchScalarGridSpec(
            num_scalar_prefetch=2,
            grid=(n_tiles, f_dim // tf),
            in_specs=[
                pl.BlockSpec((tm * SUBLANES, LANES), row_tile),
                pl.BlockSpec((None, None, d, tf), lambda r, f, te, nu: (layer, te[r], 0, f)),
                pl.BlockSpec((None, None, d, tf), lambda r, f, te, nu: (layer, te[r], 0, f)),
                pl.BlockSpec((None, None, tf, d), lambda r, f, te, nu: (layer, te[r], f, 0)),
            ],
            out_specs=pl.BlockSpec((tm * SUBLANES, LANES), lambda r, f, te, nu: (r, 0)),
            scratch_shapes=[pltpu.VMEM((tm, d), F32)],
        ),
        out_shape=jax.ShapeDtypeStruct(xs.shape, F32),
        compiler_params=_params("arbitrary", "arbitrary"),
        name="moe_experts",
    )(tile_expert, n_used, xs, wg, wu, wd)


def _combine_kernel(pos1_ref, pos2_ref, h_ref, meta_ref, ys_hbm, p_ref, g_ref, wgate_ref, wproj_ref,
                    o_ref, y_sc, sems, *, tm, d):
    i = pl.program_id(0)
    slot = i % 2

    def gather(tile, slot):
        base = tile * tm

        def issue(r, carry):
            for k, pos_ref in enumerate((pos1_ref, pos2_ref)):
                pltpu.make_async_copy(_token_rows(ys_hbm, pos_ref[base + r]),
                                      _token_rows(y_sc.at[TOP_K * slot + k], r), sems.at[slot]).start(priority=k)
            return carry

        lax.fori_loop(0, tm, issue, 0, unroll=8)

    @pl.when(i == 0)
    def _():
        gather(0, 0)

    @pl.when(i + 1 < pl.num_programs(0))
    def _():
        gather(i + 1, 1 - slot)

    for k in range(TOP_K):
        pltpu.make_async_copy(_token_rows(ys_hbm, 0, tm), y_sc.at[TOP_K * slot + k], sems.at[slot]).wait()
    meta = meta_ref[...]
    p1 = meta[:, META_P1:META_P1 + 1]
    p2 = meta[:, META_P2:META_P2 + 1]
    y1 = _from_token_tiles(y_sc.at[TOP_K * slot], tm, d)
    y2 = _from_token_tiles(y_sc.at[TOP_K * slot + 1], tm, d)
    for a in range(d // LANES):
        cols = slice(a * LANES, (a + 1) * LANES)
        o_ref[:, cols] = h_ref[:, cols] + p1 * y1[a] + p2 * y2[a]
    o_ref[...] = _ple_update(o_ref[...], p_ref, g_ref, wgate_ref, wproj_ref)


def _combine(pos1, pos2, h, meta, ys, p, g, w_gate, w_proj, layer, tm):
    t, d = h.shape
    return pl.pallas_call(
        functools.partial(_combine_kernel, tm=tm, d=d),
        grid_spec=pltpu.PrefetchScalarGridSpec(
            num_scalar_prefetch=2,
            grid=(t // tm,),
            in_specs=[
                pl.BlockSpec((tm, d), lambda i, p1, p2: (i, 0)),
                pl.BlockSpec((tm, LANES), lambda i, p1, p2: (i, 0)),
                pl.BlockSpec(memory_space=pl.ANY),
            ] + _ple_specs(p, w_gate, w_proj, layer, tm, d),
            out_specs=pl.BlockSpec((tm, d), lambda i, p1, p2: (i, 0)),
            scratch_shapes=[pltpu.VMEM((2 * TOP_K, tm * SUBLANES, LANES), F32),
                            pltpu.SemaphoreType.DMA((2,))],
        ),
        out_shape=jax.ShapeDtypeStruct((t, d), F32),
        compiler_params=_params("arbitrary"),
        name="moe_combine",
    )(pos1, pos2, h, meta, ys, p, g, w_gate, w_proj)


def _ple_specs(p, w_gate, w_proj, layer, tm, d, resident_weights=False):
    def spec(block, index, make=pl.BlockSpec):
        return make(block, lambda i, *prefetch: index(i))
    weight = functools.partial(spec, make=_resident) if resident_weights else spec
    return [
        spec((None, tm, p.shape[-1]), lambda i: (layer, i, 0)),
        spec((1, d), lambda i: (0, 0)),
        weight((None,) + w_gate.shape[1:], lambda i: (layer, 0, 0)),
        weight((None,) + w_proj.shape[1:], lambda i: (layer, 0, 0)),
    ]


def _rope_tables(seq):
    rows = seq // GRID_W
    r = jnp.broadcast_to(jnp.arange(rows, dtype=F32)[:, None], (rows, GRID_W)).reshape(seq)
    c = jnp.broadcast_to(jnp.arange(GRID_W, dtype=F32)[None, :], (rows, GRID_W)).reshape(seq)
    inv = ROPE_THETA ** (-jnp.arange(0, AXIS_DIM, 2, dtype=F32) / AXIS_DIM)
    ang = jnp.concatenate([r[:, None] * inv, c[:, None] * inv], axis=-1)
    cos, sin = jnp.cos(ang), jnp.sin(ang)
    reps = LANES // HEAD_DIM
    return (jnp.tile(jnp.concatenate([cos, cos], axis=-1), (1, reps)),
            jnp.tile(jnp.concatenate([-sin, sin], axis=-1), (1, reps)))


def _tile(n, want):
    t = min(n, want)
    assert n % t == 0, (n, t)
    return t


def kernel(x, p, norm_mix, norm_ffn, w_in, q_norm, k_norm, conv_w, conv_b, conv_ln_g, conv_ln_b, w_out,
           ffn_wg, ffn_wu, ffn_wd, pool_w, pool_scale, router_w, moe_wg, moe_wu, moe_wd, ple_norm,
           ple_gate_w, ple_proj):
    batch, seq, d = x.shape
    depth = p.shape[0]
    t = batch * seq
    q_dim = N_HEADS * HEAD_DIM
    assert seq % GRID_W == 0 and d % LANES == 0

    tm = _tile(seq, 512)
    tq = _tile(seq, 256)
    tk = _tile(seq, 8192)
    tk_online = _tile(seq, 512)
    tc = _tile(seq, 256)
    tp = _tile(seq, 256)
    tf_moe = moe_wg.shape[3] // 2
    tm_moe = _tile(t, 1024)
    n_moe_tiles = TOP_K * t // tm_moe + N_EXPERTS
    dispatch_chunk = _tile(t, 2048)

    cos_t, sin_t = _rope_tables(seq)
    row = lambda v: v.reshape(1, -1)
    tile_heads = lambda v: jnp.tile(v, LANES // HEAD_DIM).reshape(1, LANES)

    bf = lambda w: w.astype(BF16)
    w_in_b, w_out_b = bf(w_in), bf(w_out)
    ffn_wg_b, ffn_wu_b, ffn_wd_b = bf(ffn_wg), bf(ffn_wu), bf(ffn_wd)
    pool_w_b = bf(pool_w)
    moe_wg_b, moe_wu_b, moe_wd_b = bf(moe_wg), bf(moe_wu), bf(moe_wd)
    ple_gate_b, ple_proj_b = bf(ple_gate_w), bf(ple_proj)
    p_rows = p.reshape(depth, t, -1)

    h = x.reshape(t, d)
    for i in range(depth):
        j = i // 2
        ple_args = (p_rows, row(ple_norm[i]), ple_gate_b, ple_proj_b, i)
        if i % 2 == 0:
            q, k, vt, u = _in_proj(h, row(norm_mix[i]), w_in_b, j, tile_heads(q_norm[j]),
                                   tile_heads(k_norm[j]), cos_t, sin_t, seq, tm)
            logit_bound = (HEAD_DIM ** 0.5 * LOG2E) * jnp.max(jnp.abs(q_norm[j])) * jnp.max(jnp.abs(k_norm[j]))
            a = _attention(q, k, vt, logit_bound, batch, seq, tq, tk, tk_online)
            c = _conv_module(u, conv_w[j], row(conv_b[j]), row(conv_ln_g[j]), row(conv_ln_b[j]),
                             batch, seq, tc)
            h = _even_tail(h, a, c, w_out_b, row(norm_ffn[i]), ffn_wg_b, ffn_wu_b, ffn_wd_b,
                           p_rows, row(ple_norm[i]), ple_gate_b, ple_proj_b, j, i, tm)
        else:
            h = _pool_mixer(h, row(norm_mix[i]), pool_w_b, j, row(pool_scale[j]), batch, seq, tp)
            wr_pad = jnp.pad(router_w[j], ((0, 0), (0, LANES - N_EXPERTS)))
            xn_tiles, meta, counts = _router(h, row(norm_ffn[i]), wr_pad, tm)
            pos1, pos2, tile_expert, n_used, free_pos = _routing_tables(meta, counts, tm_moe, n_moe_tiles)
            xs = _dispatch(pos1, pos2, free_pos, xn_tiles, n_moe_tiles * tm_moe, dispatch_chunk)
            ys = _experts(tile_expert, n_used, xs, moe_wg_b, moe_wu_b, moe_wd_b, j, tm_moe, tf_moe)
            h = _combine(pos1, pos2, h, meta, ys, *ple_args, tm)
    return h.reshape(batch, seq, d)
```
